```python
import math
import jax
import jax.numpy as jnp
from jax import lax
import numpy as np

D_MODEL = 2048
BATCH = 4
SEQ = 2048
DEPTH = 2
DEC_BATCH = 32
DEC_SEQ = 32
PAST_LEN = 1024

CHUNK = 64
Q_BLOCK = 128
RMS_EPS = 1e-6
ROPE_THETA = 10000.0
NEG_INF = -1e30

S5_WIDTH = 1024
S5_GROUP_CH = 16
S5_GROUPS = S5_WIDTH // S5_GROUP_CH
S5_STATE = 64
S5_DT_MIN = 0.001
S5_DT_MAX = 0.1

MLA_HEADS = 8
MLA_NOPE = 128
MLA_ROPE = 64
MLA_QK = MLA_NOPE + MLA_ROPE
MLA_V = 128
MLA_Q_LORA = 512
MLA_KV_LORA = 512
MLA_WIDTH = MLA_HEADS * MLA_V

RET_HEADS = 4
RET_DK = 128
RET_DV = 256
RET_WIDTH = RET_HEADS * RET_DV

N_BRANCH = 3
BRANCH_WIDTH = 1024

N_MEM = 256
MEM_HEADS = 4
MEM_HEAD_DIM = 128
MEM_WIDTH = MEM_HEADS * MEM_HEAD_DIM

MOE_GROUPS = 4
MOE_PER_GROUP = 8
N_EXPERTS = MOE_GROUPS * MOE_PER_GROUP
MOE_TOP_K = 2
EXPERT_FF = 512
ROUTE_BLOCK = 64

IN_SIZES = (S5_WIDTH, MLA_Q_LORA, MLA_KV_LORA, MLA_ROPE, RET_HEADS * RET_DK, RET_HEADS * RET_DK,
            RET_WIDTH, RET_WIDTH, N_BRANCH * D_MODEL)
IN_COLS = sum(IN_SIZES)
IN_SPLIT_POINTS = tuple(int(c) for c in np.cumsum(IN_SIZES)[:-1])

kernel_name = 'hybrid_streaming_s5_mla_retention_hmoe_step'


def _rms(x, g):
    xf = x.astype(jnp.float32)
    y = xf * lax.rsqrt(jnp.mean(xf * xf, axis=-1, keepdims=True) + RMS_EPS)
    return y.astype(x.dtype) * g


def _head_norm(o):
    mu = jnp.mean(o, axis=-1, keepdims=True)
    c = o - mu
    return c * lax.rsqrt(jnp.mean(c * c, axis=-1, keepdims=True) + RMS_EPS)


def _rope(x, pos):
    d = x.shape[-1]
    inv_freq = ROPE_THETA ** (-jnp.arange(0, d, 2, dtype=jnp.float32) / d)
    ang = pos.astype(jnp.float32)[:, None] * inv_freq[None, :]
    cos = jnp.cos(ang)[None, :, None, :]
    sin = jnp.sin(ang)[None, :, None, :]
    x1, x2 = jnp.split(x.astype(jnp.float32), 2, axis=-1)
    return jnp.concatenate([x1 * cos - x2 * sin, x1 * sin + x2 * cos], axis=-1).astype(x.dtype)


def _rope_tail(x, pos):
    nope, pe = jnp.split(x, [x.shape[-1] - MLA_ROPE], axis=-1)
    return jnp.concatenate([nope, _rope(pe, pos)], axis=-1)


def _linear_combine(earlier, later):
    a_e, b_e = earlier
    a_l, b_l = later
    return a_l * a_e, a_l * b_e + b_l


def _s5_branch(u, h0_re, h0_im, lp):
    B, L, _ = u.shape
    ug = u.astype(jnp.float32).reshape(B, L, S5_GROUPS, S5_GROUP_CH)
    lam = lax.complex(lp['s5_a_re'].astype(jnp.float32), lp['s5_a_im'].astype(jnp.float32))
    dt = jnp.exp(lp['s5_log_dt'].astype(jnp.float32))[:, None]
    lam_bar = jnp.exp(lam * dt)
    b = lax.complex(lp['s5_b_re'].astype(jnp.float32), lp['s5_b_im'].astype(jnp.float32))
    b_bar = ((lam_bar - 1.0) / lam)[:, :, None] * b
    bu = jnp.einsum('gph,blgh->blgp', b_bar, ug.astype(jnp.complex64))
    a = jnp.broadcast_to(lam_bar, bu.shape)
    a_cum, h = lax.associative_scan(_linear_combine, (a, bu), axis=1)
    h = h + a_cum * lax.complex(h0_re.astype(jnp.float32), h0_im.astype(jnp.float32))[:, None]
    c = lax.complex(lp['s5_c_re'].astype(jnp.float32), lp['s5_c_im'].astype(jnp.float32))
    y = jnp.einsum('ghp,blgp->blgh', c, h).real + lp['s5_d'].astype(jnp.float32) * ug
    z = jax.nn.gelu(y.reshape(B, L, S5_WIDTH))
    out = z * jax.nn.sigmoid(z @ lp['s5_w_glu'].astype(jnp.float32))
    h_last = h[:, -1]
    return out, h_last.real, h_last.imag


def _chunk_attention(q, k, v, q_pos, k_pos):
    s = jnp.einsum('bqhd,bkhd->bhqk', q, k, preferred_element_type=jnp.float32) * (q.shape[-1] ** -0.5)
    visible = (k_pos[None, :] // CHUNK) <= (q_pos[:, None] // CHUNK)
    p = jax.nn.softmax(jnp.where(visible[None, None], s, NEG_INF), axis=-1)
    return jnp.einsum('bhqk,bkhd->bqhd', p.astype(v.dtype), v)


def _blockwise_attention(q, k, v, pos):
    B, L, H, dq = q.shape
    nb = L // Q_BLOCK
    q_blocks = jnp.moveaxis(q.reshape(B, nb, Q_BLOCK, H, dq), 1, 0)
    p_blocks = pos.reshape(nb, Q_BLOCK)
    out = lax.map(lambda qp: _chunk_attention(qp[0], k, v, qp[1], pos), (q_blocks, p_blocks))
    return jnp.moveaxis(out, 0, 1).reshape(B, L, H, v.shape[-1])


def _mla_branch(q_lat, kv_lat, k_pe, pos, lp, past_ckv, past_kpe):
    B, L, _ = q_lat.shape
    c_kv = _rms(kv_lat, lp['mla_kv_norm'])
    q = (_rms(q_lat, lp['mla_q_norm']) @ lp['mla_w_uq']).reshape(B, L, MLA_HEADS, MLA_QK)
    if past_ckv is None:
        keys_c, keys_pe, k_pos = c_kv, k_pe, pos
    else:
        keys_c = jnp.concatenate([past_ckv, c_kv], axis=1)
        keys_pe = jnp.concatenate([past_kpe, k_pe], axis=1)
        k_pos = jnp.arange(keys_c.shape[1], dtype=jnp.int32)
    Lk = keys_c.shape[1]
    kv = (keys_c @ lp['mla_w_ukv']).reshape(B, Lk, MLA_HEADS, MLA_NOPE + MLA_V)
    k_nope, v = jnp.split(kv, [MLA_NOPE], axis=-1)
    k = jnp.concatenate([k_nope, jnp.broadcast_to(keys_pe[:, :, None, :], (B, Lk, MLA_HEADS, MLA_ROPE))], axis=-1)
    q = _rope_tail(_rms(q, lp['mla_q_gain']), pos)
    k = _rope_tail(_rms(k, lp['mla_k_gain']), k_pos)
    if past_ckv is None:
        o = _blockwise_attention(q, k, v, pos)
    else:
        o = _chunk_attention(q, k, v, pos, k_pos)
    return o.reshape(B, L, MLA_WIDTH), c_kv


def _retention(q, k, v, s0, block):
    B, L, H, _ = q.shape
    dv = v.shape[-1]
    n = L // block
    log_g = jnp.log1p(-jnp.exp2(-5.0 - jnp.arange(H, dtype=jnp.float32)))
    idx = jnp.arange(block, dtype=jnp.float32)
    diff = idx[:, None] - idx[None, :]
    decay = jnp.where(diff >= 0, jnp.exp(log_g[:, None, None] * jnp.maximum(diff, 0.0)), 0.0)
    q_decay = jnp.exp(log_g[:, None] * (idx + 1.0))[:, :, None]
    k_decay = jnp.exp(log_g[:, None] * (block - 1.0 - idx))[:, :, None]
    blk_decay = jnp.exp(log_g * block)[:, None, None]

    def to_blocks(t):
        return t.astype(jnp.float32).reshape(B, n, block, H, t.shape[-1]).transpose(1, 0, 3, 2, 4)

    def step(s, blk):
        qb, kb, vb = blk
        att = jnp.einsum('bhqd,bhkd->bhqk', qb, kb) * decay
        o = jnp.einsum('bhqk,bhkv->bhqv', att, vb) + jnp.einsum('bhqd,bhdv->bhqv', qb * q_decay, s)
        s = blk_decay * s + jnp.einsum('bhkd,bhkv->bhdv', kb * k_decay, vb)
        return s, o

    s_last, o = lax.scan(step, s0.astype(jnp.float32), (to_blocks(q), to_blocks(k), to_blocks(v)))
    return o.transpose(1, 0, 3, 2, 4).reshape(B, L, H, dv), s_last


def _retention_branch(r_q, r_k, r_v, r_g, pos, s0, block):
    B, L, _ = r_q.shape
    q = _rope(r_q.reshape(B, L, RET_HEADS, RET_DK), pos) * (RET_DK ** -0.5)
    k = _rope(r_k.reshape(B, L, RET_HEADS, RET_DK), pos)
    v = r_v.reshape(B, L, RET_HEADS, RET_DV)
    o, s_last = _retention(q, k, v, s0, block)
    o = _head_norm(o).reshape(B, L, RET_WIDTH) * jax.nn.silu(r_g.astype(jnp.float32))
    return o, s_last


def _mixer_block(xn, pos, lp, past):
    B, L, _ = xn.shape
    u_s5, q_lat, kv_lat, k_pe, r_q, r_k, r_v, r_g, g_mix = jnp.split(xn @ lp['w_in'], IN_SPLIT_POINTS, axis=-1)
    if past is None:
        h0_re = jnp.zeros((B, S5_GROUPS, S5_STATE), jnp.float32)
        h0_im = h0_re
        ret_s0 = jnp.zeros((B, RET_HEADS, RET_DK, RET_DV), jnp.float32)
        past_ckv = past_kpe = None
        block = CHUNK
    else:
        past_ckv, past_kpe, h0_re, h0_im, ret_s0 = past
        block = L
    o_s5, s5_re, s5_im = _s5_branch(u_s5, h0_re, h0_im, lp)
    o_mla, c_kv = _mla_branch(q_lat, kv_lat, k_pe, pos, lp, past_ckv, past_kpe)
    o_ret, ret_state = _retention_branch(r_q, r_k, r_v, r_g, pos, ret_s0, block)
    branches = jnp.stack([o_s5.astype(xn.dtype), o_mla, o_ret.astype(xn.dtype)], axis=2)
    proj = jnp.einsum('blnc,ncd->blnd', branches, lp['w_branch'])
    gates = jax.nn.sigmoid(g_mix.reshape(B, L, N_BRANCH, D_MODEL))
    out = jnp.sum(gates * proj, axis=2) @ lp['w_out']
    return out, (c_kv, k_pe, s5_re, s5_im, ret_state)


def _memory_kv(mem, lp):
    B, M, _ = mem.shape
    mn = _rms(mem, lp['norm_mem'])
    k = _rms((mn @ lp['w_ck']).reshape(B, M, MEM_HEADS, MEM_HEAD_DIM), lp['cross_k_gain'])
    v = (mn @ lp['w_cv']).reshape(B, M, MEM_HEADS, MEM_HEAD_DIM)
    return k, v


def _cross_attention(xn, mem_k, mem_v, lp):
    B, L, _ = xn.shape
    q = _rms((xn @ lp['w_cq']).reshape(B, L, MEM_HEADS, MEM_HEAD_DIM), lp['cross_q_gain'])
    s = jnp.einsum('blhd,bmhd->bhlm', q, mem_k, preferred_element_type=jnp.float32) * (MEM_HEAD_DIM ** -0.5)
    p = jax.nn.softmax(s, axis=-1)
    o = jnp.einsum('bhlm,bmhd->blhd', p.astype(mem_v.dtype), mem_v).reshape(B, L, MEM_WIDTH)
    return o @ lp['w_co']


def _grouped_experts(xf, expert_idx, gate, w_gate_up, w_down):
    N, D = xf.shape
    M = N * MOE_TOP_K
    flat_e = expert_idx.reshape(M)
    order = jnp.argsort(flat_e)
    sorted_e = flat_e[order]
    counts = jnp.bincount(flat_e, length=N_EXPERTS)
    starts = jnp.cumsum(counts) - counts
    padded = (counts + ROUTE_BLOCK - 1) // ROUTE_BLOCK * ROUTE_BLOCK
    pad_ends = jnp.cumsum(padded)
    pad_starts = pad_ends - padded
    dest = pad_starts[sorted_e] + jnp.arange(M) - starts[sorted_e]
    n_blocks = -(-(M + N_EXPERTS * (ROUTE_BLOCK - 1)) // ROUTE_BLOCK)
    rows = jnp.zeros((n_blocks * ROUTE_BLOCK, D), xf.dtype).at[dest].set(xf[order // MOE_TOP_K])
    block_e = jnp.minimum(jnp.searchsorted(pad_ends, jnp.arange(n_blocks) * ROUTE_BLOCK, side='right'), N_EXPERTS - 1)

    def expert_block(args):
        xb, e = args
        g, u = jnp.split(xb @ w_gate_up[e], 2, axis=-1)
        return (jax.nn.silu(g) * u) @ w_down[e]

    out = lax.map(expert_block, (rows.reshape(n_blocks, ROUTE_BLOCK, D), block_e)).reshape(-1, D)
    y_flat = jnp.zeros((M, D), out.dtype).at[order].set(out[dest])
    return jnp.sum(y_flat.reshape(N, MOE_TOP_K, D) * gate[..., None], axis=1)


def _hier_moe(xn, lp):
    B, L, D = xn.shape
    xf = xn.reshape(B * L, D)
    g_logits = (xf @ lp['w_group_router']).astype(jnp.float32) + lp['b_group'].astype(jnp.float32)
    g_prob = jax.nn.softmax(g_logits, axis=-1)
    _, g_sel = lax.top_k(g_logits, 1)
    p_group = jnp.take_along_axis(g_prob, g_sel, axis=-1)
    e_logits = ((xf @ lp['w_expert_router']).astype(jnp.float32)
                + lp['b_expert'].astype(jnp.float32)).reshape(-1, MOE_GROUPS, MOE_PER_GROUP)
    in_group = e_logits[jnp.arange(xf.shape[0]), g_sel[:, 0]]
    e_top, e_sel = lax.top_k(in_group, MOE_TOP_K)
    gate = p_group * jax.nn.softmax(e_top, axis=-1)
    expert_idx = g_sel * MOE_PER_GROUP + e_sel
    y = _grouped_experts(xf, expert_idx, gate, lp['w_gate_up'], lp['w_down'])
    return y.reshape(B, L, D).astype(xn.dtype)


def _layer(x, pos, mem_k, mem_v, lp, past):
    mix, new_state = _mixer_block(_rms(x, lp['norm_mix']), pos, lp, past)
    x = x + mix
    x = x + _cross_attention(_rms(x, lp['norm_cross']), mem_k, mem_v, lp)
    x = x + _hier_moe(_rms(x, lp['norm_ffn']), lp)
    return x, new_state


def setup_inputs(seed: int = 0) -> dict:
    key = jax.random.key(seed)
    counter = [0]

    def nxt():
        counter[0] += 1
        return jax.random.fold_in(key, counter[0])

    def nrm(shape, scale=1.0):
        return jax.random.normal(nxt(), shape, jnp.float32) * scale

    def gain(shape):
        return 1.0 + 0.02 * nrm(shape)

    a_im0 = jnp.pi * jnp.arange(S5_STATE, dtype=jnp.float32)
    return {
        'x_prompt': nrm((BATCH, SEQ, D_MODEL)),
        'x_sample': nrm((DEC_BATCH, DEC_SEQ, D_MODEL)),
        'mem_prompt': nrm((BATCH, N_MEM, D_MODEL)),
        'cache_mla_ckv': nrm((DEPTH, DEC_BATCH, PAST_LEN, MLA_KV_LORA)),
        'cache_mla_kpe': nrm((DEPTH, DEC_BATCH, PAST_LEN, MLA_ROPE)),
        'state_s5_re': nrm((DEPTH, DEC_BATCH, S5_GROUPS, S5_STATE), 0.1),
        'state_s5_im': nrm((DEPTH, DEC_BATCH, S5_GROUPS, S5_STATE), 0.1),
        'state_ret': nrm((DEPTH, DEC_BATCH, RET_HEADS, RET_DK, RET_DV), 2.0),
        'cache_mem_k': nrm((DEPTH, DEC_BATCH, N_MEM, MEM_HEADS, MEM_HEAD_DIM)),
        'cache_mem_v': nrm((DEPTH, DEC_BATCH, N_MEM, MEM_HEADS, MEM_HEAD_DIM)),
        'norm_mix': gain((DEPTH, D_MODEL)),
        'w_in': nrm((DEPTH, D_MODEL, IN_COLS), D_MODEL ** -0.5),
        's5_a_re': -0.5 + 0.01 * nrm((DEPTH, S5_GROUPS, S5_STATE)),
        's5_a_im': a_im0 + 0.01 * nrm((DEPTH, S5_GROUPS, S5_STATE)),
        's5_log_dt': jax.random.uniform(nxt(), (DEPTH, S5_GROUPS), jnp.float32,
                                        math.log(S5_DT_MIN), math.log(S5_DT_MAX)),
        's5_b_re': nrm((DEPTH, S5_GROUPS, S5_STATE, S5_GROUP_CH), (2 * S5_GROUP_CH) ** -0.5),
        's5_b_im': nrm((DEPTH, S5_GROUPS, S5_STATE, S5_GROUP_CH), (2 * S5_GROUP_CH) ** -0.5),
        's5_c_re': nrm((DEPTH, S5_GROUPS, S5_GROUP_CH, S5_STATE), S5_STATE ** -0.5),
        's5_c_im': nrm((DEPTH, S5_GROUPS, S5_GROUP_CH, S5_STATE), S5_STATE ** -0.5),
        's5_d': nrm((DEPTH, S5_GROUPS, S5_GROUP_CH)),
        's5_w_glu': nrm((DEPTH, S5_WIDTH, S5_WIDTH), S5_WIDTH ** -0.5),
        'mla_q_norm': gain((DEPTH, MLA_Q_LORA)),
        'mla_w_uq': nrm((DEPTH, MLA_Q_LORA, MLA_HEADS * MLA_QK), MLA_Q_LORA ** -0.5),
        'mla_kv_norm': gain((DEPTH, MLA_KV_LORA)),
        'mla_w_ukv': nrm((DEPTH, MLA_KV_LORA, MLA_HEADS * (MLA_NOPE + MLA_V)), MLA_KV_LORA ** -0.5),
        'mla_q_gain': gain((DEPTH, MLA_QK)),
        'mla_k_gain': gain((DEPTH, MLA_QK)),
        'w_branch': nrm((DEPTH, N_BRANCH, BRANCH_WIDTH, D_MODEL), BRANCH_WIDTH ** -0.5),
        'w_out': nrm((DEPTH, D_MODEL, D_MODEL), D_MODEL ** -0.5),
        'norm_cross': gain((DEPTH, D_MODEL)),
        'norm_mem': gain((DEPTH, D_MODEL)),
        'w_cq': nrm((DEPTH, D_MODEL, MEM_WIDTH), D_MODEL ** -0.5),
        'w_ck': nrm((DEPTH, D_MODEL, MEM_WIDTH), D_MODEL ** -0.5),
        'w_cv': nrm((DEPTH, D_MODEL, MEM_WIDTH), D_MODEL ** -0.5),
        'cross_q_gain': gain((DEPTH, MEM_HEAD_DIM)),
        'cross_k_gain': gain((DEPTH, MEM_HEAD_DIM)),
        'w_co': nrm((DEPTH, MEM_WIDTH, D_MODEL), MEM_WIDTH ** -0.5),
        'norm_ffn': gain((DEPTH, D_MODEL)),
        'w_group_router': nrm((DEPTH, D_MODEL, MOE_GROUPS), D_MODEL ** -0.5),
        'b_group': nrm((DEPTH, MOE_GROUPS), 0.01),
        'w_expert_router': nrm((DEPTH, D_MODEL, N_EXPERTS), D_MODEL ** -0.5),
        'b_expert': nrm((DEPTH, N_EXPERTS), 0.01),
        'w_gate_up': nrm((DEPTH, N_EXPERTS, D_MODEL, 2 * EXPERT_FF), D_MODEL ** -0.5),
        'w_down': nrm((DEPTH, N_EXPERTS, EXPERT_FF, D_MODEL), EXPERT_FF ** -0.5),
    }


def reference(x_prompt, x_sample, mem_prompt,
              cache_mla_ckv, cache_mla_kpe, state_s5_re, state_s5_im, state_ret, cache_mem_k, cache_mem_v,
              norm_mix, w_in, s5_a_re, s5_a_im, s5_log_dt, s5_b_re, s5_b_im, s5_c_re, s5_c_im, s5_d, s5_w_glu,
              mla_q_norm, mla_w_uq, mla_kv_norm, mla_w_ukv, mla_q_gain, mla_k_gain,
              w_branch, w_out,
              norm_cross, norm_mem, w_cq, w_ck, w_cv, cross_q_gain, cross_k_gain, w_co,
              norm_ffn, w_group_router, b_group, w_expert_router, b_expert, w_gate_up, w_down):
    past_len = cache_mla_ckv.shape[2]
    pos_p = jnp.arange(x_prompt.shape[1], dtype=jnp.int32)
    pos_s = past_len + jnp.arange(x_sample.shape[1], dtype=jnp.int32)
    yp, ys = x_prompt, x_sample
    p_ckv, p_kpe, p_s5_re, p_s5_im, p_ret, p_mem_k, p_mem_v = [], [], [], [], [], [], []
    s_ckv, s_kpe, s_s5_re, s_s5_im, s_ret = [], [], [], [], []
    for l in range(DEPTH):
        lp = {
            'norm_mix': norm_mix[l], 'w_in': w_in[l],
            's5_a_re': s5_a_re[l], 's5_a_im': s5_a_im[l], 's5_log_dt': s5_log_dt[l],
            's5_b_re': s5_b_re[l], 's5_b_im': s5_b_im[l], 's5_c_re': s5_c_re[l], 's5_c_im': s5_c_im[l],
            's5_d': s5_d[l], 's5_w_glu': s5_w_glu[l],
            'mla_q_norm': mla_q_norm[l], 'mla_w_uq': mla_w_uq[l], 'mla_kv_norm': mla_kv_norm[l],
            'mla_w_ukv': mla_w_ukv[l], 'mla_q_gain': mla_q_gain[l], 'mla_k_gain': mla_k_gain[l],
            'w_branch': w_branch[l], 'w_out': w_out[l],
            'norm_cross': norm_cross[l], 'norm_mem': norm_mem[l], 'w_cq': w_cq[l], 'w_ck': w_ck[l],
            'w_cv': w_cv[l], 'cross_q_gain': cross_q_gain[l], 'cross_k_gain': cross_k_gain[l], 'w_co': w_co[l],
            'norm_ffn': norm_ffn[l], 'w_group_router': w_group_router[l], 'b_group': b_group[l],
            'w_expert_router': w_expert_router[l], 'b_expert': b_expert[l],
            'w_gate_up': w_gate_up[l], 'w_down': w_down[l],
        }
        mk_p, mv_p = _memory_kv(mem_prompt, lp)
        yp, st_p = _layer(yp, pos_p, mk_p, mv_p, lp, None)
        ys, st_s = _layer(ys, pos_s, cache_mem_k[l], cache_mem_v[l], lp,
                          (cache_mla_ckv[l], cache_mla_kpe[l], state_s5_re[l], state_s5_im[l], state_ret[l]))
        p_ckv.append(st_p[0]); p_kpe.append(st_p[1]); p_s5_re.append(st_p[2]); p_s5_im.append(st_p[3])
        p_ret.append(st_p[4]); p_mem_k.append(mk_p); p_mem_v.append(mv_p)
        s_ckv.append(st_s[0]); s_kpe.append(st_s[1]); s_s5_re.append(st_s[2]); s_s5_im.append(st_s[3])
        s_ret.append(st_s[4])
    return (yp, ys,
            jnp.stack(p_ckv), jnp.stack(p_kpe), jnp.stack(p_s5_re), jnp.stack(p_s5_im), jnp.stack(p_ret),
            jnp.stack(p_mem_k), jnp.stack(p_mem_v),
            jnp.stack(s_ckv), jnp.stack(s_kpe), jnp.stack(s_s5_re), jnp.stack(s_s5_im), jnp.stack(s_ret))
```

```python
import functools
import math

import jax
import jax.numpy as jnp
from jax import lax
from jax.experimental import pallas as pl
from jax.experimental.pallas import tpu as pltpu

F32 = jnp.float32
BF16 = jnp.bfloat16

D_MODEL = 2048
BATCH = 4
SEQ = 2048
DEPTH = 2
DEC_BATCH = 32
DEC_SEQ = 32
PAST_LEN = 1024
CHUNK = 64
RMS_EPS = 1e-6
ROPE_THETA = 10000.0
NEG_INF = -1e30

S5_WIDTH = 1024
S5_GROUP_CH = 16
S5_GROUPS = 64
S5_STATE = 64
S5_COLS = S5_GROUPS * S5_STATE

MLA_HEADS = 8
MLA_NOPE = 128
MLA_ROPE = 64
MLA_QK = MLA_NOPE + MLA_ROPE
MLA_V = 128
MLA_Q_LORA = 512
MLA_KV_LORA = 512
MLA_HEAD_PAD = 256
MLA_QK_PAD = MLA_HEADS * MLA_HEAD_PAD

RET_HEADS = 4
RET_DK = 128
RET_DV = 256
BRANCH_WIDTH = 1024
N_BRANCH = 3

N_MEM = 256
MEM_HEADS = 4
MEM_HEAD_DIM = 128
MEM_WIDTH = MEM_HEADS * MEM_HEAD_DIM

MOE_GROUPS = 4
MOE_PER_GROUP = 8
N_EXPERTS = 32
MOE_TOP_K = 2
EXPERT_FF = 512

N_PROMPT = BATCH * SEQ
N_SAMPLE = DEC_BATCH * DEC_SEQ
N_ROWS = N_PROMPT + N_SAMPLE

COL_GMIX = 0
COL_S5 = 6144
COL_RV = 7168
COL_RG = 8192
COL_QLAT = 9216
COL_KVLAT = 9728
COL_RQ = 10240
COL_RK = 10752
PROJ_COLS = 11264
IN_S5 = (0, 1024)
IN_QLAT = (1024, 1536)
IN_KVLAT = (1536, 2048)
IN_KPE = (2048, 2112)
IN_RQ = (2112, 2624)
IN_RK = (2624, 3136)
IN_RV = (3136, 4160)
IN_RG = (4160, 5184)
IN_GMIX = (5184, 11328)

LANES = 128
EXPERT_ROWS = 256
N_PAIRS = N_ROWS * MOE_TOP_K
N_EXPERT_BLOCKS = -(-(N_PAIRS + N_EXPERTS * (EXPERT_ROWS - 1)) // EXPERT_ROWS)
VMEM_LIMIT = 56 * 1024 * 1024


def _cparams(sem):
    return pltpu.CompilerParams(dimension_semantics=sem, vmem_limit_bytes=VMEM_LIMIT)


def _rms_rows(x, g):
    r = lax.rsqrt(jnp.mean(x * x, axis=-1, keepdims=True) + RMS_EPS)
    return (x * r) * g


def _dot(a, b):
    return jnp.dot(a, b, preferred_element_type=F32)


def _dot_nt(a, b):
    return lax.dot_general(a, b, (((1,), (1,)), ((), ())), preferred_element_type=F32)


def _rms_proj_kernel(x_ref, g_ref, w_ref, *refs, side):
    if side:
        w2_ref, o_ref, o2_ref, xn_ref = refs
    else:
        o_ref, xn_ref = refs

    @pl.when(pl.program_id(1) == 0)
    def _():
        xn_ref[...] = _rms_rows(x_ref[...], g_ref[...]).astype(BF16)
        if side:
            o2_ref[...] = _dot(xn_ref[...], w2_ref[...]).astype(o2_ref.dtype)

    o_ref[...] = _dot(xn_ref[...], w_ref[...]).astype(o_ref.dtype)


def _rms_proj(x, g, w, out_dtype, tm, tn, w2=None, out2_dtype=None):
    n, k = x.shape
    cols = w.shape[1]
    side = w2 is not None
    in_specs = [
        pl.BlockSpec((tm, k), lambda i, j: (i, 0)),
        pl.BlockSpec((1, k), lambda i, j: (0, 0)),
        pl.BlockSpec((k, tn), lambda i, j: (0, j)),
    ]
    out_specs = [pl.BlockSpec((tm, tn), lambda i, j: (i, j))]
    out_shape = [jax.ShapeDtypeStruct((n, cols), out_dtype)]
    args = [x, g.reshape(1, k), w]
    if side:
        cols2 = w2.shape[1]
        in_specs.append(pl.BlockSpec((k, cols2), lambda i, j: (0, 0)))
        out_specs.append(pl.BlockSpec((tm, cols2), lambda i, j: (i, 0)))
        out_shape.append(jax.ShapeDtypeStruct((n, cols2), out2_dtype))
        args.append(w2)
    return pl.pallas_call(
        functools.partial(_rms_proj_kernel, side=side),
        grid=(n // tm, cols // tn),
        in_specs=in_specs,
        out_specs=out_specs,
        out_shape=out_shape,
        scratch_shapes=[pltpu.VMEM((tm, k), BF16)],
        compiler_params=_cparams(("parallel", "arbitrary")),
        name="rms_proj_side" if side else "rms_proj",
    )(*args)


def _res_matmul_kernel(a_ref, w_ref, r_ref, o_ref):
    o_ref[...] = r_ref[...] + _dot(a_ref[...], w_ref[...])


def _res_matmul(a, w, res, tm, tn):
    n, k = a.shape
    cols = w.shape[1]
    return pl.pallas_call(
        _res_matmul_kernel,
        grid=(n // tm, cols // tn),
        in_specs=[
            pl.BlockSpec((tm, k), lambda i, j: (i, 0)),
            pl.BlockSpec((k, tn), lambda i, j: (0, j)),
            pl.BlockSpec((tm, tn), lambda i, j: (i, j)),
        ],
        out_specs=pl.BlockSpec((tm, tn), lambda i, j: (i, j)),
        out_shape=jax.ShapeDtypeStruct((n, cols), F32),
        compiler_params=_cparams(("parallel", "arbitrary")),
        name="res_matmul",
    )(a, w, res)


S5_K_SLAB = 256
S5_N_SLAB = 1024
S5_SLABS = S5_WIDTH // S5_K_SLAB
S5_SCAN_COLS = 512


def _gelu_tanh(x):
    return 0.5 * x * (1.0 + jnp.tanh(math.sqrt(2.0 / math.pi) * (x + 0.044715 * (x * x * x))))


def _sigmoid(x):
    return 1.0 / (1.0 + jnp.exp(-x))


def _s5_kernel(u_ref, bre_ref, bim_ref, cre_ref, cim_ref, ar_ref, ai_ref, d_ref, wglu_ref,
               h0r_ref, h0i_ref, o_ref, hr_out_ref, hi_out_ref, sre_ref, sim_ref, cr_ref, ci_ref, *, tt):
    @pl.when(pl.program_id(1) == 0)
    def _():
        cr_ref[...] = h0r_ref[0]
        ci_ref[...] = h0i_ref[0]

    u = u_ref[...]
    for n in range(S5_SLABS):
        un = u[:, S5_K_SLAB * n:S5_K_SLAB * (n + 1)]
        sre_ref[:, S5_N_SLAB * n:S5_N_SLAB * (n + 1)] = _dot(un, bre_ref[n])
        sim_ref[:, S5_N_SLAB * n:S5_N_SLAB * (n + 1)] = _dot(un, bim_ref[n])

    for c in range(S5_COLS // S5_SCAN_COLS):
        sl = slice(c * S5_SCAN_COLS, (c + 1) * S5_SCAN_COLS)
        ar = ar_ref[:, sl]
        ai = ai_ref[:, sl]

        def body(r, carry, sl=sl, ar=ar, ai=ai):
            hr, hi = carry
            nr = ar * hr - ai * hi + sre_ref[pl.ds(r, 1), sl]
            ni = ar * hi + ai * hr + sim_ref[pl.ds(r, 1), sl]
            sre_ref[pl.ds(r, 1), sl] = nr
            sim_ref[pl.ds(r, 1), sl] = ni
            return nr, ni

        hr, hi = lax.fori_loop(0, tt, body, (cr_ref[:, sl], ci_ref[:, sl]))
        cr_ref[:, sl] = hr
        ci_ref[:, sl] = hi

    ys = []
    for n in range(S5_SLABS):
        hr_n = sre_ref[:, S5_N_SLAB * n:S5_N_SLAB * (n + 1)].astype(BF16)
        hi_n = sim_ref[:, S5_N_SLAB * n:S5_N_SLAB * (n + 1)].astype(BF16)
        ys.append(_dot(hr_n, cre_ref[n]) + _dot(hi_n, cim_ref[n]))
    y = jnp.concatenate(ys, axis=1) + d_ref[...] * u.astype(F32)
    z = _gelu_tanh(y)
    gate = _sigmoid(_dot(z.astype(BF16), wglu_ref[...]))
    o_ref[...] = (z * gate).astype(o_ref.dtype)
    hr_out_ref[0] = cr_ref[...]
    hi_out_ref[0] = ci_ref[...]


def _s5_tables(lp):
    a_re, a_im = lp['s5_a_re'], lp['s5_a_im']
    dt = jnp.exp(lp['s5_log_dt'])[:, None]
    mag = jnp.exp(a_re * dt)
    lb_re = mag * jnp.cos(a_im * dt)
    lb_im = mag * jnp.sin(a_im * dt)
    den = a_re * a_re + a_im * a_im
    n_re = lb_re - 1.0
    co_re = (n_re * a_re + lb_im * a_im) / den
    co_im = (lb_im * a_re - n_re * a_im) / den
    bb_re = co_re[..., None] * lp['s5_b_re'] - co_im[..., None] * lp['s5_b_im']
    bb_im = co_re[..., None] * lp['s5_b_im'] + co_im[..., None] * lp['s5_b_re']
    eye = jnp.eye(S5_GROUPS, dtype=F32)

    def b_tiles(bb):
        full = jnp.einsum('gph,gk->ghkp', bb, eye).reshape(S5_WIDTH, S5_COLS)
        full = full.reshape(S5_SLABS, S5_K_SLAB, S5_SLABS, S5_N_SLAB)
        return jnp.stack([full[n, :, n, :] for n in range(S5_SLABS)]).astype(BF16)

    def c_tiles(cc):
        full = jnp.einsum('ghp,gk->gpkh', cc, eye).reshape(S5_COLS, S5_WIDTH)
        full = full.reshape(S5_SLABS, S5_N_SLAB, S5_SLABS, S5_K_SLAB)
        return jnp.stack([full[n, :, n, :] for n in range(S5_SLABS)]).astype(BF16)

    return dict(
        bre=b_tiles(bb_re), bim=b_tiles(bb_im),
        cre=c_tiles(lp['s5_c_re']), cim=c_tiles(-lp['s5_c_im']),
        ar=lb_re.reshape(1, S5_COLS), ai=lb_im.reshape(1, S5_COLS),
        d=lp['s5_d'].reshape(1, S5_WIDTH),
    )


def _s5_branch(proj, tabs, wglu, h0_re, h0_im, n_seq, seq_len, tt, row0):
    nt = seq_len // tt
    base = row0 // tt
    col = COL_S5 // S5_WIDTH
    const3 = lambda s, t: (0, 0, 0)
    const2 = lambda s, t: (0, 0)
    out, hr, hi = pl.pallas_call(
        functools.partial(_s5_kernel, tt=tt),
        grid=(n_seq, nt),
        in_specs=[
            pl.BlockSpec((tt, S5_WIDTH), lambda s, t: (base + s * nt + t, col)),
            pl.BlockSpec((S5_SLABS, S5_K_SLAB, S5_N_SLAB), const3),
            pl.BlockSpec((S5_SLABS, S5_K_SLAB, S5_N_SLAB), const3),
            pl.BlockSpec((S5_SLABS, S5_N_SLAB, S5_K_SLAB), const3),
            pl.BlockSpec((S5_SLABS, S5_N_SLAB, S5_K_SLAB), const3),
            pl.BlockSpec((1, S5_COLS), const2),
            pl.BlockSpec((1, S5_COLS), const2),
            pl.BlockSpec((1, S5_WIDTH), const2),
            pl.BlockSpec((S5_WIDTH, S5_WIDTH), const2),
            pl.BlockSpec((1, 1, S5_COLS), lambda s, t: (s, 0, 0)),
            pl.BlockSpec((1, 1, S5_COLS), lambda s, t: (s, 0, 0)),
        ],
        out_specs=[
            pl.BlockSpec((tt, S5_WIDTH), lambda s, t: (s * nt + t, 0)),
            pl.BlockSpec((1, 1, S5_COLS), lambda s, t: (s, 0, 0)),
            pl.BlockSpec((1, 1, S5_COLS), lambda s, t: (s, 0, 0)),
        ],
        out_shape=[
            jax.ShapeDtypeStruct((n_seq * seq_len, S5_WIDTH), BF16),
            jax.ShapeDtypeStruct((n_seq, 1, S5_COLS), F32),
            jax.ShapeDtypeStruct((n_seq, 1, S5_COLS), F32),
        ],
        scratch_shapes=[
            pltpu.VMEM((tt, S5_COLS), F32), pltpu.VMEM((tt, S5_COLS), F32),
            pltpu.VMEM((1, S5_COLS), F32), pltpu.VMEM((1, S5_COLS), F32),
        ],
        compiler_params=_cparams(("parallel", "arbitrary")),
        name="s5_branch",
    )(proj, tabs['bre'], tabs['bim'], tabs['cre'], tabs['cim'], tabs['ar'], tabs['ai'], tabs['d'], wglu,
      h0_re.reshape(n_seq, 1, S5_COLS), h0_im.reshape(n_seq, 1, S5_COLS))
    return (out, hr.reshape(n_seq, S5_GROUPS, S5_STATE), hi.reshape(n_seq, S5_GROUPS, S5_STATE))


def _rope_tables(pos, d):
    inv_freq = ROPE_THETA ** (-jnp.arange(0, d, 2, dtype=F32) / d)
    ang = pos.astype(F32)[:, None] * inv_freq[None, :]
    cos, sin = jnp.cos(ang), jnp.sin(ang)
    pad = jnp.zeros((pos.shape[0], LANES - d), F32)
    return (jnp.concatenate([cos, cos, pad], axis=1), jnp.concatenate([-sin, sin, pad], axis=1))


def _pad_head_vec(g):
    return jnp.concatenate([g, jnp.zeros((MLA_HEAD_PAD - MLA_QK,), F32)]).reshape(1, MLA_HEAD_PAD)


def _qk_head(nope, pe, gain, cos2, sin2, scale):
    ss = jnp.sum(nope * nope, axis=-1, keepdims=True) + jnp.sum(pe * pe, axis=-1, keepdims=True)
    r = lax.rsqrt(ss * (1.0 / MLA_QK) + RMS_EPS)
    nope = (nope * r) * gain[:, :MLA_NOPE]
    pe = (pe * r) * gain[:, MLA_NOPE:]
    lane = lax.broadcasted_iota(jnp.int32, pe.shape, 1)
    half = MLA_ROPE // 2
    swap = jnp.where(lane < half, pltpu.roll(pe, LANES - half, 1), pltpu.roll(pe, half, 1))
    pe = pe * cos2 + swap * sin2
    return nope * scale, pe * scale


def _mla_q_kernel(lat_ref, g_ref, w_ref, gain_ref, cos_ref, sin_ref, o_ref):
    xn = _rms_rows(lat_ref[...].astype(F32), g_ref[...]).astype(BF16)
    q = _dot(xn, w_ref[...])
    gain = gain_ref[...]
    cos2 = cos_ref[...]
    sin2 = sin_ref[...]
    for h in range(MLA_HEADS):
        c0 = h * MLA_HEAD_PAD
        nope, pe = _qk_head(q[:, c0:c0 + MLA_NOPE], q[:, c0 + MLA_NOPE:c0 + MLA_HEAD_PAD],
                            gain, cos2, sin2, MLA_QK ** -0.5)
        o_ref[:, c0:c0 + MLA_NOPE] = nope.astype(BF16)
        o_ref[:, c0 + MLA_NOPE:c0 + MLA_HEAD_PAD] = pe.astype(BF16)


def _mla_q(proj, g, w_uq_pad, gain_pad, cos2, sin2, tm):
    n = proj.shape[0]
    col = COL_QLAT // MLA_Q_LORA
    c2 = lambda i: (0, 0)
    return pl.pallas_call(
        _mla_q_kernel,
        grid=(n // tm,),
        in_specs=[
            pl.BlockSpec((tm, MLA_Q_LORA), lambda i: (i, col)),
            pl.BlockSpec((1, MLA_Q_LORA), c2),
            pl.BlockSpec((MLA_Q_LORA, MLA_QK_PAD), c2),
            pl.BlockSpec((1, MLA_HEAD_PAD), c2),
            pl.BlockSpec((tm, LANES), lambda i: (i, 0)),
            pl.BlockSpec((tm, LANES), lambda i: (i, 0)),
        ],
        out_specs=pl.BlockSpec((tm, MLA_QK_PAD), lambda i: (i, 0)),
        out_shape=jax.ShapeDtypeStruct((n, MLA_QK_PAD), BF16),
        compiler_params=_cparams(("parallel",)),
        name="mla_q",
    )(proj, g.reshape(1, -1), w_uq_pad, gain_pad, cos2, sin2)


def _mla_kv_kernel(lat_ref, g_ref, w_ref, gain_ref, kpe_ref, cos_ref, sin_ref, *out_refs, normalize):
    lat = lat_ref[...].astype(F32)
    if normalize:
        ckv_ref, k_ref, v_ref = out_refs
        lat = _rms_rows(lat, g_ref[...])
        ckv_ref[...] = lat
    else:
        k_ref, v_ref = out_refs
    kv = _dot(lat.astype(BF16), w_ref[...])
    gain = gain_ref[...]
    kpe = kpe_ref[...]
    cos2 = cos_ref[...]
    sin2 = sin_ref[...]
    for h in range(MLA_HEADS):
        c0 = h * (MLA_NOPE + MLA_V)
        nope, pe = _qk_head(kv[:, c0:c0 + MLA_NOPE], kpe, gain, cos2, sin2, 1.0)
        k0 = h * MLA_HEAD_PAD
        k_ref[:, k0:k0 + MLA_NOPE] = nope.astype(BF16)
        k_ref[:, k0 + MLA_NOPE:k0 + MLA_HEAD_PAD] = pe.astype(BF16)
        v_ref[:, h * MLA_V:(h + 1) * MLA_V] = kv[:, c0 + MLA_NOPE:c0 + MLA_NOPE + MLA_V].astype(BF16)


def _mla_kv(lat, lat_col, g, w_ukv, gain_pad, kpe_pad, cos2, sin2, tm, normalize):
    n = lat.shape[0]
    c2 = lambda i: (0, 0)
    row = lambda i: (i, 0)
    out_specs = [pl.BlockSpec((tm, MLA_QK_PAD), row), pl.BlockSpec((tm, MLA_HEADS * MLA_V), row)]
    out_shape = [jax.ShapeDtypeStruct((n, MLA_QK_PAD), BF16), jax.ShapeDtypeStruct((n, MLA_HEADS * MLA_V), BF16)]
    if normalize:
        out_specs = [pl.BlockSpec((tm, MLA_KV_LORA), row)] + out_specs
        out_shape = [jax.ShapeDtypeStruct((n, MLA_KV_LORA), F32)] + out_shape
    return pl.pallas_call(
        functools.partial(_mla_kv_kernel, normalize=normalize),
        grid=(n // tm,),
        in_specs=[
            pl.BlockSpec((tm, MLA_KV_LORA), lambda i: (i, lat_col)),
            pl.BlockSpec((1, MLA_KV_LORA), c2),
            pl.BlockSpec((MLA_KV_LORA, MLA_HEADS * (MLA_NOPE + MLA_V)), c2),
            pl.BlockSpec((1, MLA_HEAD_PAD), c2),
            pl.BlockSpec((tm, LANES), row),
            pl.BlockSpec((tm, LANES), row),
            pl.BlockSpec((tm, LANES), row),
        ],
        out_specs=out_specs,
        out_shape=out_shape,
        compiler_params=_cparams(("parallel",)),
        name="mla_kv_norm" if normalize else "mla_kv_past",
    )(lat, g.reshape(1, -1), w_ukv, gain_pad, kpe_pad, cos2, sin2)


def _attn_prompt_kernel(q_ref, k_ref, v_ref, o_ref, m_ref, l_ref, acc_ref, *, tq):
    qi = pl.program_id(2)
    ki = pl.program_id(3)

    @pl.when(ki == 0)
    def _():
        m_ref[...] = jnp.full(m_ref.shape, NEG_INF, F32)
        l_ref[...] = jnp.zeros(l_ref.shape, F32)
        acc_ref[...] = jnp.zeros(acc_ref.shape, F32)

    @pl.when(ki <= qi)
    def _():
        s = _dot_nt(q_ref[...], k_ref[...])
        row_chunk = lax.broadcasted_iota(jnp.int32, s.shape, 0) // CHUNK
        col_chunk = lax.broadcasted_iota(jnp.int32, s.shape, 1) // CHUNK
        visible = jnp.logical_or(col_chunk <= row_chunk, ki < qi)
        s = jnp.where(visible, s, NEG_INF)
        m_old = m_ref[...]
        m_new = jnp.maximum(m_old, jnp.max(s, axis=-1, keepdims=True))
        alpha = jnp.exp(m_old - m_new)
        p = jnp.exp(s - m_new)
        l_ref[...] = alpha * l_ref[...] + jnp.sum(p, axis=-1, keepdims=True)
        acc_ref[...] = alpha * acc_ref[...] + _dot(p.astype(BF16), v_ref[...])
        m_ref[...] = m_new

    @pl.when(ki == pl.num_programs(3) - 1)
    def _():
        o_ref[...] = (acc_ref[...] / l_ref[...]).astype(o_ref.dtype)


def _attn_prompt(q, k, v, n_batch, seq_len, tq):
    assert tq % CHUNK == 0
    nq = seq_len // tq
    return pl.pallas_call(
        functools.partial(_attn_prompt_kernel, tq=tq),
        grid=(n_batch, MLA_HEADS, nq, nq),
        in_specs=[
            pl.BlockSpec((tq, MLA_HEAD_PAD), lambda b, h, qi, ki: (b * nq + qi, h)),
            pl.BlockSpec((tq, MLA_HEAD_PAD), lambda b, h, qi, ki: (b * nq + jnp.minimum(ki, qi), h)),
            pl.BlockSpec((tq, MLA_V), lambda b, h, qi, ki: (b * nq + jnp.minimum(ki, qi), h)),
        ],
        out_specs=pl.BlockSpec((tq, MLA_V), lambda b, h, qi, ki: (b * nq + qi, h)),
        out_shape=jax.ShapeDtypeStruct((n_batch * seq_len, MLA_HEADS * MLA_V), BF16),
        scratch_shapes=[pltpu.VMEM((tq, 1), F32), pltpu.VMEM((tq, 1), F32), pltpu.VMEM((tq, MLA_V), F32)],
        compiler_params=_cparams(("parallel", "parallel", "parallel", "arbitrary")),
        name="attn_prompt",
    )(q, k, v)


def _attn_sample_kernel(q_ref, kp_ref, kn_ref, vp_ref, vn_ref, o_ref):
    for h in range(MLA_HEADS):
        ks = slice(h * MLA_HEAD_PAD, (h + 1) * MLA_HEAD_PAD)
        vs = slice(h * MLA_V, (h + 1) * MLA_V)
        q = q_ref[:, ks]
        s_p = _dot_nt(q, kp_ref[:, ks])
        s_n = _dot_nt(q, kn_ref[:, ks])
        m = jnp.maximum(jnp.max(s_p, axis=-1, keepdims=True), jnp.max(s_n, axis=-1, keepdims=True))
        p_p = jnp.exp(s_p - m)
        p_n = jnp.exp(s_n - m)
        l = jnp.sum(p_p, axis=-1, keepdims=True) + jnp.sum(p_n, axis=-1, keepdims=True)
        o = _dot(p_p.astype(BF16), vp_ref[:, vs]) + _dot(p_n.astype(BF16), vn_ref[:, vs])
        o_ref[:, vs] = (o / l).astype(o_ref.dtype)


def _attn_sample(q, k_new, v_new, k_past, v_past, row0):
    assert (PAST_LEN + DEC_SEQ - 1) // CHUNK <= PAST_LEN // CHUNK
    base = row0 // DEC_SEQ
    new = lambda b: (base + b, 0)
    past = lambda b: (b, 0)
    return pl.pallas_call(
        _attn_sample_kernel,
        grid=(DEC_BATCH,),
        in_specs=[
            pl.BlockSpec((DEC_SEQ, MLA_QK_PAD), new),
            pl.BlockSpec((PAST_LEN, MLA_QK_PAD), past),
            pl.BlockSpec((DEC_SEQ, MLA_QK_PAD), new),
            pl.BlockSpec((PAST_LEN, MLA_HEADS * MLA_V), past),
            pl.BlockSpec((DEC_SEQ, MLA_HEADS * MLA_V), new),
        ],
        out_specs=pl.BlockSpec((DEC_SEQ, MLA_HEADS * MLA_V), past),
        out_shape=jax.ShapeDtypeStruct((N_SAMPLE, MLA_HEADS * MLA_V), BF16),
        compiler_params=_cparams(("parallel",)),
        name="attn_sample",
    )(q, k_past, k_new, v_past, v_new)


def _retention_tables(block):
    log_g = jnp.log1p(-jnp.exp2(-5.0 - jnp.arange(RET_HEADS, dtype=F32)))
    idx = jnp.arange(block, dtype=F32)
    diff = idx[:, None] - idx[None, :]
    decay = jnp.where(diff >= 0, jnp.exp(log_g[:, None, None] * jnp.maximum(diff, 0.0)), 0.0)
    q_decay = jnp.exp(log_g[:, None] * (idx + 1.0))
    k_decay = jnp.exp(log_g[:, None] * (block - 1.0 - idx))
    blk_decay = jnp.exp(log_g * block)
    return (decay,
            jnp.broadcast_to(q_decay[:, :, None], (RET_HEADS, block, RET_DK)),
            jnp.broadcast_to(k_decay[:, :, None], (RET_HEADS, block, RET_DK)),
            jnp.broadcast_to(blk_decay[:, None, None], (RET_HEADS, 1, RET_DV)))


def _rope128(x, cos, sin):
    return x * cos + pltpu.roll(x, RET_DK // 2, 1) * sin


def _retention_kernel(rq_ref, rk_ref, rv_ref, rg_ref, cos_ref, sin_ref, dec_ref, qd_ref, kd_ref, bd_ref, s0_ref,
                      o_ref, s_out_ref, s_ref):
    @pl.when(pl.program_id(2) == 0)
    def _():
        s_ref[...] = s0_ref[0, 0]

    cos = cos_ref[...]
    sin = sin_ref[...]
    q = _rope128(rq_ref[...].astype(F32), cos, sin) * (RET_DK ** -0.5)
    k = _rope128(rk_ref[...].astype(F32), cos, sin)
    v = rv_ref[...]
    s_old = s_ref[...]
    att = _dot_nt(q.astype(BF16), k.astype(BF16)) * dec_ref[0]
    o = _dot(att.astype(BF16), v) + _dot((q * qd_ref[0]).astype(BF16), s_old.astype(BF16))
    kd = (k * kd_ref[0]).astype(BF16)
    s_new = bd_ref[0] * s_old + lax.dot_general(kd, v, (((0,), (0,)), ((), ())), preferred_element_type=F32)
    s_ref[...] = s_new
    s_out_ref[0, 0] = s_new
    mu = jnp.mean(o, axis=-1, keepdims=True)
    c = o - mu
    o = c * lax.rsqrt(jnp.mean(c * c, axis=-1, keepdims=True) + RMS_EPS)
    g = rg_ref[...].astype(F32)
    o_ref[...] = (o * (g * _sigmoid(g))).astype(o_ref.dtype)


def _retention_branch(proj, cos, sin, s0, n_seq, seq_len, block, row0):
    nblk = seq_len // block
    base = row0 // block
    dec, qd, kd, bd = _retention_tables(block)
    rows = lambda b, h, t: base + b * nblk + t
    head3 = lambda b, h, t: (h, 0, 0)
    state = lambda b, h, t: (b, h, 0, 0)
    out, s_last = pl.pallas_call(
        _retention_kernel,
        grid=(n_seq, RET_HEADS, nblk),
        in_specs=[
            pl.BlockSpec((block, RET_DK), lambda b, h, t: (rows(b, h, t), COL_RQ // RET_DK + h)),
            pl.BlockSpec((block, RET_DK), lambda b, h, t: (rows(b, h, t), COL_RK // RET_DK + h)),
            pl.BlockSpec((block, RET_DV), lambda b, h, t: (rows(b, h, t), COL_RV // RET_DV + h)),
            pl.BlockSpec((block, RET_DV), lambda b, h, t: (rows(b, h, t), COL_RG // RET_DV + h)),
            pl.BlockSpec((block, LANES), lambda b, h, t: (rows(b, h, t), 0)),
            pl.BlockSpec((block, LANES), lambda b, h, t: (rows(b, h, t), 0)),
            pl.BlockSpec((1, block, block), head3),
            pl.BlockSpec((1, block, RET_DK), head3),
            pl.BlockSpec((1, block, RET_DK), head3),
            pl.BlockSpec((1, 1, RET_DV), head3),
            pl.BlockSpec((1, 1, RET_DK, RET_DV), state),
        ],
        out_specs=[
            pl.BlockSpec((block, RET_DV), lambda b, h, t: (b * nblk + t, h)),
            pl.BlockSpec((1, 1, RET_DK, RET_DV), state),
        ],
        out_shape=[
            jax.ShapeDtypeStruct((n_seq * seq_len, RET_HEADS * RET_DV), BF16),
            jax.ShapeDtypeStruct((n_seq, RET_HEADS, RET_DK, RET_DV), F32),
        ],
        scratch_shapes=[pltpu.VMEM((RET_DK, RET_DV), F32)],
        compiler_params=_cparams(("parallel", "parallel", "arbitrary")),
        name="retention",
    )(proj, proj, proj, proj, cos, sin, dec, qd, kd, bd, s0)
    return out, s_last


def _merge_kernel(a_ref, b_ref, c_ref, w_ref, g0_ref, g1_ref, g2_ref, o_ref):
    acc = _sigmoid(g0_ref[...].astype(F32)) * _dot(a_ref[...], w_ref[0])
    acc += _sigmoid(g1_ref[...].astype(F32)) * _dot(b_ref[...], w_ref[1])
    acc += _sigmoid(g2_ref[...].astype(F32)) * _dot(c_ref[...], w_ref[2])
    o_ref[...] = acc.astype(o_ref.dtype)


def _merge(o_a, o_b, o_c, w_branch, proj, tm, tn):
    n = o_a.shape[0]
    nj = D_MODEL // tn
    br = pl.BlockSpec((tm, BRANCH_WIDTH), lambda i, j: (i, 0))
    gate = lambda b: pl.BlockSpec((tm, tn), lambda i, j: (i, COL_GMIX // tn + b * nj + j))
    return pl.pallas_call(
        _merge_kernel,
        grid=(n // tm, nj),
        in_specs=[br, br, br,
                  pl.BlockSpec((N_BRANCH, BRANCH_WIDTH, tn), lambda i, j: (0, 0, j)),
                  gate(0), gate(1), gate(2)],
        out_specs=pl.BlockSpec((tm, tn), lambda i, j: (i, j)),
        out_shape=jax.ShapeDtypeStruct((n, D_MODEL), BF16),
        compiler_params=_cparams(("parallel", "arbitrary")),
        name="branch_merge",
    )(o_a, o_b, o_c, w_branch, proj, proj, proj)


def _memkv_kernel(m_ref, g_ref, wk_ref, wv_ref, kg_ref, k_ref, v_ref):
    mn = _rms_rows(m_ref[...], g_ref[...]).astype(BF16)
    kk = _dot(mn, wk_ref[...])
    for h in range(MEM_HEADS):
        sl = slice(h * MEM_HEAD_DIM, (h + 1) * MEM_HEAD_DIM)
        k_ref[:, sl] = _rms_rows(kk[:, sl], kg_ref[...])
    v_ref[...] = _dot(mn, wv_ref[...])


def _memkv(mem, g, wk, wv, k_gain, tm):
    n, k = mem.shape
    c2 = lambda i: (0, 0)
    return pl.pallas_call(
        _memkv_kernel,
        grid=(n // tm,),
        in_specs=[
            pl.BlockSpec((tm, k), lambda i: (i, 0)),
            pl.BlockSpec((1, k), c2),
            pl.BlockSpec((k, MEM_WIDTH), c2),
            pl.BlockSpec((k, MEM_WIDTH), c2),
            pl.BlockSpec((1, MEM_HEAD_DIM), c2),
        ],
        out_specs=[pl.BlockSpec((tm, MEM_WIDTH), lambda i: (i, 0)), pl.BlockSpec((tm, MEM_WIDTH), lambda i: (i, 0))],
        out_shape=[jax.ShapeDtypeStruct((n, MEM_WIDTH), F32), jax.ShapeDtypeStruct((n, MEM_WIDTH), F32)],
        compiler_params=_cparams(("parallel",)),
        name="memory_kv",
    )(mem, g.reshape(1, k), wk, wv, k_gain.reshape(1, MEM_HEAD_DIM))


def _cross_attn_kernel(q_ref, k_ref, v_ref, qg_ref, o_ref):
    for h in range(MEM_HEADS):
        sl = slice(h * MEM_HEAD_DIM, (h + 1) * MEM_HEAD_DIM)
        q = _rms_rows(q_ref[:, sl].astype(F32), qg_ref[...]) * (MEM_HEAD_DIM ** -0.5)
        s = _dot_nt(q.astype(BF16), k_ref[0, :, sl].astype(BF16))
        p = jnp.exp(s - jnp.max(s, axis=-1, keepdims=True))
        l = jnp.sum(p, axis=-1, keepdims=True)
        o = _dot(p.astype(BF16), v_ref[0, :, sl].astype(BF16))
        o_ref[:, sl] = (o / l).astype(o_ref.dtype)


def _cross_attn(qc, mem_k, mem_v, q_gain, n_batch, rows_per_batch, tm, row0):
    tiles = rows_per_batch // tm
    base = row0 // tm
    return pl.pallas_call(
        _cross_attn_kernel,
        grid=(n_batch, tiles),
        in_specs=[
            pl.BlockSpec((tm, MEM_WIDTH), lambda b, t: (base + b * tiles + t, 0)),
            pl.BlockSpec((1, N_MEM, MEM_WIDTH), lambda b, t: (b, 0, 0)),
            pl.BlockSpec((1, N_MEM, MEM_WIDTH), lambda b, t: (b, 0, 0)),
            pl.BlockSpec((1, MEM_HEAD_DIM), lambda b, t: (0, 0)),
        ],
        out_specs=pl.BlockSpec((tm, MEM_WIDTH), lambda b, t: (b * tiles + t, 0)),
        out_shape=jax.ShapeDtypeStruct((n_batch * rows_per_batch, MEM_WIDTH), BF16),
        compiler_params=_cparams(("parallel", "arbitrary")),
        name="cross_attn",
    )(qc, mem_k, mem_v, q_gain.reshape(1, MEM_HEAD_DIM))


ROUTER_LANES = LANES
FAR_LANE = 4 * LANES


def _split_bf16(x):
    hi = x.astype(BF16)
    lo = (x - hi.astype(F32)).astype(BF16)
    return hi, lo


def _router_kernel(x_ref, g_ref, w_ref, b_ref, xn_ref, ids_ref, gates_ref):
    xn = _rms_rows(x_ref[...], g_ref[...])
    xn_ref[...] = xn
    xh, xl = _split_bf16(xn)
    wh, wl = _split_bf16(w_ref[...])
    logits = _dot(xh, wh) + (_dot(xh, wl) + _dot(xl, wh)) + b_ref[...]
    lane = lax.broadcasted_iota(jnp.int32, logits.shape, 1)

    def first_max(vals):
        top = jnp.max(vals, axis=-1, keepdims=True)
        idx = jnp.min(jnp.where(vals == top, lane, FAR_LANE), axis=-1, keepdims=True)
        return top, idx

    g_mask = jnp.logical_and(lane >= N_EXPERTS, lane < N_EXPERTS + MOE_GROUPS)
    g_top, g_lane = first_max(jnp.where(g_mask, logits, NEG_INF))
    p_group = 1.0 / jnp.sum(jnp.where(g_mask, jnp.exp(logits - g_top), 0.0), axis=-1, keepdims=True)
    g_sel = g_lane - N_EXPERTS
    e_mask = jnp.logical_and(lane < N_EXPERTS, lane // MOE_PER_GROUP == g_sel)
    e_vals = jnp.where(e_mask, logits, NEG_INF)
    e1_top, e1 = first_max(e_vals)
    e2_top, e2 = first_max(jnp.where(lane == e1, NEG_INF, e_vals))
    t = jnp.exp(e2_top - e1_top)
    gate1 = p_group / (1.0 + t)
    gate2 = p_group * t / (1.0 + t)
    ids_ref[...] = jnp.where(lane == 0, e1, jnp.where(lane == 1, e2, 0))
    gates_ref[...] = jnp.where(lane == 0, gate1, jnp.where(lane == 1, gate2, 0.0))


def _router(x, g, w_router, b_router, tm):
    n, k = x.shape
    c2 = lambda i: (0, 0)
    row = lambda i: (i, 0)
    return pl.pallas_call(
        _router_kernel,
        grid=(n // tm,),
        in_specs=[
            pl.BlockSpec((tm, k), row),
            pl.BlockSpec((1, k), c2),
            pl.BlockSpec((k, ROUTER_LANES), c2),
            pl.BlockSpec((1, ROUTER_LANES), c2),
        ],
        out_specs=[pl.BlockSpec((tm, k), row), pl.BlockSpec((tm, ROUTER_LANES), row),
                   pl.BlockSpec((tm, ROUTER_LANES), row)],
        out_shape=[jax.ShapeDtypeStruct((n, k), F32), jax.ShapeDtypeStruct((n, ROUTER_LANES), jnp.int32),
                   jax.ShapeDtypeStruct((n, ROUTER_LANES), F32)],
        compiler_params=_cparams(("parallel",)),
        name="moe_router",
    )(x, g.reshape(1, k), w_router, b_router)


def _route_plan(expert_idx):
    flat_e = expert_idx.reshape(N_PAIRS)
    onehot = (flat_e[:, None] == jnp.arange(N_EXPERTS, dtype=jnp.int32)[None, :]).astype(jnp.int32)
    csum = jnp.cumsum(onehot, axis=0)
    rank = jnp.take_along_axis(csum, flat_e[:, None], axis=1)[:, 0] - 1
    counts = csum[-1]
    padded = (counts + EXPERT_ROWS - 1) // EXPERT_ROWS * EXPERT_ROWS
    pad_ends = jnp.cumsum(padded)
    pad_starts = pad_ends - padded
    dest = (pad_starts[flat_e] + rank).astype(jnp.int32)
    n_used = (pad_ends[-1] // EXPERT_ROWS).astype(jnp.int32).reshape(1)
    block_start = jnp.arange(N_EXPERT_BLOCKS, dtype=jnp.int32) * EXPERT_ROWS
    block_e = jnp.minimum(jnp.searchsorted(pad_ends, block_start, side='right'), N_EXPERTS - 1).astype(jnp.int32)
    src = jnp.zeros((N_EXPERT_BLOCKS * EXPERT_ROWS,), jnp.int32).at[dest].set(
        jnp.arange(N_PAIRS, dtype=jnp.int32) // MOE_TOP_K)
    return dest, src, block_e, n_used


def _row_gather_kernel(src_ref, x_hbm, o_ref, sem, *, rows):
    base = pl.program_id(0) * rows

    def row_copy(i):
        return pltpu.make_async_copy(x_hbm.at[pl.ds(src_ref[base + i], 1)], o_ref.at[pl.ds(i, 1)], sem)

    def start(i, c):
        row_copy(i).start()
        return c

    def wait(i, c):
        row_copy(i).wait()
        return c

    lax.fori_loop(0, rows, start, 0)
    lax.fori_loop(0, rows, wait, 0)


def _row_gather(x, src, rows):
    n_out = src.shape[0]
    width = x.shape[1]
    return pl.pallas_call(
        functools.partial(_row_gather_kernel, rows=rows),
        grid_spec=pltpu.PrefetchScalarGridSpec(
            num_scalar_prefetch=1,
            grid=(n_out // rows,),
            in_specs=[pl.BlockSpec(memory_space=pl.ANY)],
            out_specs=pl.BlockSpec((rows, width), lambda i, src: (i, 0)),
            scratch_shapes=[pltpu.SemaphoreType.DMA(())],
        ),
        out_shape=jax.ShapeDtypeStruct((n_out, width), x.dtype),
        compiler_params=_cparams(("arbitrary",)),
        name="moe_row_gather",
    )(src, x)


def _experts_kernel(be_ref, nu_ref, x_ref, wgu_ref, wd_ref, o_ref):
    used = pl.program_id(0) < nu_ref[0]

    @pl.when(used)
    def _():
        gu = _dot(x_ref[...].astype(BF16), wgu_ref[0].astype(BF16))
        g = gu[:, :EXPERT_FF]
        h = (g * _sigmoid(g)) * gu[:, EXPERT_FF:]
        o_ref[...] = _dot(h.astype(BF16), wd_ref[0].astype(BF16))

    @pl.when(jnp.logical_not(used))
    def _():
        o_ref[...] = jnp.zeros(o_ref.shape, o_ref.dtype)


def _experts(x_sorted, block_e, n_used, w_gate_up, w_down):
    return pl.pallas_call(
        _experts_kernel,
        grid_spec=pltpu.PrefetchScalarGridSpec(
            num_scalar_prefetch=2,
            grid=(N_EXPERT_BLOCKS,),
            in_specs=[
                pl.BlockSpec((EXPERT_ROWS, D_MODEL), lambda i, be, nu: (i, 0)),
                pl.BlockSpec((1, D_MODEL, 2 * EXPERT_FF), lambda i, be, nu: (be[i], 0, 0)),
                pl.BlockSpec((1, EXPERT_FF, D_MODEL), lambda i, be, nu: (be[i], 0, 0)),
            ],
            out_specs=pl.BlockSpec((EXPERT_ROWS, D_MODEL), lambda i, be, nu: (i, 0)),
        ),
        out_shape=jax.ShapeDtypeStruct((N_EXPERT_BLOCKS * EXPERT_ROWS, D_MODEL), F32),
        compiler_params=_cparams(("arbitrary",)),
        name="moe_experts",
    )(block_e, n_used, x_sorted, w_gate_up, w_down)


def _combine_kernel(dest_ref, x_ref, gates_ref, y_hbm, o_ref, buf_ref, sem, *, rows):
    base = pl.program_id(0) * rows

    def row_copy(i, k):
        return pltpu.make_async_copy(y_hbm.at[pl.ds(dest_ref[(base + i) * MOE_TOP_K + k], 1)],
                                     buf_ref.at[k, pl.ds(i, 1)], sem)

    def start(i, c):
        for k in range(MOE_TOP_K):
            row_copy(i, k).start()
        return c

    def wait(i, c):
        for k in range(MOE_TOP_K):
            row_copy(i, k).wait()
        return c

    lax.fori_loop(0, rows, start, 0)
    lax.fori_loop(0, rows, wait, 0)
    gates = gates_ref[...]
    y = gates[:, 0:1] * buf_ref[0] + gates[:, 1:2] * buf_ref[1]
    o_ref[...] = x_ref[...] + y


def _combine(x, gates, y_sorted, dest, rows):
    n, width = x.shape
    return pl.pallas_call(
        functools.partial(_combine_kernel, rows=rows),
        grid_spec=pltpu.PrefetchScalarGridSpec(
            num_scalar_prefetch=1,
            grid=(n // rows,),
            in_specs=[
                pl.BlockSpec((rows, width), lambda i, d: (i, 0)),
                pl.BlockSpec((rows, ROUTER_LANES), lambda i, d: (i, 0)),
                pl.BlockSpec(memory_space=pl.ANY),
            ],
            out_specs=pl.BlockSpec((rows, width), lambda i, d: (i, 0)),
            scratch_shapes=[pltpu.VMEM((MOE_TOP_K, rows, width), F32), pltpu.SemaphoreType.DMA(())],
        ),
        out_shape=jax.ShapeDtypeStruct((n, width), F32),
        compiler_params=_cparams(("arbitrary",)),
        name="moe_combine",
    )(dest, x, gates, y_sorted)


def _hier_moe(x, lp):
    w_router = jnp.concatenate(
        [lp['w_expert_router'], lp['w_group_router'],
         jnp.zeros((D_MODEL, ROUTER_LANES - N_EXPERTS - MOE_GROUPS), F32)], axis=1)
    b_router = jnp.concatenate(
        [lp['b_expert'], lp['b_group'], jnp.zeros((ROUTER_LANES - N_EXPERTS - MOE_GROUPS,), F32)]).reshape(1, -1)
    xn, ids, gates = _router(x, lp['norm_ffn'], w_router, b_router, 512)
    dest, src, block_e, n_used = _route_plan(ids[:, :MOE_TOP_K])
    x_sorted = _row_gather(xn, src, EXPERT_ROWS)
    y_sorted = _experts(x_sorted, block_e, n_used, lp['w_gate_up'], lp['w_down'])
    return _combine(x, gates, y_sorted, dest, 256)


def _cols(w, *ranges):
    return jnp.concatenate([w[:, a:b] for a, b in ranges], axis=1)


def _layer(x, mem_prompt, lp, past, tables):
    past_ckv, past_kpe, h0_re, h0_im, ret_s0, mem_k_s, mem_v_s = past
    cos_mla, sin_mla, cos_past, sin_past, cos_ret, sin_ret = tables
    w_in = lp['w_in']
    w_main = _cols(w_in, IN_GMIX, IN_S5, IN_RV, IN_RG, IN_QLAT, IN_KVLAT, IN_RQ, IN_RK).astype(BF16)
    w_kpe = jnp.pad(_cols(w_in, IN_KPE), ((0, 0), (0, LANES - MLA_ROPE))).astype(BF16)
    proj, kpe = _rms_proj(x, lp['norm_mix'], w_main, BF16, 1024, 512, w2=w_kpe, out2_dtype=F32)

    s5_tabs = _s5_tables(lp)
    w_glu = lp['s5_w_glu'].astype(BF16)
    zeros_s5 = jnp.zeros((BATCH, S5_GROUPS, S5_STATE), F32)
    a_p, p_s5_re, p_s5_im = _s5_branch(proj, s5_tabs, w_glu, zeros_s5, zeros_s5, BATCH, SEQ, 256, 0)
    a_s, s_s5_re, s_s5_im = _s5_branch(proj, s5_tabs, w_glu, h0_re, h0_im, DEC_BATCH, DEC_SEQ, DEC_SEQ, N_PROMPT)

    w_uq = lp['mla_w_uq'].reshape(MLA_Q_LORA, MLA_HEADS, MLA_QK)
    w_uq = jnp.pad(w_uq, ((0, 0), (0, 0), (0, MLA_HEAD_PAD - MLA_QK))).reshape(MLA_Q_LORA, MLA_QK_PAD).astype(BF16)
    w_ukv = lp['mla_w_ukv'].astype(BF16)
    q_gain = _pad_head_vec(lp['mla_q_gain'])
    k_gain = _pad_head_vec(lp['mla_k_gain'])
    q = _mla_q(proj, lp['mla_q_norm'], w_uq, q_gain, cos_mla, sin_mla, 512)
    c_kv, k_new, v_new = _mla_kv(proj, COL_KVLAT // MLA_KV_LORA, lp['mla_kv_norm'], w_ukv, k_gain, kpe,
                                 cos_mla, sin_mla, 512, True)
    kpe_past = jnp.pad(past_kpe.reshape(DEC_BATCH * PAST_LEN, MLA_ROPE), ((0, 0), (0, LANES - MLA_ROPE)))
    k_past, v_past = _mla_kv(past_ckv.reshape(DEC_BATCH * PAST_LEN, MLA_KV_LORA), 0, lp['mla_kv_norm'], w_ukv,
                             k_gain, kpe_past, cos_past, sin_past, 512, False)
    b_p = _attn_prompt(q, k_new, v_new, BATCH, SEQ, 512)
    b_s = _attn_sample(q, k_new, v_new, k_past, v_past, N_PROMPT)

    zeros_ret = jnp.zeros((BATCH, RET_HEADS, RET_DK, RET_DV), F32)
    c_p, p_ret = _retention_branch(proj, cos_ret, sin_ret, zeros_ret, BATCH, SEQ, 256, 0)
    c_s, s_ret = _retention_branch(proj, cos_ret, sin_ret, ret_s0, DEC_BATCH, DEC_SEQ, DEC_SEQ, N_PROMPT)

    cat = lambda p, s: jnp.concatenate([p, s], axis=0)
    merged = _merge(cat(a_p, a_s), cat(b_p, b_s), cat(c_p, c_s), lp['w_branch'].astype(BF16), proj, 512, 512)
    x = _res_matmul(merged, lp['w_out'].astype(BF16), x, 512, 512)

    mem_k_p, mem_v_p = _memkv(mem_prompt, lp['norm_mem'], lp['w_ck'].astype(BF16), lp['w_cv'].astype(BF16),
                              lp['cross_k_gain'], 256)
    qc = _rms_proj(x, lp['norm_cross'], lp['w_cq'].astype(BF16), BF16, 512, MEM_WIDTH)[0]
    o_p = _cross_attn(qc, mem_k_p.reshape(BATCH, N_MEM, MEM_WIDTH), mem_v_p.reshape(BATCH, N_MEM, MEM_WIDTH),
                      lp['cross_q_gain'], BATCH, SEQ, 512, 0)
    o_s = _cross_attn(qc, mem_k_s.reshape(DEC_BATCH, N_MEM, MEM_WIDTH), mem_v_s.reshape(DEC_BATCH, N_MEM, MEM_WIDTH),
                      lp['cross_q_gain'], DEC_BATCH, DEC_SEQ, DEC_SEQ, N_PROMPT)
    x = _res_matmul(cat(o_p, o_s), lp['w_co'].astype(BF16), x, 512, 512)

    x = _hier_moe(x, lp)

    kpe_rows = kpe[:, :MLA_ROPE]
    state_p = (c_kv[:N_PROMPT].reshape(BATCH, SEQ, MLA_KV_LORA), kpe_rows[:N_PROMPT].reshape(BATCH, SEQ, MLA_ROPE),
               p_s5_re, p_s5_im, p_ret,
               mem_k_p.reshape(BATCH, N_MEM, MEM_HEADS, MEM_HEAD_DIM),
               mem_v_p.reshape(BATCH, N_MEM, MEM_HEADS, MEM_HEAD_DIM))
    state_s = (c_kv[N_PROMPT:].reshape(DEC_BATCH, DEC_SEQ, MLA_KV_LORA),
               kpe_rows[N_PROMPT:].reshape(DEC_BATCH, DEC_SEQ, MLA_ROPE), s_s5_re, s_s5_im, s_ret)
    return x, state_p, state_s


_LAYER_PARAMS = (
    'norm_mix', 'w_in', 's5_a_re', 's5_a_im', 's5_log_dt', 's5_b_re', 's5_b_im', 's5_c_re', 's5_c_im', 's5_d',
    's5_w_glu', 'mla_q_norm', 'mla_w_uq', 'mla_kv_norm', 'mla_w_ukv', 'mla_q_gain', 'mla_k_gain', 'w_branch',
    'w_out', 'norm_cross', 'norm_mem', 'w_cq', 'w_ck', 'w_cv', 'cross_q_gain', 'cross_k_gain', 'w_co', 'norm_ffn',
    'w_group_router', 'b_group', 'w_expert_router', 'b_expert', 'w_gate_up', 'w_down')


def kernel(x_prompt, x_sample, mem_prompt, cache_mla_ckv, cache_mla_kpe, state_s5_re, state_s5_im, state_ret,
           cache_mem_k, cache_mem_v, norm_mix, w_in, s5_a_re, s5_a_im, s5_log_dt, s5_b_re, s5_b_im, s5_c_re,
           s5_c_im, s5_d, s5_w_glu, mla_q_norm, mla_w_uq, mla_kv_norm, mla_w_ukv, mla_q_gain, mla_k_gain,
           w_branch, w_out, norm_cross, norm_mem, w_cq, w_ck, w_cv, cross_q_gain, cross_k_gain, w_co, norm_ffn,
           w_group_router, b_group, w_expert_router, b_expert, w_gate_up, w_down):
    params = dict(zip(_LAYER_PARAMS, (
        norm_mix, w_in, s5_a_re, s5_a_im, s5_log_dt, s5_b_re, s5_b_im, s5_c_re, s5_c_im, s5_d, s5_w_glu,
        mla_q_norm, mla_w_uq, mla_kv_norm, mla_w_ukv, mla_q_gain, mla_k_gain, w_branch, w_out, norm_cross,
        norm_mem, w_cq, w_ck, w_cv, cross_q_gain, cross_k_gain, w_co, norm_ffn, w_group_router, b_group,
        w_expert_router, b_expert, w_gate_up, w_down)))
    assert x_prompt.shape == (BATCH, SEQ, D_MODEL) and x_sample.shape == (DEC_BATCH, DEC_SEQ, D_MODEL)
    assert cache_mla_ckv.shape == (DEPTH, DEC_BATCH, PAST_LEN, MLA_KV_LORA)

    pos_p = jnp.arange(SEQ, dtype=jnp.int32)
    pos_s = PAST_LEN + jnp.arange(DEC_SEQ, dtype=jnp.int32)
    pos_rows = jnp.concatenate([jnp.tile(pos_p, BATCH), jnp.tile(pos_s, DEC_BATCH)])
    pos_past = jnp.tile(jnp.arange(PAST_LEN, dtype=jnp.int32), DEC_BATCH)
    tables = _rope_tables(pos_rows, MLA_ROPE) + _rope_tables(pos_past, MLA_ROPE) + _rope_tables(pos_rows, RET_DK)

    x = jnp.concatenate([x_prompt.reshape(N_PROMPT, D_MODEL), x_sample.reshape(N_SAMPLE, D_MODEL)], axis=0)
    mem2d = mem_prompt.reshape(BATCH * N_MEM, D_MODEL)
    outs_p, outs_s = [], []
    for l in range(DEPTH):
        lp = {name: value[l] for name, value in params.items()}
        past = (cache_mla_ckv[l], cache_mla_kpe[l], state_s5_re[l], state_s5_im[l], state_ret[l],
                cache_mem_k[l], cache_mem_v[l])
        x, st_p, st_s = _layer(x, mem2d, lp, past, tables)
        outs_p.append(st_p)
        outs_s.append(st_s)
    stack = lambda outs, i: jnp.stack([o[i] for o in outs])
    return ((x[:N_PROMPT].reshape(BATCH, SEQ, D_MODEL), x[N_PROMPT:].reshape(DEC_BATCH, DEC_SEQ, D_MODEL))
            + tuple(stack(outs_p, i) for i in range(7)) + tuple(stack(outs_s, i) for i in range(5)))
```

```python
import functools
import math

import jax
import jax.numpy as jnp
from jax import lax
from jax.experimental import pallas as pl
from jax.experimental.pallas import tpu as pltpu

F32 = jnp.float32
BF16 = jnp.bfloat16

D_MODEL = 2048
BATCH = 4
SEQ = 2048
DEPTH = 2
DEC_BATCH = 32
DEC_SEQ = 32
PAST_LEN = 1024
CHUNK = 64
RMS_EPS = 1e-6
ROPE_THETA = 10000.0
NEG_INF = -1e30

S5_WIDTH = 1024
S5_GROUP_CH = 16
S5_GROUPS = 64
S5_STATE = 64
S5_COLS = S5_GROUPS * S5_STATE

MLA_HEADS = 8
MLA_NOPE = 128
MLA_ROPE = 64
MLA_QK = MLA_NOPE + MLA_ROPE
MLA_V = 128
MLA_Q_LORA = 512
MLA_KV_LORA = 512
MLA_HEAD_PAD = 256
MLA_QK_PAD = MLA_HEADS * MLA_HEAD_PAD

RET_HEADS = 4
RET_DK = 128
RET_DV = 256
BRANCH_WIDTH = 1024
N_BRANCH = 3

N_MEM = 256
MEM_HEADS = 4
MEM_HEAD_DIM = 128
MEM_WIDTH = MEM_HEADS * MEM_HEAD_DIM

MOE_GROUPS = 4
MOE_PER_GROUP = 8
N_EXPERTS = 32
MOE_TOP_K = 2
EXPERT_FF = 512

N_PROMPT = BATCH * SEQ
N_SAMPLE = DEC_BATCH * DEC_SEQ
N_ROWS = N_PROMPT + N_SAMPLE

COL_GMIX = 0
COL_S5 = 6144
COL_RV = 7168
COL_RG = 8192
COL_QLAT = 9216
COL_KVLAT = 9728
COL_RQ = 10240
COL_RK = 10752
PROJ_COLS = 11264
IN_S5 = (0, 1024)
IN_QLAT = (1024, 1536)
IN_KVLAT = (1536, 2048)
IN_KPE = (2048, 2112)
IN_RQ = (2112, 2624)
IN_RK = (2624, 3136)
IN_RV = (3136, 4160)
IN_RG = (4160, 5184)
IN_GMIX = (5184, 11328)

LANES = 128
EXPERT_ROWS = 256
N_PAIRS = N_ROWS * MOE_TOP_K
N_EXPERT_BLOCKS = -(-(N_PAIRS + N_EXPERTS * (EXPERT_ROWS - 1)) // EXPERT_ROWS)
VMEM_LIMIT = 56 * 1024 * 1024


def _cparams(sem):
    return pltpu.CompilerParams(dimension_semantics=sem, vmem_limit_bytes=VMEM_LIMIT)


def _rms_rows(x, g):
    r = lax.rsqrt(jnp.mean(x * x, axis=-1, keepdims=True) + RMS_EPS)
    return (x * r) * g


def _dot(a, b):
    return jnp.dot(a, b, preferred_element_type=F32)


def _rows_call(kernel_fn, into, *, in_specs, args, **kw):
    if into is None:
        out0 = kw['out_shape'][0] if isinstance(kw['out_shape'], (list, tuple)) else kw['out_shape']
        into = jnp.zeros(out0.shape, out0.dtype)

    def aliased_kernel(into_ref, *refs):
        del into_ref
        kernel_fn(*refs)

    return pl.pallas_call(aliased_kernel, in_specs=[pl.BlockSpec(memory_space=pl.ANY)] + in_specs,
                          input_output_aliases={0: 0}, **kw)(into, *args)


def _dot_nt(a, b):
    return lax.dot_general(a, b, (((1,), (1,)), ((), ())), preferred_element_type=F32)


def _rms_proj_kernel(x_ref, g_ref, w_ref, *refs, side):
    if side:
        w2_ref, o_ref, o2_ref, xn_ref = refs
    else:
        o_ref, xn_ref = refs

    @pl.when(pl.program_id(1) == 0)
    def _():
        xn_ref[...] = _rms_rows(x_ref[...], g_ref[...]).astype(BF16)
        if side:
            o2_ref[...] = _dot(xn_ref[...], w2_ref[...]).astype(o2_ref.dtype)

    o_ref[...] = _dot(xn_ref[...], w_ref[...]).astype(o_ref.dtype)


def _rms_proj(x, g, w, out_dtype, tm, tn, w2=None, out2_dtype=None):
    n, k = x.shape
    cols = w.shape[1]
    side = w2 is not None
    in_specs = [
        pl.BlockSpec((tm, k), lambda i, j: (i, 0)),
        pl.BlockSpec((1, k), lambda i, j: (0, 0)),
        pl.BlockSpec((k, tn), lambda i, j: (0, j)),
    ]
    out_specs = [pl.BlockSpec((tm, tn), lambda i, j: (i, j))]
    out_shape = [jax.ShapeDtypeStruct((n, cols), out_dtype)]
    args = [x, g.reshape(1, k), w]
    if side:
        cols2 = w2.shape[1]
        in_specs.append(pl.BlockSpec((k, cols2), lambda i, j: (0, 0)))
        out_specs.append(pl.BlockSpec((tm, cols2), lambda i, j: (i, 0)))
        out_shape.append(jax.ShapeDtypeStruct((n, cols2), out2_dtype))
        args.append(w2)
    return pl.pallas_call(
        functools.partial(_rms_proj_kernel, side=side),
        grid=(n // tm, cols // tn),
        in_specs=in_specs,
        out_specs=out_specs,
        out_shape=out_shape,
        scratch_shapes=[pltpu.VMEM((tm, k), BF16)],
        compiler_params=_cparams(("parallel", "arbitrary")),
        name="rms_proj_side" if side else "rms_proj",
    )(*args)


def _res_matmul_kernel(a_ref, w_ref, r_ref, o_ref):
    o_ref[...] = r_ref[...] + _dot(a_ref[...], w_ref[...])


def _res_matmul(a, w, res, tm, tn):
    n, k = a.shape
    cols = w.shape[1]
    return pl.pallas_call(
        _res_matmul_kernel,
        grid=(n // tm, cols // tn),
        in_specs=[
            pl.BlockSpec((tm, k), lambda i, j: (i, 0)),
            pl.BlockSpec((k, tn), lambda i, j: (0, j)),
            pl.BlockSpec((tm, tn), lambda i, j: (i, j)),
        ],
        out_specs=pl.BlockSpec((tm, tn), lambda i, j: (i, j)),
        out_shape=jax.ShapeDtypeStruct((n, cols), F32),
        compiler_params=_cparams(("parallel", "arbitrary")),
        name="res_matmul",
    )(a, w, res)


S5_K_SLAB = 256
S5_N_SLAB = 1024
S5_SLABS = S5_WIDTH // S5_K_SLAB
S5_SCAN_COLS = 512


def _gelu_tanh(x):
    return 0.5 * x * (1.0 + jnp.tanh(math.sqrt(2.0 / math.pi) * (x + 0.044715 * (x * x * x))))


def _sigmoid(x):
    return 1.0 / (1.0 + jnp.exp(-x))


def _s5_kernel(u_ref, bre_ref, bim_ref, cre_ref, cim_ref, ar_ref, ai_ref, d_ref, wglu_ref,
               h0r_ref, h0i_ref, o_ref, hr_out_ref, hi_out_ref, sre_ref, sim_ref, cr_ref, ci_ref, *, tt):
    @pl.when(pl.program_id(1) == 0)
    def _():
        cr_ref[...] = h0r_ref[0]
        ci_ref[...] = h0i_ref[0]

    u = u_ref[...]
    for n in range(S5_SLABS):
        un = u[:, S5_K_SLAB * n:S5_K_SLAB * (n + 1)]
        sre_ref[:, S5_N_SLAB * n:S5_N_SLAB * (n + 1)] = _dot(un, bre_ref[n])
        sim_ref[:, S5_N_SLAB * n:S5_N_SLAB * (n + 1)] = _dot(un, bim_ref[n])

    for c in range(S5_COLS // S5_SCAN_COLS):
        sl = slice(c * S5_SCAN_COLS, (c + 1) * S5_SCAN_COLS)
        ar = ar_ref[:, sl]
        ai = ai_ref[:, sl]

        def body(r, carry, sl=sl, ar=ar, ai=ai):
            hr, hi = carry
            nr = ar * hr - ai * hi + sre_ref[pl.ds(r, 1), sl]
            ni = ar * hi + ai * hr + sim_ref[pl.ds(r, 1), sl]
            sre_ref[pl.ds(r, 1), sl] = nr
            sim_ref[pl.ds(r, 1), sl] = ni
            return nr, ni

        hr, hi = lax.fori_loop(0, tt, body, (cr_ref[:, sl], ci_ref[:, sl]))
        cr_ref[:, sl] = hr
        ci_ref[:, sl] = hi

    ys = []
    for n in range(S5_SLABS):
        hr_n = sre_ref[:, S5_N_SLAB * n:S5_N_SLAB * (n + 1)].astype(BF16)
        hi_n = sim_ref[:, S5_N_SLAB * n:S5_N_SLAB * (n + 1)].astype(BF16)
        ys.append(_dot(hr_n, cre_ref[n]) + _dot(hi_n, cim_ref[n]))
    y = jnp.concatenate(ys, axis=1) + d_ref[...] * u.astype(F32)
    z = _gelu_tanh(y)
    gate = _sigmoid(_dot(z.astype(BF16), wglu_ref[...]))
    o_ref[...] = (z * gate).astype(o_ref.dtype)
    hr_out_ref[0] = cr_ref[...]
    hi_out_ref[0] = ci_ref[...]


def _s5_tables(lp):
    a_re, a_im = lp['s5_a_re'], lp['s5_a_im']
    dt = jnp.exp(lp['s5_log_dt'])[:, None]
    mag = jnp.exp(a_re * dt)
    lb_re = mag * jnp.cos(a_im * dt)
    lb_im = mag * jnp.sin(a_im * dt)
    den = a_re * a_re + a_im * a_im
    n_re = lb_re - 1.0
    co_re = (n_re * a_re + lb_im * a_im) / den
    co_im = (lb_im * a_re - n_re * a_im) / den
    bb_re = co_re[..., None] * lp['s5_b_re'] - co_im[..., None] * lp['s5_b_im']
    bb_im = co_re[..., None] * lp['s5_b_im'] + co_im[..., None] * lp['s5_b_re']
    per_slab = S5_GROUPS // S5_SLABS
    eye = jnp.eye(per_slab, dtype=F32)

    def b_tiles(bb):
        bb = bb.reshape(S5_SLABS, per_slab, S5_STATE, S5_GROUP_CH)
        return (bb.transpose(0, 1, 3, 2)[:, :, :, None, :] * eye[None, :, None, :, None]).reshape(
            S5_SLABS, S5_K_SLAB, S5_N_SLAB).astype(BF16)

    def c_tiles(cc):
        cc = cc.reshape(S5_SLABS, per_slab, S5_GROUP_CH, S5_STATE)
        return (cc.transpose(0, 1, 3, 2)[:, :, :, None, :] * eye[None, :, None, :, None]).reshape(
            S5_SLABS, S5_N_SLAB, S5_K_SLAB).astype(BF16)

    return dict(
        bre=b_tiles(bb_re), bim=b_tiles(bb_im),
        cre=c_tiles(lp['s5_c_re']), cim=c_tiles(-lp['s5_c_im']),
        ar=lb_re.reshape(1, S5_COLS), ai=lb_im.reshape(1, S5_COLS),
        d=lp['s5_d'].reshape(1, S5_WIDTH),
    )


def _s5_branch(proj, tabs, wglu, h0_re, h0_im, n_seq, seq_len, tt, row0, into=None):
    nt = seq_len // tt
    base = row0 // tt
    col = COL_S5 // S5_WIDTH
    const3 = lambda s, t: (0, 0, 0)
    const2 = lambda s, t: (0, 0)
    out, hr, hi = _rows_call(
        functools.partial(_s5_kernel, tt=tt), into,
        grid=(n_seq, nt),
        in_specs=[
            pl.BlockSpec((tt, S5_WIDTH), lambda s, t: (base + s * nt + t, col)),
            pl.BlockSpec((S5_SLABS, S5_K_SLAB, S5_N_SLAB), const3),
            pl.BlockSpec((S5_SLABS, S5_K_SLAB, S5_N_SLAB), const3),
            pl.BlockSpec((S5_SLABS, S5_N_SLAB, S5_K_SLAB), const3),
            pl.BlockSpec((S5_SLABS, S5_N_SLAB, S5_K_SLAB), const3),
            pl.BlockSpec((1, S5_COLS), const2),
            pl.BlockSpec((1, S5_COLS), const2),
            pl.BlockSpec((1, S5_WIDTH), const2),
            pl.BlockSpec((S5_WIDTH, S5_WIDTH), const2),
            pl.BlockSpec((1, 1, S5_COLS), lambda s, t: (s, 0, 0)),
            pl.BlockSpec((1, 1, S5_COLS), lambda s, t: (s, 0, 0)),
        ],
        out_specs=[
            pl.BlockSpec((tt, S5_WIDTH), lambda s, t: (base + s * nt + t, 0)),
            pl.BlockSpec((1, 1, S5_COLS), lambda s, t: (s, 0, 0)),
            pl.BlockSpec((1, 1, S5_COLS), lambda s, t: (s, 0, 0)),
        ],
        out_shape=[
            jax.ShapeDtypeStruct((N_ROWS, S5_WIDTH), BF16),
            jax.ShapeDtypeStruct((n_seq, 1, S5_COLS), F32),
            jax.ShapeDtypeStruct((n_seq, 1, S5_COLS), F32),
        ],
        scratch_shapes=[
            pltpu.VMEM((tt, S5_COLS), F32), pltpu.VMEM((tt, S5_COLS), F32),
            pltpu.VMEM((1, S5_COLS), F32), pltpu.VMEM((1, S5_COLS), F32),
        ],
        compiler_params=_cparams(("parallel", "arbitrary")),
        name="s5_branch",
        args=(proj, tabs['bre'], tabs['bim'], tabs['cre'], tabs['cim'], tabs['ar'], tabs['ai'], tabs['d'], wglu,
              h0_re.reshape(n_seq, 1, S5_COLS), h0_im.reshape(n_seq, 1, S5_COLS)),
    )
    return (out, hr.reshape(n_seq, S5_GROUPS, S5_STATE), hi.reshape(n_seq, S5_GROUPS, S5_STATE))


def _rope_tables(pos, d):
    inv_freq = ROPE_THETA ** (-jnp.arange(0, d, 2, dtype=F32) / d)
    ang = pos.astype(F32)[:, None] * inv_freq[None, :]
    cos, sin = jnp.cos(ang), jnp.sin(ang)
    pad = jnp.zeros((pos.shape[0], LANES - d), F32)
    return (jnp.concatenate([cos, cos, pad], axis=1), jnp.concatenate([-sin, sin, pad], axis=1))


def _pad_head_vec(g):
    return jnp.concatenate([g, jnp.zeros((MLA_HEAD_PAD - MLA_QK,), F32)]).reshape(1, MLA_HEAD_PAD)


def _qk_head(nope, pe, gain, cos2, sin2, scale):
    ss = jnp.sum(nope * nope, axis=-1, keepdims=True) + jnp.sum(pe * pe, axis=-1, keepdims=True)
    r = lax.rsqrt(ss * (1.0 / MLA_QK) + RMS_EPS)
    nope = (nope * r) * gain[:, :MLA_NOPE]
    pe = (pe * r) * gain[:, MLA_NOPE:]
    lane = lax.broadcasted_iota(jnp.int32, pe.shape, 1)
    half = MLA_ROPE // 2
    swap = jnp.where(lane < half, pltpu.roll(pe, LANES - half, 1), pltpu.roll(pe, half, 1))
    pe = pe * cos2 + swap * sin2
    return nope * scale, pe * scale


def _mla_q_kernel(lat_ref, g_ref, w_ref, gain_ref, cos_ref, sin_ref, o_ref):
    xn = _rms_rows(lat_ref[...].astype(F32), g_ref[...]).astype(BF16)
    q = _dot(xn, w_ref[...])
    gain = gain_ref[...]
    cos2 = cos_ref[...]
    sin2 = sin_ref[...]
    for h in range(MLA_HEADS):
        c0 = h * MLA_HEAD_PAD
        nope, pe = _qk_head(q[:, c0:c0 + MLA_NOPE], q[:, c0 + MLA_NOPE:c0 + MLA_HEAD_PAD],
                            gain, cos2, sin2, MLA_QK ** -0.5)
        o_ref[:, c0:c0 + MLA_NOPE] = nope.astype(BF16)
        o_ref[:, c0 + MLA_NOPE:c0 + MLA_HEAD_PAD] = pe.astype(BF16)


def _mla_q(proj, g, w_uq_pad, gain_pad, cos2, sin2, tab_block, tm):
    n = proj.shape[0]
    col = COL_QLAT // MLA_Q_LORA
    c2 = lambda i: (0, 0)
    tab = lambda i: (tab_block(i), 0)
    return pl.pallas_call(
        _mla_q_kernel,
        grid=(n // tm,),
        in_specs=[
            pl.BlockSpec((tm, MLA_Q_LORA), lambda i: (i, col)),
            pl.BlockSpec((1, MLA_Q_LORA), c2),
            pl.BlockSpec((MLA_Q_LORA, MLA_QK_PAD), c2),
            pl.BlockSpec((1, MLA_HEAD_PAD), c2),
            pl.BlockSpec((tm, LANES), tab),
            pl.BlockSpec((tm, LANES), tab),
        ],
        out_specs=pl.BlockSpec((tm, MLA_QK_PAD), lambda i: (i, 0)),
        out_shape=jax.ShapeDtypeStruct((n, MLA_QK_PAD), BF16),
        compiler_params=_cparams(("parallel",)),
        name="mla_q",
    )(proj, g.reshape(1, -1), w_uq_pad, gain_pad, cos2, sin2)


def _mla_kv_kernel(lat_ref, g_ref, w_ref, gain_ref, kpe_ref, cos_ref, sin_ref, *out_refs, normalize):
    lat = lat_ref[...].astype(F32)
    if normalize:
        ckv_ref, k_ref, v_ref = out_refs
        lat = _rms_rows(lat, g_ref[...])
        ckv_ref[...] = lat
    else:
        k_ref, v_ref = out_refs
    kv = _dot(lat.astype(BF16), w_ref[...])
    gain = gain_ref[...]
    kpe = kpe_ref[...]
    cos2 = cos_ref[...]
    sin2 = sin_ref[...]
    for h in range(MLA_HEADS):
        c0 = h * (MLA_NOPE + MLA_V)
        nope, pe = _qk_head(kv[:, c0:c0 + MLA_NOPE], kpe, gain, cos2, sin2, 1.0)
        k0 = h * MLA_HEAD_PAD
        k_ref[:, k0:k0 + MLA_NOPE] = nope.astype(BF16)
        k_ref[:, k0 + MLA_NOPE:k0 + MLA_HEAD_PAD] = pe.astype(BF16)
        v_ref[:, h * MLA_V:(h + 1) * MLA_V] = kv[:, c0 + MLA_NOPE:c0 + MLA_NOPE + MLA_V].astype(BF16)


def _mla_kv(lat, lat_row0, lat_col, n, g, w_ukv, gain_pad, kpe_pad, kpe_row0, cos2, sin2, tab_block, tm, normalize):
    c2 = lambda i: (0, 0)
    row = lambda i: (i, 0)
    tab = lambda i: (tab_block(i), 0)
    out_specs = [pl.BlockSpec((tm, MLA_QK_PAD), row), pl.BlockSpec((tm, MLA_HEADS * MLA_V), row)]
    out_shape = [jax.ShapeDtypeStruct((n, MLA_QK_PAD), BF16), jax.ShapeDtypeStruct((n, MLA_HEADS * MLA_V), BF16)]
    if normalize:
        out_specs = [pl.BlockSpec((tm, MLA_KV_LORA), row)] + out_specs
        out_shape = [jax.ShapeDtypeStruct((n, MLA_KV_LORA), F32)] + out_shape
    return pl.pallas_call(
        functools.partial(_mla_kv_kernel, normalize=normalize),
        grid=(n // tm,),
        in_specs=[
            pl.BlockSpec((tm, MLA_KV_LORA), lambda i: (lat_row0 // tm + i, lat_col)),
            pl.BlockSpec((1, MLA_KV_LORA), c2),
            pl.BlockSpec((MLA_KV_LORA, MLA_HEADS * (MLA_NOPE + MLA_V)), c2),
            pl.BlockSpec((1, MLA_HEAD_PAD), c2),
            pl.BlockSpec((tm, LANES), lambda i: (kpe_row0 // tm + i, 0)),
            pl.BlockSpec((tm, LANES), tab),
            pl.BlockSpec((tm, LANES), tab),
        ],
        out_specs=out_specs,
        out_shape=out_shape,
        compiler_params=_cparams(("parallel",)),
        name="mla_kv_norm" if normalize else "mla_kv_past",
    )(lat, g.reshape(1, -1), w_ukv, gain_pad, kpe_pad, cos2, sin2)


def _attn_prompt_kernel(q_ref, k_ref, v_ref, o_ref, m_ref, l_ref, acc_ref, *, tq):
    qi = pl.program_id(2)
    ki = pl.program_id(3)

    @pl.when(ki == 0)
    def _():
        m_ref[...] = jnp.full(m_ref.shape, NEG_INF, F32)
        l_ref[...] = jnp.zeros(l_ref.shape, F32)
        acc_ref[...] = jnp.zeros(acc_ref.shape, F32)

    @pl.when(ki <= qi)
    def _():
        s = _dot_nt(q_ref[...], k_ref[...])
        row_chunk = lax.broadcasted_iota(jnp.int32, s.shape, 0) // CHUNK
        col_chunk = lax.broadcasted_iota(jnp.int32, s.shape, 1) // CHUNK
        visible = jnp.logical_or(col_chunk <= row_chunk, ki < qi)
        s = jnp.where(visible, s, NEG_INF)
        m_old = m_ref[...]
        m_new = jnp.maximum(m_old, jnp.max(s, axis=-1, keepdims=True))
        alpha = jnp.exp(m_old - m_new)
        p = jnp.exp(s - m_new)
        l_ref[...] = alpha * l_ref[...] + jnp.sum(p, axis=-1, keepdims=True)
        acc_ref[...] = alpha * acc_ref[...] + _dot(p.astype(BF16), v_ref[...])
        m_ref[...] = m_new

    @pl.when(ki == pl.num_programs(3) - 1)
    def _():
        o_ref[...] = (acc_ref[...] / l_ref[...]).astype(o_ref.dtype)


def _attn_prompt(q, k, v, n_batch, seq_len, tq):
    assert tq % CHUNK == 0
    nq = seq_len // tq
    return _rows_call(
        functools.partial(_attn_prompt_kernel, tq=tq), None,
        grid=(n_batch, MLA_HEADS, nq, nq),
        in_specs=[
            pl.BlockSpec((tq, MLA_HEAD_PAD), lambda b, h, qi, ki: (b * nq + qi, h)),
            pl.BlockSpec((tq, MLA_HEAD_PAD), lambda b, h, qi, ki: (b * nq + jnp.minimum(ki, qi), h)),
            pl.BlockSpec((tq, MLA_V), lambda b, h, qi, ki: (b * nq + jnp.minimum(ki, qi), h)),
        ],
        out_specs=pl.BlockSpec((tq, MLA_V), lambda b, h, qi, ki: (b * nq + qi, h)),
        out_shape=jax.ShapeDtypeStruct((N_ROWS, MLA_HEADS * MLA_V), BF16),
        scratch_shapes=[pltpu.VMEM((tq, 1), F32), pltpu.VMEM((tq, 1), F32), pltpu.VMEM((tq, MLA_V), F32)],
        compiler_params=_cparams(("parallel", "parallel", "parallel", "arbitrary")),
        name="attn_prompt",
        args=(q, k, v),
    )


def _attn_sample_kernel(q_ref, kp_ref, kn_ref, vp_ref, vn_ref, o_ref):
    for h in range(MLA_HEADS):
        ks = slice(h * MLA_HEAD_PAD, (h + 1) * MLA_HEAD_PAD)
        vs = slice(h * MLA_V, (h + 1) * MLA_V)
        q = q_ref[:, ks]
        s_p = _dot_nt(q, kp_ref[:, ks])
        s_n = _dot_nt(q, kn_ref[:, ks])
        m = jnp.maximum(jnp.max(s_p, axis=-1, keepdims=True), jnp.max(s_n, axis=-1, keepdims=True))
        p_p = jnp.exp(s_p - m)
        p_n = jnp.exp(s_n - m)
        l = jnp.sum(p_p, axis=-1, keepdims=True) + jnp.sum(p_n, axis=-1, keepdims=True)
        o = _dot(p_p.astype(BF16), vp_ref[:, vs]) + _dot(p_n.astype(BF16), vn_ref[:, vs])
        o_ref[:, vs] = (o / l).astype(o_ref.dtype)


def _attn_sample(q, k_new, v_new, k_past, v_past, row0, into):
    assert (PAST_LEN + DEC_SEQ - 1) // CHUNK <= PAST_LEN // CHUNK
    base = row0 // DEC_SEQ
    new = lambda b: (base + b, 0)
    past = lambda b: (b, 0)
    return _rows_call(
        _attn_sample_kernel, into,
        grid=(DEC_BATCH,),
        in_specs=[
            pl.BlockSpec((DEC_SEQ, MLA_QK_PAD), new),
            pl.BlockSpec((PAST_LEN, MLA_QK_PAD), past),
            pl.BlockSpec((DEC_SEQ, MLA_QK_PAD), new),
            pl.BlockSpec((PAST_LEN, MLA_HEADS * MLA_V), past),
            pl.BlockSpec((DEC_SEQ, MLA_HEADS * MLA_V), new),
        ],
        out_specs=pl.BlockSpec((DEC_SEQ, MLA_HEADS * MLA_V), new),
        out_shape=jax.ShapeDtypeStruct((N_ROWS, MLA_HEADS * MLA_V), BF16),
        compiler_params=_cparams(("parallel",)),
        name="attn_sample",
        args=(q, k_past, k_new, v_past, v_new),
    )


def _retention_tables(block):
    log_g = jnp.log1p(-jnp.exp2(-5.0 - jnp.arange(RET_HEADS, dtype=F32)))
    idx = jnp.arange(block, dtype=F32)
    diff = idx[:, None] - idx[None, :]
    decay = jnp.where(diff >= 0, jnp.exp(log_g[:, None, None] * jnp.maximum(diff, 0.0)), 0.0)
    q_decay = jnp.exp(log_g[:, None] * (idx + 1.0))
    k_decay = jnp.exp(log_g[:, None] * (block - 1.0 - idx))
    blk_decay = jnp.exp(log_g * block)
    return (decay,
            jnp.broadcast_to(q_decay[:, :, None], (RET_HEADS, block, RET_DK)),
            jnp.broadcast_to(k_decay[:, :, None], (RET_HEADS, block, RET_DK)),
            jnp.broadcast_to(blk_decay[:, None, None], (RET_HEADS, 1, RET_DV)))


def _rope128(x, cos, sin):
    return x * cos + pltpu.roll(x, RET_DK // 2, 1) * sin


def _retention_kernel(rq_ref, rk_ref, rv_ref, rg_ref, cos_ref, sin_ref, dec_ref, qd_ref, kd_ref, bd_ref, s0_ref,
                      o_ref, s_out_ref, s_ref):
    @pl.when(pl.program_id(2) == 0)
    def _():
        s_ref[...] = s0_ref[0, 0]

    cos = cos_ref[...]
    sin = sin_ref[...]
    q = _rope128(rq_ref[...].astype(F32), cos, sin) * (RET_DK ** -0.5)
    k = _rope128(rk_ref[...].astype(F32), cos, sin)
    v = rv_ref[...]
    s_old = s_ref[...]
    att = _dot_nt(q.astype(BF16), k.astype(BF16)) * dec_ref[0]
    o = _dot(att.astype(BF16), v) + _dot((q * qd_ref[0]).astype(BF16), s_old.astype(BF16))
    kd = (k * kd_ref[0]).astype(BF16)
    s_new = bd_ref[0] * s_old + lax.dot_general(kd, v, (((0,), (0,)), ((), ())), preferred_element_type=F32)
    s_ref[...] = s_new
    s_out_ref[0, 0] = s_new
    mu = jnp.mean(o, axis=-1, keepdims=True)
    c = o - mu
    o = c * lax.rsqrt(jnp.mean(c * c, axis=-1, keepdims=True) + RMS_EPS)
    g = rg_ref[...].astype(F32)
    o_ref[...] = (o * (g * _sigmoid(g))).astype(o_ref.dtype)


def _retention_branch(proj, cos, sin, tab_block, s0, s0_base, n_seq, seq_len, block, row0, into=None):
    nblk = seq_len // block
    base = row0 // block
    dec, qd, kd, bd = _retention_tables(block)
    rows = lambda b, h, t: base + b * nblk + t
    tab = lambda b, h, t: (tab_block(t), 0)
    head3 = lambda b, h, t: (h, 0, 0)
    state = lambda b, h, t: (b, h, 0, 0)
    out, s_last = _rows_call(
        _retention_kernel, into,
        grid=(n_seq, RET_HEADS, nblk),
        in_specs=[
            pl.BlockSpec((block, RET_DK), lambda b, h, t: (rows(b, h, t), COL_RQ // RET_DK + h)),
            pl.BlockSpec((block, RET_DK), lambda b, h, t: (rows(b, h, t), COL_RK // RET_DK + h)),
            pl.BlockSpec((block, RET_DV), lambda b, h, t: (rows(b, h, t), COL_RV // RET_DV + h)),
            pl.BlockSpec((block, RET_DV), lambda b, h, t: (rows(b, h, t), COL_RG // RET_DV + h)),
            pl.BlockSpec((block, LANES), tab),
            pl.BlockSpec((block, LANES), tab),
            pl.BlockSpec((1, block, block), head3),
            pl.BlockSpec((1, block, RET_DK), head3),
            pl.BlockSpec((1, block, RET_DK), head3),
            pl.BlockSpec((1, 1, RET_DV), head3),
            pl.BlockSpec((1, 1, RET_DK, RET_DV), lambda b, h, t: (s0_base + b, h, 0, 0)),
        ],
        out_specs=[
            pl.BlockSpec((block, RET_DV), lambda b, h, t: (rows(b, h, t), h)),
            pl.BlockSpec((1, 1, RET_DK, RET_DV), state),
        ],
        out_shape=[
            jax.ShapeDtypeStruct((N_ROWS, RET_HEADS * RET_DV), BF16),
            jax.ShapeDtypeStruct((n_seq, RET_HEADS, RET_DK, RET_DV), F32),
        ],
        scratch_shapes=[pltpu.VMEM((RET_DK, RET_DV), F32)],
        compiler_params=_cparams(("parallel", "parallel", "arbitrary")),
        name="retention",
        args=(proj, proj, proj, proj, cos, sin, dec, qd, kd, bd, s0),
    )
    return out, s_last


def _merge_kernel(a_ref, b_ref, c_ref, w_ref, g0_ref, g1_ref, g2_ref, o_ref):
    acc = _sigmoid(g0_ref[...].astype(F32)) * _dot(a_ref[...], w_ref[0])
    acc += _sigmoid(g1_ref[...].astype(F32)) * _dot(b_ref[...], w_ref[1])
    acc += _sigmoid(g2_ref[...].astype(F32)) * _dot(c_ref[...], w_ref[2])
    o_ref[...] = acc.astype(o_ref.dtype)


def _merge(o_a, o_b, o_c, w_branch, proj, tm, tn):
    n = o_a.shape[0]
    nj = D_MODEL // tn
    br = pl.BlockSpec((tm, BRANCH_WIDTH), lambda i, j: (i, 0))
    gate = lambda b: pl.BlockSpec((tm, tn), lambda i, j: (i, COL_GMIX // tn + b * nj + j))
    return pl.pallas_call(
        _merge_kernel,
        grid=(n // tm, nj),
        in_specs=[br, br, br,
                  pl.BlockSpec((N_BRANCH, BRANCH_WIDTH, tn), lambda i, j: (0, 0, j)),
                  gate(0), gate(1), gate(2)],
        out_specs=pl.BlockSpec((tm, tn), lambda i, j: (i, j)),
        out_shape=jax.ShapeDtypeStruct((n, D_MODEL), BF16),
        compiler_params=_cparams(("parallel", "arbitrary")),
        name="branch_merge",
    )(o_a, o_b, o_c, w_branch, proj, proj, proj)


def _memkv_kernel(m_ref, g_ref, wk_ref, wv_ref, kg_ref, k_ref, v_ref):
    mn = _rms_rows(m_ref[...], g_ref[...]).astype(BF16)
    kk = _dot(mn, wk_ref[...])
    for h in range(MEM_HEADS):
        sl = slice(h * MEM_HEAD_DIM, (h + 1) * MEM_HEAD_DIM)
        k_ref[:, sl] = _rms_rows(kk[:, sl], kg_ref[...])
    v_ref[...] = _dot(mn, wv_ref[...])


def _memkv(mem, g, wk, wv, k_gain, tm):
    n, k = mem.shape
    c2 = lambda i: (0, 0)
    return pl.pallas_call(
        _memkv_kernel,
        grid=(n // tm,),
        in_specs=[
            pl.BlockSpec((tm, k), lambda i: (i, 0)),
            pl.BlockSpec((1, k), c2),
            pl.BlockSpec((k, MEM_WIDTH), c2),
            pl.BlockSpec((k, MEM_WIDTH), c2),
            pl.BlockSpec((1, MEM_HEAD_DIM), c2),
        ],
        out_specs=[pl.BlockSpec((tm, MEM_WIDTH), lambda i: (i, 0)), pl.BlockSpec((tm, MEM_WIDTH), lambda i: (i, 0))],
        out_shape=[jax.ShapeDtypeStruct((n, MEM_WIDTH), F32), jax.ShapeDtypeStruct((n, MEM_WIDTH), F32)],
        compiler_params=_cparams(("parallel",)),
        name="memory_kv",
    )(mem, g.reshape(1, k), wk, wv, k_gain.reshape(1, MEM_HEAD_DIM))


def _cross_attn_kernel(q_ref, k_ref, v_ref, qg_ref, o_ref, *, head_axis):
    for h in range(MEM_HEADS):
        sl = slice(h * MEM_HEAD_DIM, (h + 1) * MEM_HEAD_DIM)
        if head_axis:
            k, v = k_ref[0, :, h, :], v_ref[0, :, h, :]
        else:
            k, v = k_ref[0, :, sl], v_ref[0, :, sl]
        q = _rms_rows(q_ref[:, sl].astype(F32), qg_ref[...]) * (MEM_HEAD_DIM ** -0.5)
        s = _dot_nt(q.astype(BF16), k.astype(BF16))
        p = jnp.exp(s - jnp.max(s, axis=-1, keepdims=True))
        l = jnp.sum(p, axis=-1, keepdims=True)
        o = _dot(p.astype(BF16), v.astype(BF16))
        o_ref[:, sl] = (o / l).astype(o_ref.dtype)


def _cross_attn(qc, mem_k, mem_v, mem0, q_gain, n_batch, rows_per_batch, tm, row0, into=None):
    tiles = rows_per_batch // tm
    base = row0 // tm
    head_axis = mem_k.ndim == 4
    if head_axis:
        mem_spec = pl.BlockSpec((1, N_MEM, MEM_HEADS, MEM_HEAD_DIM), lambda b, t: (mem0 + b, 0, 0, 0))
    else:
        mem_spec = pl.BlockSpec((1, N_MEM, MEM_WIDTH), lambda b, t: (mem0 + b, 0, 0))
    return _rows_call(
        functools.partial(_cross_attn_kernel, head_axis=head_axis), into,
        grid=(n_batch, tiles),
        in_specs=[
            pl.BlockSpec((tm, MEM_WIDTH), lambda b, t: (base + b * tiles + t, 0)),
            mem_spec, mem_spec,
            pl.BlockSpec((1, MEM_HEAD_DIM), lambda b, t: (0, 0)),
        ],
        out_specs=pl.BlockSpec((tm, MEM_WIDTH), lambda b, t: (base + b * tiles + t, 0)),
        out_shape=jax.ShapeDtypeStruct((N_ROWS, MEM_WIDTH), BF16),
        compiler_params=_cparams(("parallel", "arbitrary")),
        name="cross_attn",
        args=(qc, mem_k, mem_v, q_gain.reshape(1, MEM_HEAD_DIM)),
    )


ROUTER_LANES = LANES
FAR_LANE = 4 * LANES


def _split_bf16(x):
    hi = x.astype(BF16)
    lo = (x - hi.astype(F32)).astype(BF16)
    return hi, lo


def _router_kernel(x_ref, g_ref, w_ref, b_ref, xn_ref, ids_ref, gates_ref):
    xn = _rms_rows(x_ref[...], g_ref[...])
    xn_ref[...] = xn
    xh, xl = _split_bf16(xn)
    wh, wl = _split_bf16(w_ref[...])
    logits = _dot(xh, wh) + (_dot(xh, wl) + _dot(xl, wh)) + b_ref[...]
    lane = lax.broadcasted_iota(jnp.int32, logits.shape, 1)

    def first_max(vals):
        top = jnp.max(vals, axis=-1, keepdims=True)
        idx = jnp.min(jnp.where(vals == top, lane, FAR_LANE), axis=-1, keepdims=True)
        return top, idx

    g_mask = jnp.logical_and(lane >= N_EXPERTS, lane < N_EXPERTS + MOE_GROUPS)
    g_top, g_lane = first_max(jnp.where(g_mask, logits, NEG_INF))
    p_group = 1.0 / jnp.sum(jnp.where(g_mask, jnp.exp(logits - g_top), 0.0), axis=-1, keepdims=True)
    g_sel = g_lane - N_EXPERTS
    e_mask = jnp.logical_and(lane < N_EXPERTS, lane // MOE_PER_GROUP == g_sel)
    e_vals = jnp.where(e_mask, logits, NEG_INF)
    e1_top, e1 = first_max(e_vals)
    e2_top, e2 = first_max(jnp.where(lane == e1, NEG_INF, e_vals))
    t = jnp.exp(e2_top - e1_top)
    gate1 = p_group / (1.0 + t)
    gate2 = p_group * t / (1.0 + t)
    ids_ref[...] = jnp.where(lane == 0, e1, jnp.where(lane == 1, e2, 0))
    gates_ref[...] = jnp.where(lane == 0, gate1, jnp.where(lane == 1, gate2, 0.0))


def _router(x, g, w_router, b_router, tm):
    n, k = x.shape
    c2 = lambda i: (0, 0)
    row = lambda i: (i, 0)
    return pl.pallas_call(
        _router_kernel,
        grid=(n // tm,),
        in_specs=[
            pl.BlockSpec((tm, k), row),
            pl.BlockSpec((1, k), c2),
            pl.BlockSpec((k, ROUTER_LANES), c2),
            pl.BlockSpec((1, ROUTER_LANES), c2),
        ],
        out_specs=[pl.BlockSpec((tm, k), row), pl.BlockSpec((tm, ROUTER_LANES), row),
                   pl.BlockSpec((tm, ROUTER_LANES), row)],
        out_shape=[jax.ShapeDtypeStruct((n, k), F32), jax.ShapeDtypeStruct((n, ROUTER_LANES), jnp.int32),
                   jax.ShapeDtypeStruct((n, ROUTER_LANES), F32)],
        compiler_params=_cparams(("parallel",)),
        name="moe_router",
    )(x, g.reshape(1, k), w_router, b_router)


def _route_plan(expert_idx):
    flat_e = expert_idx.reshape(N_PAIRS)
    onehot = (flat_e[:, None] == jnp.arange(N_EXPERTS, dtype=jnp.int32)[None, :]).astype(jnp.int32)
    csum = jnp.cumsum(onehot, axis=0)
    rank = jnp.take_along_axis(csum, flat_e[:, None], axis=1)[:, 0] - 1
    counts = csum[-1]
    padded = (counts + EXPERT_ROWS - 1) // EXPERT_ROWS * EXPERT_ROWS
    pad_ends = jnp.cumsum(padded)
    pad_starts = pad_ends - padded
    dest = (pad_starts[flat_e] + rank).astype(jnp.int32)
    n_used = (pad_ends[-1] // EXPERT_ROWS).astype(jnp.int32).reshape(1)
    block_start = jnp.arange(N_EXPERT_BLOCKS, dtype=jnp.int32) * EXPERT_ROWS
    block_e = jnp.sum((pad_ends[None, :] <= block_start[:, None]).astype(jnp.int32), axis=1)
    block_e = jnp.minimum(block_e, N_EXPERTS - 1)
    src = jnp.zeros((N_EXPERT_BLOCKS * EXPERT_ROWS,), jnp.int32).at[dest].set(
        jnp.arange(N_PAIRS, dtype=jnp.int32) // MOE_TOP_K)
    return dest, src, block_e, n_used


DMA_UNROLL = 8


def _experts_kernel(src_ref, be_ref, nu_ref, x_hbm, wgu_ref, wd_ref, o_ref, xbuf_ref, sem):
    i = pl.program_id(0)
    n_used = nu_ref[0]
    slot = lax.rem(i, 2)

    def row_copy(blk, buf, r):
        return pltpu.make_async_copy(x_hbm.at[pl.ds(src_ref[blk * EXPERT_ROWS + r], 1)],
                                     xbuf_ref.at[buf, pl.ds(r, 1)], sem.at[buf])

    def start_block(blk, buf):
        def body(c, carry):
            for u in range(DMA_UNROLL):
                row_copy(blk, buf, c * DMA_UNROLL + u).start(priority=u % 2)
            return carry
        lax.fori_loop(0, EXPERT_ROWS // DMA_UNROLL, body, 0)

    def wait_block(blk, buf):
        def body(c, carry):
            for u in range(DMA_UNROLL):
                row_copy(blk, buf, c * DMA_UNROLL + u).wait()
            return carry
        lax.fori_loop(0, EXPERT_ROWS // DMA_UNROLL, body, 0)

    @pl.when(jnp.logical_and(i == 0, n_used > 0))
    def _():
        start_block(0, 0)

    @pl.when(i + 1 < n_used)
    def _():
        start_block(i + 1, 1 - slot)

    @pl.when(i < n_used)
    def _():
        wait_block(i, slot)
        gu = _dot(xbuf_ref[slot].astype(BF16), wgu_ref[0, 0].astype(BF16))
        g = gu[:, :EXPERT_FF]
        h = (g * _sigmoid(g)) * gu[:, EXPERT_FF:]
        o_ref[...] = _dot(h.astype(BF16), wd_ref[0, 0].astype(BF16))

    @pl.when(i >= n_used)
    def _():
        o_ref[...] = jnp.zeros(o_ref.shape, o_ref.dtype)


def _experts(xn, src, block_e, n_used, w_gate_up, w_down, layer):
    return pl.pallas_call(
        _experts_kernel,
        grid_spec=pltpu.PrefetchScalarGridSpec(
            num_scalar_prefetch=3,
            grid=(N_EXPERT_BLOCKS,),
            in_specs=[
                pl.BlockSpec(memory_space=pl.ANY),
                pl.BlockSpec((1, 1, D_MODEL, 2 * EXPERT_FF), lambda i, src, be, nu: (layer, be[i], 0, 0)),
                pl.BlockSpec((1, 1, EXPERT_FF, D_MODEL), lambda i, src, be, nu: (layer, be[i], 0, 0)),
            ],
            out_specs=pl.BlockSpec((EXPERT_ROWS, D_MODEL), lambda i, src, be, nu: (i, 0)),
            scratch_shapes=[pltpu.VMEM((2, EXPERT_ROWS, D_MODEL), F32), pltpu.SemaphoreType.DMA((2,))],
        ),
        out_shape=jax.ShapeDtypeStruct((N_EXPERT_BLOCKS * EXPERT_ROWS, D_MODEL), F32),
        compiler_params=_cparams(("arbitrary",)),
        name="moe_experts",
    )(src, block_e, n_used, xn, w_gate_up, w_down)


def _combine_kernel(dest_ref, x_ref, gates_ref, y_hbm, o_ref, buf_ref, sem, *, rows):
    base = pl.program_id(0) * rows
    per_iter = DMA_UNROLL // MOE_TOP_K

    def row_copy(i, k):
        return pltpu.make_async_copy(y_hbm.at[pl.ds(dest_ref[(base + i) * MOE_TOP_K + k], 1)],
                                     buf_ref.at[k, pl.ds(i, 1)], sem)

    def start(c, carry):
        for u in range(per_iter):
            for k in range(MOE_TOP_K):
                row_copy(c * per_iter + u, k).start(priority=k)
        return carry

    def wait(c, carry):
        for u in range(per_iter):
            for k in range(MOE_TOP_K):
                row_copy(c * per_iter + u, k).wait()
        return carry

    lax.fori_loop(0, rows // per_iter, start, 0)
    lax.fori_loop(0, rows // per_iter, wait, 0)
    gates = gates_ref[...]
    y = gates[:, 0:1] * buf_ref[0] + gates[:, 1:2] * buf_ref[1]
    o_ref[...] = x_ref[...] + y


def _combine(x, gates, y_sorted, dest, rows):
    n, width = x.shape
    return pl.pallas_call(
        functools.partial(_combine_kernel, rows=rows),
        grid_spec=pltpu.PrefetchScalarGridSpec(
            num_scalar_prefetch=1,
            grid=(n // rows,),
            in_specs=[
                pl.BlockSpec((rows, width), lambda i, d: (i, 0)),
                pl.BlockSpec((rows, ROUTER_LANES), lambda i, d: (i, 0)),
                pl.BlockSpec(memory_space=pl.ANY),
            ],
            out_specs=pl.BlockSpec((rows, width), lambda i, d: (i, 0)),
            scratch_shapes=[pltpu.VMEM((MOE_TOP_K, rows, width), F32), pltpu.SemaphoreType.DMA(())],
        ),
        out_shape=jax.ShapeDtypeStruct((n, width), F32),
        compiler_params=_cparams(("arbitrary",)),
        name="moe_combine",
    )(dest, x, gates, y_sorted)


def _hier_moe(x, lp, w_gate_up, w_down, layer):
    w_router = jnp.concatenate(
        [lp['w_expert_router'], lp['w_group_router'],
         jnp.zeros((D_MODEL, ROUTER_LANES - N_EXPERTS - MOE_GROUPS), F32)], axis=1)
    b_router = jnp.concatenate(
        [lp['b_expert'], lp['b_group'], jnp.zeros((ROUTER_LANES - N_EXPERTS - MOE_GROUPS,), F32)]).reshape(1, -1)
    xn, ids, gates = _router(x, lp['norm_ffn'], w_router, b_router, 512)
    dest, src, block_e, n_used = _route_plan(ids[:, :MOE_TOP_K])
    y_sorted = _experts(xn, src, block_e, n_used, w_gate_up, w_down, layer)
    return _combine(x, gates, y_sorted, dest, 256)


def _cols(w, *ranges):
    return jnp.concatenate([w[:, a:b] for a, b in ranges], axis=1)


def _prompt_or_sample_block(i, tm):
    return jnp.where(i < N_PROMPT // tm, lax.rem(i, SEQ // tm), SEQ // tm)


def _layer(x, mem_prompt, lp, layer, caches, tables):
    (ckv_all, kpe_past_all, h0_re, h0_im, ret_s0_all, mem_k_all, mem_v_all, w_gate_up, w_down) = caches
    cos_mla, sin_mla, cos_past, sin_past, cos_ret, sin_ret = tables
    tm = 512
    w_in = lp['w_in']
    w_main = _cols(w_in, IN_GMIX, IN_S5, IN_RV, IN_RG, IN_QLAT, IN_KVLAT, IN_RQ, IN_RK).astype(BF16)
    w_kpe = jnp.pad(_cols(w_in, IN_KPE), ((0, 0), (0, LANES - MLA_ROPE))).astype(BF16)
    proj, kpe = _rms_proj(x, lp['norm_mix'], w_main, BF16, 1024, 512, w2=w_kpe, out2_dtype=F32)

    s5_tabs = _s5_tables(lp)
    w_glu = lp['s5_w_glu'].astype(BF16)
    zeros_s5 = jnp.zeros((BATCH, S5_GROUPS, S5_STATE), F32)
    a, p_s5_re, p_s5_im = _s5_branch(proj, s5_tabs, w_glu, zeros_s5, zeros_s5, BATCH, SEQ, 256, 0)
    a, s_s5_re, s_s5_im = _s5_branch(proj, s5_tabs, w_glu, h0_re, h0_im, DEC_BATCH, DEC_SEQ, DEC_SEQ, N_PROMPT,
                                     into=a)

    w_uq = lp['mla_w_uq'].reshape(MLA_Q_LORA, MLA_HEADS, MLA_QK)
    w_uq = jnp.pad(w_uq, ((0, 0), (0, 0), (0, MLA_HEAD_PAD - MLA_QK))).reshape(MLA_Q_LORA, MLA_QK_PAD).astype(BF16)
    w_ukv = lp['mla_w_ukv'].astype(BF16)
    q_gain = _pad_head_vec(lp['mla_q_gain'])
    k_gain = _pad_head_vec(lp['mla_k_gain'])
    row_tab = lambda i: _prompt_or_sample_block(i, tm)
    q = _mla_q(proj, lp['mla_q_norm'], w_uq, q_gain, cos_mla, sin_mla, row_tab, tm)
    c_kv, k_new, v_new = _mla_kv(proj, 0, COL_KVLAT // MLA_KV_LORA, N_ROWS, lp['mla_kv_norm'], w_ukv, k_gain,
                                 kpe, 0, cos_mla, sin_mla, row_tab, tm, True)
    n_past = DEC_BATCH * PAST_LEN
    k_past, v_past = _mla_kv(ckv_all, layer * n_past, 0, n_past, lp['mla_kv_norm'], w_ukv, k_gain,
                             kpe_past_all, layer * n_past, cos_past, sin_past,
                             lambda i: lax.rem(i, PAST_LEN // tm), tm, False)
    b = _attn_prompt(q, k_new, v_new, BATCH, SEQ, 512)
    b = _attn_sample(q, k_new, v_new, k_past, v_past, N_PROMPT, into=b)

    zeros_ret = jnp.zeros((BATCH, RET_HEADS, RET_DK, RET_DV), F32)
    c, p_ret = _retention_branch(proj, cos_ret, sin_ret, lambda t: t, zeros_ret, 0, BATCH, SEQ, 256, 0)
    c, s_ret = _retention_branch(proj, cos_ret, sin_ret, lambda t: SEQ // DEC_SEQ, ret_s0_all, layer * DEC_BATCH,
                                 DEC_BATCH, DEC_SEQ, DEC_SEQ, N_PROMPT, into=c)

    merged = _merge(a, b, c, lp['w_branch'].astype(BF16), proj, 512, 512)
    x = _res_matmul(merged, lp['w_out'].astype(BF16), x, 512, 512)

    mem_k_p, mem_v_p = _memkv(mem_prompt, lp['norm_mem'], lp['w_ck'].astype(BF16), lp['w_cv'].astype(BF16),
                              lp['cross_k_gain'], 256)
    qc = _rms_proj(x, lp['norm_cross'], lp['w_cq'].astype(BF16), BF16, 512, MEM_WIDTH)[0]
    o = _cross_attn(qc, mem_k_p.reshape(BATCH, N_MEM, MEM_WIDTH), mem_v_p.reshape(BATCH, N_MEM, MEM_WIDTH), 0,
                    lp['cross_q_gain'], BATCH, SEQ, 512, 0)
    o = _cross_attn(qc, mem_k_all, mem_v_all, layer * DEC_BATCH, lp['cross_q_gain'], DEC_BATCH, DEC_SEQ, DEC_SEQ,
                    N_PROMPT, into=o)
    x = _res_matmul(o, lp['w_co'].astype(BF16), x, 512, 512)

    x = _hier_moe(x, lp, w_gate_up, w_down, layer)

    kpe_rows = kpe[:, :MLA_ROPE]
    state_p = (c_kv[:N_PROMPT].reshape(BATCH, SEQ, MLA_KV_LORA), kpe_rows[:N_PROMPT].reshape(BATCH, SEQ, MLA_ROPE),
               p_s5_re, p_s5_im, p_ret,
               mem_k_p.reshape(BATCH, N_MEM, MEM_HEADS, MEM_HEAD_DIM),
               mem_v_p.reshape(BATCH, N_MEM, MEM_HEADS, MEM_HEAD_DIM))
    state_s = (c_kv[N_PROMPT:].reshape(DEC_BATCH, DEC_SEQ, MLA_KV_LORA),
               kpe_rows[N_PROMPT:].reshape(DEC_BATCH, DEC_SEQ, MLA_ROPE), s_s5_re, s_s5_im, s_ret)
    return x, state_p, state_s


_LAYER_PARAMS = (
    'norm_mix', 'w_in', 's5_a_re', 's5_a_im', 's5_log_dt', 's5_b_re', 's5_b_im', 's5_c_re', 's5_c_im', 's5_d',
    's5_w_glu', 'mla_q_norm', 'mla_w_uq', 'mla_kv_norm', 'mla_w_ukv', 'mla_q_gain', 'mla_k_gain', 'w_branch',
    'w_out', 'norm_cross', 'norm_mem', 'w_cq', 'w_ck', 'w_cv', 'cross_q_gain', 'cross_k_gain', 'w_co', 'norm_ffn',
    'w_group_router', 'b_group', 'w_expert_router', 'b_expert')


def kernel(x_prompt, x_sample, mem_prompt, cache_mla_ckv, cache_mla_kpe, state_s5_re, state_s5_im, state_ret,
           cache_mem_k, cache_mem_v, norm_mix, w_in, s5_a_re, s5_a_im, s5_log_dt, s5_b_re, s5_b_im, s5_c_re,
           s5_c_im, s5_d, s5_w_glu, mla_q_norm, mla_w_uq, mla_kv_norm, mla_w_ukv, mla_q_gain, mla_k_gain,
           w_branch, w_out, norm_cross, norm_mem, w_cq, w_ck, w_cv, cross_q_gain, cross_k_gain, w_co, norm_ffn,
           w_group_router, b_group, w_expert_router, b_expert, w_gate_up, w_down):
    params = dict(zip(_LAYER_PARAMS, (
        norm_mix, w_in, s5_a_re, s5_a_im, s5_log_dt, s5_b_re, s5_b_im, s5_c_re, s5_c_im, s5_d, s5_w_glu,
        mla_q_norm, mla_w_uq, mla_kv_norm, mla_w_ukv, mla_q_gain, mla_k_gain, w_branch, w_out, norm_cross,
        norm_mem, w_cq, w_ck, w_cv, cross_q_gain, cross_k_gain, w_co, norm_ffn, w_group_router, b_group,
        w_expert_router, b_expert)))
    assert x_prompt.shape == (BATCH, SEQ, D_MODEL) and x_sample.shape == (DEC_BATCH, DEC_SEQ, D_MODEL)
    assert cache_mla_ckv.shape == (DEPTH, DEC_BATCH, PAST_LEN, MLA_KV_LORA)

    pos_p = jnp.arange(SEQ, dtype=jnp.int32)
    pos_s = PAST_LEN + jnp.arange(DEC_SEQ, dtype=jnp.int32)
    tile_rows = 512
    tables = (_rope_tables(jnp.concatenate([pos_p, jnp.tile(pos_s, tile_rows // DEC_SEQ)]), MLA_ROPE)
              + _rope_tables(jnp.arange(PAST_LEN, dtype=jnp.int32), MLA_ROPE)
              + _rope_tables(jnp.concatenate([pos_p, pos_s]), RET_DK))

    x = jnp.concatenate([x_prompt.reshape(N_PROMPT, D_MODEL), x_sample.reshape(N_SAMPLE, D_MODEL)], axis=0)
    mem2d = mem_prompt.reshape(BATCH * N_MEM, D_MODEL)
    n_past = DEPTH * DEC_BATCH * PAST_LEN
    ckv_all = cache_mla_ckv.reshape(n_past, MLA_KV_LORA)
    kpe_past_all = jnp.pad(cache_mla_kpe.reshape(n_past, MLA_ROPE), ((0, 0), (0, LANES - MLA_ROPE)))
    ret_s0_all = state_ret.reshape(DEPTH * DEC_BATCH, RET_HEADS, RET_DK, RET_DV)
    mem_k_all = cache_mem_k.reshape(DEPTH * DEC_BATCH, N_MEM, MEM_HEADS, MEM_HEAD_DIM)
    mem_v_all = cache_mem_v.reshape(DEPTH * DEC_BATCH, N_MEM, MEM_HEADS, MEM_HEAD_DIM)
    outs_p, outs_s = [], []
    for l in range(DEPTH):
        lp = {name: value[l] for name, value in params.items()}
        caches = (ckv_all, kpe_past_all, state_s5_re[l], state_s5_im[l], ret_s0_all, mem_k_all, mem_v_all,
                  w_gate_up, w_down)
        x, st_p, st_s = _layer(x, mem2d, lp, l, caches, tables)
        outs_p.append(st_p)
        outs_s.append(st_s)
    stack = lambda outs, i: jnp.stack([o[i] for o in outs])
    return ((x[:N_PROMPT].reshape(BATCH, SEQ, D_MODEL), x[N_PROMPT:].reshape(DEC_BATCH, DEC_SEQ, D_MODEL))
            + tuple(stack(outs_p, i) for i in range(7)) + tuple(stack(outs_s, i) for i in range(5)))
```

```python
import functools
import math

import jax
import jax.numpy as jnp
from jax import lax
from jax.experimental import pallas as pl
from jax.experimental.pallas import tpu as pltpu

F32 = jnp.float32
BF16 = jnp.bfloat16

D_MODEL = 2048
BATCH = 4
SEQ = 2048
DEPTH = 2
DEC_BATCH = 32
DEC_SEQ = 32
PAST_LEN = 1024
CHUNK = 64
RMS_EPS = 1e-6
ROPE_THETA = 10000.0
NEG_INF = -1e30

S5_WIDTH = 1024
S5_GROUP_CH = 16
S5_GROUPS = 64
S5_STATE = 64
S5_COLS = S5_GROUPS * S5_STATE

MLA_HEADS = 8
MLA_NOPE = 128
MLA_ROPE = 64
MLA_QK = MLA_NOPE + MLA_ROPE
MLA_V = 128
MLA_Q_LORA = 512
MLA_KV_LORA = 512
MLA_HEAD_PAD = 256
MLA_QK_PAD = MLA_HEADS * MLA_HEAD_PAD

RET_HEADS = 4
RET_DK = 128
RET_DV = 256
BRANCH_WIDTH = 1024
N_BRANCH = 3

N_MEM = 256
MEM_HEADS = 4
MEM_HEAD_DIM = 128
MEM_WIDTH = MEM_HEADS * MEM_HEAD_DIM

MOE_GROUPS = 4
MOE_PER_GROUP = 8
N_EXPERTS = 32
MOE_TOP_K = 2
EXPERT_FF = 512

N_PROMPT = BATCH * SEQ
N_SAMPLE = DEC_BATCH * DEC_SEQ
N_ROWS = N_PROMPT + N_SAMPLE

COL_GMIX = 0
COL_S5 = 6144
COL_RV = 7168
COL_RG = 8192
COL_QLAT = 9216
COL_KVLAT = 9728
COL_RQ = 10240
COL_RK = 10752
PROJ_COLS = 11264
IN_S5 = (0, 1024)
IN_QLAT = (1024, 1536)
IN_KVLAT = (1536, 2048)
IN_KPE = (2048, 2112)
IN_RQ = (2112, 2624)
IN_RK = (2624, 3136)
IN_RV = (3136, 4160)
IN_RG = (4160, 5184)
IN_GMIX = (5184, 11328)

LANES = 128
EXPERT_ROWS = 256
N_PAIRS = N_ROWS * MOE_TOP_K
N_EXPERT_BLOCKS = -(-(N_PAIRS + N_EXPERTS * (EXPERT_ROWS - 1)) // EXPERT_ROWS)
VMEM_LIMIT = 56 * 1024 * 1024


def _cparams(sem):
    return pltpu.CompilerParams(dimension_semantics=sem, vmem_limit_bytes=VMEM_LIMIT)


def _rms_rows(x, g):
    r = lax.rsqrt(jnp.mean(x * x, axis=-1, keepdims=True) + RMS_EPS)
    return (x * r) * g


def _dot(a, b):
    return jnp.dot(a, b, preferred_element_type=F32)


def _rows_call(kernel_fn, into, *, in_specs, args, **kw):
    if into is None:
        out0 = kw['out_shape'][0] if isinstance(kw['out_shape'], (list, tuple)) else kw['out_shape']
        into = jnp.zeros(out0.shape, out0.dtype)

    def aliased_kernel(into_ref, *refs):
        del into_ref
        kernel_fn(*refs)

    return pl.pallas_call(aliased_kernel, in_specs=[pl.BlockSpec(memory_space=pl.ANY)] + in_specs,
                          input_output_aliases={0: 0}, **kw)(into, *args)


def _dot_nt(a, b):
    return lax.dot_general(a, b, (((1,), (1,)), ((), ())), preferred_element_type=F32)


def _rms_proj_kernel(x_ref, g_ref, w_ref, *refs, side):
    if side:
        w2_ref, o_ref, o2_ref, xn_ref = refs
    else:
        o_ref, xn_ref = refs

    @pl.when(pl.program_id(1) == 0)
    def _():
        xn_ref[...] = _rms_rows(x_ref[...], g_ref[...]).astype(BF16)
        if side:
            o2_ref[...] = _dot(xn_ref[...], w2_ref[...]).astype(o2_ref.dtype)

    o_ref[...] = _dot(xn_ref[...], w_ref[...]).astype(o_ref.dtype)


def _rms_proj(x, g, w, out_dtype, tm, tn, w2=None, out2_dtype=None):
    n, k = x.shape
    cols = w.shape[1]
    side = w2 is not None
    in_specs = [
        pl.BlockSpec((tm, k), lambda i, j: (i, 0)),
        pl.BlockSpec((1, k), lambda i, j: (0, 0)),
        pl.BlockSpec((k, tn), lambda i, j: (0, j)),
    ]
    out_specs = [pl.BlockSpec((tm, tn), lambda i, j: (i, j))]
    out_shape = [jax.ShapeDtypeStruct((n, cols), out_dtype)]
    args = [x, g.reshape(1, k), w]
    if side:
        cols2 = w2.shape[1]
        in_specs.append(pl.BlockSpec((k, cols2), lambda i, j: (0, 0)))
        out_specs.append(pl.BlockSpec((tm, cols2), lambda i, j: (i, 0)))
        out_shape.append(jax.ShapeDtypeStruct((n, cols2), out2_dtype))
        args.append(w2)
    return pl.pallas_call(
        functools.partial(_rms_proj_kernel, side=side),
        grid=(n // tm, cols // tn),
        in_specs=in_specs,
        out_specs=out_specs,
        out_shape=out_shape,
        scratch_shapes=[pltpu.VMEM((tm, k), BF16)],
        compiler_params=_cparams(("parallel", "arbitrary")),
        name="rms_proj_side" if side else "rms_proj",
    )(*args)


def _res_matmul_kernel(a_ref, w_ref, r_ref, o_ref):
    o_ref[...] = r_ref[...] + _dot(a_ref[...], w_ref[...])


def _res_matmul(a, w, res, tm, tn):
    n, k = a.shape
    cols = w.shape[1]
    return pl.pallas_call(
        _res_matmul_kernel,
        grid=(n // tm, cols // tn),
        in_specs=[
            pl.BlockSpec((tm, k), lambda i, j: (i, 0)),
            pl.BlockSpec((k, tn), lambda i, j: (0, j)),
            pl.BlockSpec((tm, tn), lambda i, j: (i, j)),
        ],
        out_specs=pl.BlockSpec((tm, tn), lambda i, j: (i, j)),
        out_shape=jax.ShapeDtypeStruct((n, cols), F32),
        compiler_params=_cparams(("parallel", "arbitrary")),
        name="res_matmul",
    )(a, w, res)


S5_K_SLAB = 256
S5_N_SLAB = 1024
S5_SLABS = S5_WIDTH // S5_K_SLAB
S5_SCAN_COLS = 1024
S5_SCAN_ROWS = 8


def _gelu_tanh(x):
    return 0.5 * x * (1.0 + jnp.tanh(math.sqrt(2.0 / math.pi) * (x + 0.044715 * (x * x * x))))


def _sigmoid(x):
    return 1.0 / (1.0 + jnp.exp(-x))


def _s5_kernel(u_ref, bre_ref, bim_ref, cre_ref, cim_ref, pw_ref, d_ref, wglu_ref,
               h0r_ref, h0i_ref, o_ref, hr_out_ref, hi_out_ref, sre_ref, sim_ref, cr_ref, ci_ref, *, tt):
    @pl.when(pl.program_id(1) == 0)
    def _():
        cr_ref[...] = h0r_ref[0]
        ci_ref[...] = h0i_ref[0]

    u = u_ref[...]
    for n in range(S5_SLABS):
        un = u[:, S5_K_SLAB * n:S5_K_SLAB * (n + 1)]
        sre_ref[:, S5_N_SLAB * n:S5_N_SLAB * (n + 1)] = _dot(un, bre_ref[n])
        sim_ref[:, S5_N_SLAB * n:S5_N_SLAB * (n + 1)] = _dot(un, bim_ref[n])

    for c in range(S5_COLS // S5_SCAN_COLS):
        sl = slice(c * S5_SCAN_COLS, (c + 1) * S5_SCAN_COLS)

        def body(blk, carry, sl=sl):
            cr, ci = carry
            rows = pl.ds(pl.multiple_of(blk * S5_SCAN_ROWS, S5_SCAN_ROWS), S5_SCAN_ROWS)
            xr = sre_ref[rows, sl]
            xi = sim_ref[rows, sl]
            for j in range(3):
                kr = pw_ref[j, 0, :, sl]
                ki = pw_ref[j, 1, :, sl]
                sr = pltpu.roll(xr, 1 << j, 0)
                si = pltpu.roll(xi, 1 << j, 0)
                xr, xi = xr + (kr * sr - ki * si), xi + (kr * si + ki * sr)
            pr = pw_ref[3, 0, :, sl]
            pi = pw_ref[3, 1, :, sl]
            hr = xr + (pr * cr - pi * ci)
            hi = xi + (pr * ci + pi * cr)
            sre_ref[rows, sl] = hr
            sim_ref[rows, sl] = hi
            last = S5_SCAN_ROWS - 1
            return (jnp.broadcast_to(hr[last:last + 1], hr.shape), jnp.broadcast_to(hi[last:last + 1], hi.shape))

        init = (jnp.broadcast_to(cr_ref[:, sl], (S5_SCAN_ROWS, S5_SCAN_COLS)),
                jnp.broadcast_to(ci_ref[:, sl], (S5_SCAN_ROWS, S5_SCAN_COLS)))
        hr, hi = lax.fori_loop(0, tt // S5_SCAN_ROWS, body, init)
        cr_ref[:, sl] = hr[0:1]
        ci_ref[:, sl] = hi[0:1]

    ys = []
    for n in range(S5_SLABS):
        hr_n = sre_ref[:, S5_N_SLAB * n:S5_N_SLAB * (n + 1)].astype(BF16)
        hi_n = sim_ref[:, S5_N_SLAB * n:S5_N_SLAB * (n + 1)].astype(BF16)
        ys.append(_dot(hr_n, cre_ref[n]) + _dot(hi_n, cim_ref[n]))
    y = jnp.concatenate(ys, axis=1) + d_ref[...] * u.astype(F32)
    z = _gelu_tanh(y)
    gate = _sigmoid(_dot(z.astype(BF16), wglu_ref[...]))
    o_ref[...] = (z * gate).astype(o_ref.dtype)
    hr_out_ref[0] = cr_ref[...]
    hi_out_ref[0] = ci_ref[...]


def _s5_tables(lp):
    a_re, a_im = lp['s5_a_re'], lp['s5_a_im']
    dt = jnp.exp(lp['s5_log_dt'])[:, None]
    mag = jnp.exp(a_re * dt)
    lb_re = mag * jnp.cos(a_im * dt)
    lb_im = mag * jnp.sin(a_im * dt)
    den = a_re * a_re + a_im * a_im
    n_re = lb_re - 1.0
    co_re = (n_re * a_re + lb_im * a_im) / den
    co_im = (lb_im * a_re - n_re * a_im) / den
    bb_re = co_re[..., None] * lp['s5_b_re'] - co_im[..., None] * lp['s5_b_im']
    bb_im = co_re[..., None] * lp['s5_b_im'] + co_im[..., None] * lp['s5_b_re']
    per_slab = S5_GROUPS // S5_SLABS
    eye = jnp.eye(per_slab, dtype=F32)

    def b_tiles(bb):
        bb = bb.reshape(S5_SLABS, per_slab, S5_STATE, S5_GROUP_CH)
        return (bb.transpose(0, 1, 3, 2)[:, :, :, None, :] * eye[None, :, None, :, None]).reshape(
            S5_SLABS, S5_K_SLAB, S5_N_SLAB).astype(BF16)

    def c_tiles(cc):
        cc = cc.reshape(S5_SLABS, per_slab, S5_GROUP_CH, S5_STATE)
        return (cc.transpose(0, 1, 3, 2)[:, :, :, None, :] * eye[None, :, None, :, None]).reshape(
            S5_SLABS, S5_N_SLAB, S5_K_SLAB).astype(BF16)

    def lam_pow(k):
        m = jnp.exp(k * (a_re * dt).reshape(1, S5_COLS))
        ang = k * (a_im * dt).reshape(1, S5_COLS)
        return jnp.stack([m * jnp.cos(ang), m * jnp.sin(ang)])

    t = jnp.arange(S5_SCAN_ROWS, dtype=F32)[:, None]
    steps = [jnp.where(t >= float(1 << j), lam_pow(jnp.full_like(t, float(1 << j))), 0.0) for j in range(3)]
    pw = jnp.stack(steps + [lam_pow(t + 1.0)])

    return dict(
        bre=b_tiles(bb_re), bim=b_tiles(bb_im),
        cre=c_tiles(lp['s5_c_re']), cim=c_tiles(-lp['s5_c_im']),
        pw=pw, d=lp['s5_d'].reshape(1, S5_WIDTH),
    )


def _s5_branch(proj, tabs, wglu, h0_re, h0_im, n_seq, seq_len, tt, row0, into=None):
    nt = seq_len // tt
    base = row0 // tt
    col = COL_S5 // S5_WIDTH
    const3 = lambda s, t: (0, 0, 0)
    const2 = lambda s, t: (0, 0)
    out, hr, hi = _rows_call(
        functools.partial(_s5_kernel, tt=tt), into,
        grid=(n_seq, nt),
        in_specs=[
            pl.BlockSpec((tt, S5_WIDTH), lambda s, t: (base + s * nt + t, col)),
            pl.BlockSpec((S5_SLABS, S5_K_SLAB, S5_N_SLAB), const3),
            pl.BlockSpec((S5_SLABS, S5_K_SLAB, S5_N_SLAB), const3),
            pl.BlockSpec((S5_SLABS, S5_N_SLAB, S5_K_SLAB), const3),
            pl.BlockSpec((S5_SLABS, S5_N_SLAB, S5_K_SLAB), const3),
            pl.BlockSpec((4, 2, S5_SCAN_ROWS, S5_COLS), lambda s, t: (0, 0, 0, 0)),
            pl.BlockSpec((1, S5_WIDTH), const2),
            pl.BlockSpec((S5_WIDTH, S5_WIDTH), const2),
            pl.BlockSpec((1, 1, S5_COLS), lambda s, t: (s, 0, 0)),
            pl.BlockSpec((1, 1, S5_COLS), lambda s, t: (s, 0, 0)),
        ],
        out_specs=[
            pl.BlockSpec((tt, S5_WIDTH), lambda s, t: (base + s * nt + t, 0)),
            pl.BlockSpec((1, 1, S5_COLS), lambda s, t: (s, 0, 0)),
            pl.BlockSpec((1, 1, S5_COLS), lambda s, t: (s, 0, 0)),
        ],
        out_shape=[
            jax.ShapeDtypeStruct((N_ROWS, S5_WIDTH), BF16),
            jax.ShapeDtypeStruct((n_seq, 1, S5_COLS), F32),
            jax.ShapeDtypeStruct((n_seq, 1, S5_COLS), F32),
        ],
        scratch_shapes=[
            pltpu.VMEM((tt, S5_COLS), F32), pltpu.VMEM((tt, S5_COLS), F32),
            pltpu.VMEM((1, S5_COLS), F32), pltpu.VMEM((1, S5_COLS), F32),
        ],
        compiler_params=_cparams(("parallel", "arbitrary")),
        name="s5_branch",
        args=(proj, tabs['bre'], tabs['bim'], tabs['cre'], tabs['cim'], tabs['pw'], tabs['d'], wglu,
              h0_re.reshape(n_seq, 1, S5_COLS), h0_im.reshape(n_seq, 1, S5_COLS)),
    )
    return (out, hr.reshape(n_seq, S5_GROUPS, S5_STATE), hi.reshape(n_seq, S5_GROUPS, S5_STATE))


def _rope_tables(pos, d):
    inv_freq = ROPE_THETA ** (-jnp.arange(0, d, 2, dtype=F32) / d)
    ang = pos.astype(F32)[:, None] * inv_freq[None, :]
    cos, sin = jnp.cos(ang), jnp.sin(ang)
    pad = jnp.zeros((pos.shape[0], LANES - d), F32)
    return (jnp.concatenate([cos, cos, pad], axis=1), jnp.concatenate([-sin, sin, pad], axis=1))


def _pad_head_vec(g):
    return jnp.concatenate([g, jnp.zeros((MLA_HEAD_PAD - MLA_QK,), F32)]).reshape(1, MLA_HEAD_PAD)


def _qk_head(nope, pe, gain, cos2, sin2, scale):
    ss = jnp.sum(nope * nope, axis=-1, keepdims=True) + jnp.sum(pe * pe, axis=-1, keepdims=True)
    r = lax.rsqrt(ss * (1.0 / MLA_QK) + RMS_EPS)
    nope = (nope * r) * gain[:, :MLA_NOPE]
    pe = (pe * r) * gain[:, MLA_NOPE:]
    lane = lax.broadcasted_iota(jnp.int32, pe.shape, 1)
    half = MLA_ROPE // 2
    swap = jnp.where(lane < half, pltpu.roll(pe, LANES - half, 1), pltpu.roll(pe, half, 1))
    pe = pe * cos2 + swap * sin2
    return nope * scale, pe * scale


def _mla_q_kernel(lat_ref, g_ref, w_ref, gain_ref, cos_ref, sin_ref, o_ref):
    xn = _rms_rows(lat_ref[...].astype(F32), g_ref[...]).astype(BF16)
    q = _dot(xn, w_ref[...])
    gain = gain_ref[...]
    cos2 = cos_ref[...]
    sin2 = sin_ref[...]
    for h in range(MLA_HEADS):
        c0 = h * MLA_HEAD_PAD
        nope, pe = _qk_head(q[:, c0:c0 + MLA_NOPE], q[:, c0 + MLA_NOPE:c0 + MLA_HEAD_PAD],
                            gain, cos2, sin2, MLA_QK ** -0.5)
        o_ref[:, c0:c0 + MLA_NOPE] = nope.astype(BF16)
        o_ref[:, c0 + MLA_NOPE:c0 + MLA_HEAD_PAD] = pe.astype(BF16)


def _mla_q(proj, g, w_uq_pad, gain_pad, cos2, sin2, tab_block, tm):
    n = proj.shape[0]
    col = COL_QLAT // MLA_Q_LORA
    c2 = lambda i: (0, 0)
    tab = lambda i: (tab_block(i), 0)
    return pl.pallas_call(
        _mla_q_kernel,
        grid=(n // tm,),
        in_specs=[
            pl.BlockSpec((tm, MLA_Q_LORA), lambda i: (i, col)),
            pl.BlockSpec((1, MLA_Q_LORA), c2),
            pl.BlockSpec((MLA_Q_LORA, MLA_QK_PAD), c2),
            pl.BlockSpec((1, MLA_HEAD_PAD), c2),
            pl.BlockSpec((tm, LANES), tab),
            pl.BlockSpec((tm, LANES), tab),
        ],
        out_specs=pl.BlockSpec((tm, MLA_QK_PAD), lambda i: (i, 0)),
        out_shape=jax.ShapeDtypeStruct((n, MLA_QK_PAD), BF16),
        compiler_params=_cparams(("parallel",)),
        name="mla_q",
    )(proj, g.reshape(1, -1), w_uq_pad, gain_pad, cos2, sin2)


def _mla_kv_kernel(lat_ref, g_ref, w_ref, gain_ref, kpe_ref, cos_ref, sin_ref, *out_refs, normalize):
    lat = lat_ref[...].astype(F32)
    if normalize:
        ckv_ref, k_ref, v_ref = out_refs
        lat = _rms_rows(lat, g_ref[...])
        ckv_ref[...] = lat
    else:
        k_ref, v_ref = out_refs
    kv = _dot(lat.astype(BF16), w_ref[...])
    gain = gain_ref[...]
    kpe = kpe_ref[...]
    cos2 = cos_ref[...]
    sin2 = sin_ref[...]
    for h in range(MLA_HEADS):
        c0 = h * (MLA_NOPE + MLA_V)
        nope, pe = _qk_head(kv[:, c0:c0 + MLA_NOPE], kpe, gain, cos2, sin2, 1.0)
        k0 = h * MLA_HEAD_PAD
        k_ref[:, k0:k0 + MLA_NOPE] = nope.astype(BF16)
        k_ref[:, k0 + MLA_NOPE:k0 + MLA_HEAD_PAD] = pe.astype(BF16)
        v_ref[:, h * MLA_V:(h + 1) * MLA_V] = kv[:, c0 + MLA_NOPE:c0 + MLA_NOPE + MLA_V].astype(BF16)


def _mla_kv(lat, lat_row0, lat_col, n, g, w_ukv, gain_pad, kpe_pad, kpe_row0, cos2, sin2, tab_block, tm, normalize):
    c2 = lambda i: (0, 0)
    row = lambda i: (i, 0)
    tab = lambda i: (tab_block(i), 0)
    out_specs = [pl.BlockSpec((tm, MLA_QK_PAD), row), pl.BlockSpec((tm, MLA_HEADS * MLA_V), row)]
    out_shape = [jax.ShapeDtypeStruct((n, MLA_QK_PAD), BF16), jax.ShapeDtypeStruct((n, MLA_HEADS * MLA_V), BF16)]
    if normalize:
        out_specs = [pl.BlockSpec((tm, MLA_KV_LORA), row)] + out_specs
        out_shape = [jax.ShapeDtypeStruct((n, MLA_KV_LORA), F32)] + out_shape
    return pl.pallas_call(
        functools.partial(_mla_kv_kernel, normalize=normalize),
        grid=(n // tm,),
        in_specs=[
            pl.BlockSpec((tm, MLA_KV_LORA), lambda i: (lat_row0 // tm + i, lat_col)),
            pl.BlockSpec((1, MLA_KV_LORA), c2),
            pl.BlockSpec((MLA_KV_LORA, MLA_HEADS * (MLA_NOPE + MLA_V)), c2),
            pl.BlockSpec((1, MLA_HEAD_PAD), c2),
            pl.BlockSpec((tm, LANES), lambda i: (kpe_row0 // tm + i, 0)),
            pl.BlockSpec((tm, LANES), tab),
            pl.BlockSpec((tm, LANES), tab),
        ],
        out_specs=out_specs,
        out_shape=out_shape,
        compiler_params=_cparams(("parallel",)),
        name="mla_kv_norm" if normalize else "mla_kv_past",
    )(lat, g.reshape(1, -1), w_ukv, gain_pad, kpe_pad, cos2, sin2)


ATTN_HEADS_PER_STEP = 2


def _attn_prompt_kernel(q_ref, k_ref, v_ref, o_ref, *, tq):
    qi = pl.program_id(2)

    def key_tile(ki, state, diagonal):
        rows = pl.ds(pl.multiple_of(ki * tq, tq), tq)
        new_state = []
        for h in range(ATTN_HEADS_PER_STEP):
            qk = slice(h * MLA_HEAD_PAD, (h + 1) * MLA_HEAD_PAD)
            vs = slice(h * MLA_V, (h + 1) * MLA_V)
            s = _dot_nt(q_ref[:, qk], k_ref[rows, qk])
            if diagonal:
                row_chunk = lax.broadcasted_iota(jnp.int32, s.shape, 0) // CHUNK
                col_chunk = lax.broadcasted_iota(jnp.int32, s.shape, 1) // CHUNK
                s = jnp.where(col_chunk <= row_chunk, s, NEG_INF)
            m_old, l_old, acc_old = state[h]
            m_new = jnp.maximum(m_old, jnp.max(s, axis=-1, keepdims=True))
            alpha = jnp.exp(m_old - m_new)
            p = jnp.exp(s - m_new)
            l_new = alpha * l_old + jnp.sum(p, axis=-1, keepdims=True)
            acc_new = alpha * acc_old + _dot(p.astype(BF16), v_ref[rows, vs])
            new_state.append((m_new, l_new, acc_new))
        return tuple(new_state)

    init = tuple((jnp.full((tq, 1), NEG_INF, F32), jnp.zeros((tq, 1), F32), jnp.zeros((tq, MLA_V), F32))
                 for _ in range(ATTN_HEADS_PER_STEP))
    state = lax.fori_loop(0, qi, lambda ki, st: key_tile(ki, st, False), init)
    state = key_tile(qi, state, True)
    for h in range(ATTN_HEADS_PER_STEP):
        _, l_fin, acc_fin = state[h]
        o_ref[:, h * MLA_V:(h + 1) * MLA_V] = (acc_fin / l_fin).astype(o_ref.dtype)


def _attn_prompt(q, k, v, n_batch, seq_len, tq):
    assert tq % CHUNK == 0
    nq = seq_len // tq
    hp = ATTN_HEADS_PER_STEP
    return _rows_call(
        functools.partial(_attn_prompt_kernel, tq=tq), None,
        grid=(n_batch, MLA_HEADS // hp, nq),
        in_specs=[
            pl.BlockSpec((tq, hp * MLA_HEAD_PAD), lambda b, h, qi: (b * nq + qi, h)),
            pl.BlockSpec((seq_len, hp * MLA_HEAD_PAD), lambda b, h, qi: (b, h)),
            pl.BlockSpec((seq_len, hp * MLA_V), lambda b, h, qi: (b, h)),
        ],
        out_specs=pl.BlockSpec((tq, hp * MLA_V), lambda b, h, qi: (b * nq + qi, h)),
        out_shape=jax.ShapeDtypeStruct((N_ROWS, MLA_HEADS * MLA_V), BF16),
        compiler_params=_cparams(("parallel", "parallel", "arbitrary")),
        name="attn_prompt",
        args=(q, k, v),
    )


def _attn_sample_kernel(q_ref, kp_ref, kn_ref, vp_ref, vn_ref, o_ref):
    for h in range(MLA_HEADS):
        ks = slice(h * MLA_HEAD_PAD, (h + 1) * MLA_HEAD_PAD)
        vs = slice(h * MLA_V, (h + 1) * MLA_V)
        q = q_ref[:, ks]
        s_p = _dot_nt(q, kp_ref[:, ks])
        s_n = _dot_nt(q, kn_ref[:, ks])
        m = jnp.maximum(jnp.max(s_p, axis=-1, keepdims=True), jnp.max(s_n, axis=-1, keepdims=True))
        p_p = jnp.exp(s_p - m)
        p_n = jnp.exp(s_n - m)
        l = jnp.sum(p_p, axis=-1, keepdims=True) + jnp.sum(p_n, axis=-1, keepdims=True)
        o = _dot(p_p.astype(BF16), vp_ref[:, vs]) + _dot(p_n.astype(BF16), vn_ref[:, vs])
        o_ref[:, vs] = (o / l).astype(o_ref.dtype)


def _attn_sample(q, k_new, v_new, k_past, v_past, row0, into):
    assert (PAST_LEN + DEC_SEQ - 1) // CHUNK <= PAST_LEN // CHUNK
    base = row0 // DEC_SEQ
    new = lambda b: (base + b, 0)
    past = lambda b: (b, 0)
    return _rows_call(
        _attn_sample_kernel, into,
        grid=(DEC_BATCH,),
        in_specs=[
            pl.BlockSpec((DEC_SEQ, MLA_QK_PAD), new),
            pl.BlockSpec((PAST_LEN, MLA_QK_PAD), past),
            pl.BlockSpec((DEC_SEQ, MLA_QK_PAD), new),
            pl.BlockSpec((PAST_LEN, MLA_HEADS * MLA_V), past),
            pl.BlockSpec((DEC_SEQ, MLA_HEADS * MLA_V), new),
        ],
        out_specs=pl.BlockSpec((DEC_SEQ, MLA_HEADS * MLA_V), new),
        out_shape=jax.ShapeDtypeStruct((N_ROWS, MLA_HEADS * MLA_V), BF16),
        compiler_params=_cparams(("parallel",)),
        name="attn_sample",
        args=(q, k_past, k_new, v_past, v_new),
    )


def _retention_tables(block):
    log_g = jnp.log1p(-jnp.exp2(-5.0 - jnp.arange(RET_HEADS, dtype=F32)))
    idx = jnp.arange(block, dtype=F32)
    diff = idx[:, None] - idx[None, :]
    decay = jnp.where(diff >= 0, jnp.exp(log_g[:, None, None] * jnp.maximum(diff, 0.0)), 0.0)
    q_decay = jnp.exp(log_g[:, None] * (idx + 1.0))
    k_decay = jnp.exp(log_g[:, None] * (block - 1.0 - idx))
    blk_decay = jnp.exp(log_g * block)
    return (decay,
            jnp.broadcast_to(q_decay[:, :, None], (RET_HEADS, block, RET_DK)),
            jnp.broadcast_to(k_decay[:, :, None], (RET_HEADS, block, RET_DK)),
            jnp.broadcast_to(blk_decay[:, None, None], (RET_HEADS, 1, RET_DV)))


def _rope128(x, cos, sin):
    return x * cos + pltpu.roll(x, RET_DK // 2, 1) * sin


def _retention_kernel(rq_ref, rk_ref, rv_ref, rg_ref, cos_ref, sin_ref, dec_ref, qd_ref, kd_ref, bd_ref, s0_ref,
                      o_ref, s_out_ref, s_ref):
    @pl.when(pl.program_id(1) == 0)
    def _():
        s_ref[...] = s0_ref[0]

    cos = cos_ref[...]
    sin = sin_ref[...]
    for h in range(RET_HEADS):
        ks = slice(h * RET_DK, (h + 1) * RET_DK)
        vs = slice(h * RET_DV, (h + 1) * RET_DV)
        q = _rope128(rq_ref[:, ks].astype(F32), cos, sin) * (RET_DK ** -0.5)
        k = _rope128(rk_ref[:, ks].astype(F32), cos, sin)
        v = rv_ref[:, vs]
        s_old = s_ref[h]
        att = _dot_nt(q.astype(BF16), k.astype(BF16)) * dec_ref[h]
        o = _dot(att.astype(BF16), v) + _dot((q * qd_ref[h]).astype(BF16), s_old.astype(BF16))
        kd = (k * kd_ref[h]).astype(BF16)
        s_new = bd_ref[h] * s_old + lax.dot_general(kd, v, (((0,), (0,)), ((), ())), preferred_element_type=F32)
        s_ref[h] = s_new
        s_out_ref[0, h] = s_new
        mu = jnp.mean(o, axis=-1, keepdims=True)
        c = o - mu
        o = c * lax.rsqrt(jnp.mean(c * c, axis=-1, keepdims=True) + RMS_EPS)
        g = rg_ref[:, vs].astype(F32)
        o_ref[:, vs] = (o * (g * _sigmoid(g))).astype(o_ref.dtype)


def _retention_branch(proj, cos, sin, tab_block, s0, s0_base, n_seq, seq_len, block, row0, into=None):
    nblk = seq_len // block
    base = row0 // block
    dec, qd, kd, bd = _retention_tables(block)
    rows = lambda b, t: base + b * nblk + t
    tab = lambda b, t: (tab_block(t), 0)
    all3 = lambda b, t: (0, 0, 0)
    qk_w = RET_HEADS * RET_DK
    v_w = RET_HEADS * RET_DV
    out, s_last = _rows_call(
        _retention_kernel, into,
        grid=(n_seq, nblk),
        in_specs=[
            pl.BlockSpec((block, qk_w), lambda b, t: (rows(b, t), COL_RQ // qk_w)),
            pl.BlockSpec((block, qk_w), lambda b, t: (rows(b, t), COL_RK // qk_w)),
            pl.BlockSpec((block, v_w), lambda b, t: (rows(b, t), COL_RV // v_w)),
            pl.BlockSpec((block, v_w), lambda b, t: (rows(b, t), COL_RG // v_w)),
            pl.BlockSpec((block, LANES), tab),
            pl.BlockSpec((block, LANES), tab),
            pl.BlockSpec((RET_HEADS, block, block), all3),
            pl.BlockSpec((RET_HEADS, block, RET_DK), all3),
            pl.BlockSpec((RET_HEADS, block, RET_DK), all3),
            pl.BlockSpec((RET_HEADS, 1, RET_DV), all3),
            pl.BlockSpec((1, RET_HEADS, RET_DK, RET_DV), lambda b, t: (s0_base + b, 0, 0, 0)),
        ],
        out_specs=[
            pl.BlockSpec((block, v_w), lambda b, t: (rows(b, t), 0)),
            pl.BlockSpec((1, RET_HEADS, RET_DK, RET_DV), lambda b, t: (b, 0, 0, 0)),
        ],
        out_shape=[
            jax.ShapeDtypeStruct((N_ROWS, v_w), BF16),
            jax.ShapeDtypeStruct((n_seq, RET_HEADS, RET_DK, RET_DV), F32),
        ],
        scratch_shapes=[pltpu.VMEM((RET_HEADS, RET_DK, RET_DV), F32)],
        compiler_params=_cparams(("parallel", "arbitrary")),
        name="retention",
        args=(proj, proj, proj, proj, cos, sin, dec, qd, kd, bd, s0),
    )
    return out, s_last


def _merge_kernel(a_ref, b_ref, c_ref, w_ref, g0_ref, g1_ref, g2_ref, o_ref):
    acc = _sigmoid(g0_ref[...].astype(F32)) * _dot(a_ref[...], w_ref[0])
    acc += _sigmoid(g1_ref[...].astype(F32)) * _dot(b_ref[...], w_ref[1])
    acc += _sigmoid(g2_ref[...].astype(F32)) * _dot(c_ref[...], w_ref[2])
    o_ref[...] = acc.astype(o_ref.dtype)


def _merge(o_a, o_b, o_c, w_branch, proj, tm, tn):
    n = o_a.shape[0]
    nj = D_MODEL // tn
    br = pl.BlockSpec((tm, BRANCH_WIDTH), lambda i, j: (i, 0))
    gate = lambda b: pl.BlockSpec((tm, tn), lambda i, j: (i, COL_GMIX // tn + b * nj + j))
    return pl.pallas_call(
        _merge_kernel,
        grid=(n // tm, nj),
        in_specs=[br, br, br,
                  pl.BlockSpec((N_BRANCH, BRANCH_WIDTH, tn), lambda i, j: (0, 0, j)),
                  gate(0), gate(1), gate(2)],
        out_specs=pl.BlockSpec((tm, tn), lambda i, j: (i, j)),
        out_shape=jax.ShapeDtypeStruct((n, D_MODEL), BF16),
        compiler_params=_cparams(("parallel", "arbitrary")),
        name="branch_merge",
    )(o_a, o_b, o_c, w_branch, proj, proj, proj)


def _memkv_kernel(m_ref, g_ref, wk_ref, wv_ref, kg_ref, k_ref, v_ref):
    mn = _rms_rows(m_ref[...], g_ref[...]).astype(BF16)
    kk = _dot(mn, wk_ref[...])
    for h in range(MEM_HEADS):
        sl = slice(h * MEM_HEAD_DIM, (h + 1) * MEM_HEAD_DIM)
        k_ref[:, sl] = _rms_rows(kk[:, sl], kg_ref[...])
    v_ref[...] = _dot(mn, wv_ref[...])


def _memkv(mem, g, wk, wv, k_gain, tm):
    n, k = mem.shape
    c2 = lambda i: (0, 0)
    return pl.pallas_call(
        _memkv_kernel,
        grid=(n // tm,),
        in_specs=[
            pl.BlockSpec((tm, k), lambda i: (i, 0)),
            pl.BlockSpec((1, k), c2),
            pl.BlockSpec((k, MEM_WIDTH), c2),
            pl.BlockSpec((k, MEM_WIDTH), c2),
            pl.BlockSpec((1, MEM_HEAD_DIM), c2),
        ],
        out_specs=[pl.BlockSpec((tm, MEM_WIDTH), lambda i: (i, 0)), pl.BlockSpec((tm, MEM_WIDTH), lambda i: (i, 0))],
        out_shape=[jax.ShapeDtypeStruct((n, MEM_WIDTH), F32), jax.ShapeDtypeStruct((n, MEM_WIDTH), F32)],
        compiler_params=_cparams(("parallel",)),
        name="memory_kv",
    )(mem, g.reshape(1, k), wk, wv, k_gain.reshape(1, MEM_HEAD_DIM))


def _cross_attn_kernel(q_ref, k_ref, v_ref, qg_ref, o_ref, *, head_axis):
    for h in range(MEM_HEADS):
        sl = slice(h * MEM_HEAD_DIM, (h + 1) * MEM_HEAD_DIM)
        if head_axis:
            k, v = k_ref[0, :, h, :], v_ref[0, :, h, :]
        else:
            k, v = k_ref[0, :, sl], v_ref[0, :, sl]
        q = _rms_rows(q_ref[:, sl].astype(F32), qg_ref[...]) * (MEM_HEAD_DIM ** -0.5)
        s = _dot_nt(q.astype(BF16), k.astype(BF16))
        p = jnp.exp(s - jnp.max(s, axis=-1, keepdims=True))
        l = jnp.sum(p, axis=-1, keepdims=True)
        o = _dot(p.astype(BF16), v.astype(BF16))
        o_ref[:, sl] = (o / l).astype(o_ref.dtype)


def _cross_attn(qc, mem_k, mem_v, mem0, q_gain, n_batch, rows_per_batch, tm, row0, into=None):
    tiles = rows_per_batch // tm
    base = row0 // tm
    head_axis = mem_k.ndim == 4
    if head_axis:
        mem_spec = pl.BlockSpec((1, N_MEM, MEM_HEADS, MEM_HEAD_DIM), lambda b, t: (mem0 + b, 0, 0, 0))
    else:
        mem_spec = pl.BlockSpec((1, N_MEM, MEM_WIDTH), lambda b, t: (mem0 + b, 0, 0))
    return _rows_call(
        functools.partial(_cross_attn_kernel, head_axis=head_axis), into,
        grid=(n_batch, tiles),
        in_specs=[
            pl.BlockSpec((tm, MEM_WIDTH), lambda b, t: (base + b * tiles + t, 0)),
            mem_spec, mem_spec,
            pl.BlockSpec((1, MEM_HEAD_DIM), lambda b, t: (0, 0)),
        ],
        out_specs=pl.BlockSpec((tm, MEM_WIDTH), lambda b, t: (base + b * tiles + t, 0)),
        out_shape=jax.ShapeDtypeStruct((N_ROWS, MEM_WIDTH), BF16),
        compiler_params=_cparams(("parallel", "arbitrary")),
        name="cross_attn",
        args=(qc, mem_k, mem_v, q_gain.reshape(1, MEM_HEAD_DIM)),
    )


ROUTER_LANES = LANES
FAR_LANE = 4 * LANES


def _split_bf16(x):
    hi = x.astype(BF16)
    lo = (x - hi.astype(F32)).astype(BF16)
    return hi, lo


def _router_kernel(x_ref, g_ref, w_ref, b_ref, xn_ref, ids_ref, gates_ref):
    xn = _rms_rows(x_ref[...], g_ref[...])
    xn_ref[...] = xn
    xh, xl = _split_bf16(xn)
    wh, wl = _split_bf16(w_ref[...])
    logits = _dot(xh, wh) + (_dot(xh, wl) + _dot(xl, wh)) + b_ref[...]
    lane = lax.broadcasted_iota(jnp.int32, logits.shape, 1)

    def first_max(vals):
        top = jnp.max(vals, axis=-1, keepdims=True)
        idx = jnp.min(jnp.where(vals == top, lane, FAR_LANE), axis=-1, keepdims=True)
        return top, idx

    g_mask = jnp.logical_and(lane >= N_EXPERTS, lane < N_EXPERTS + MOE_GROUPS)
    g_top, g_lane = first_max(jnp.where(g_mask, logits, NEG_INF))
    p_group = 1.0 / jnp.sum(jnp.where(g_mask, jnp.exp(logits - g_top), 0.0), axis=-1, keepdims=True)
    g_sel = g_lane - N_EXPERTS
    e_mask = jnp.logical_and(lane < N_EXPERTS, lane // MOE_PER_GROUP == g_sel)
    e_vals = jnp.where(e_mask, logits, NEG_INF)
    e1_top, e1 = first_max(e_vals)
    e2_top, e2 = first_max(jnp.where(lane == e1, NEG_INF, e_vals))
    t = jnp.exp(e2_top - e1_top)
    gate1 = p_group / (1.0 + t)
    gate2 = p_group * t / (1.0 + t)
    ids_ref[...] = jnp.where(lane == 0, e1, jnp.where(lane == 1, e2, 0))
    gates_ref[...] = jnp.where(lane == 0, gate1, jnp.where(lane == 1, gate2, 0.0))


def _router(x, g, w_router, b_router, tm):
    n, k = x.shape
    c2 = lambda i: (0, 0)
    row = lambda i: (i, 0)
    return pl.pallas_call(
        _router_kernel,
        grid=(n // tm,),
        in_specs=[
            pl.BlockSpec((tm, k), row),
            pl.BlockSpec((1, k), c2),
            pl.BlockSpec((k, ROUTER_LANES), c2),
            pl.BlockSpec((1, ROUTER_LANES), c2),
        ],
        out_specs=[pl.BlockSpec((tm, k), row), pl.BlockSpec((tm, ROUTER_LANES), row),
                   pl.BlockSpec((tm, ROUTER_LANES), row)],
        out_shape=[jax.ShapeDtypeStruct((n, k), F32), jax.ShapeDtypeStruct((n, ROUTER_LANES), jnp.int32),
                   jax.ShapeDtypeStruct((n, ROUTER_LANES), F32)],
        compiler_params=_cparams(("parallel",)),
        name="moe_router",
    )(x, g.reshape(1, k), w_router, b_router)


def _route_plan(expert_idx):
    flat_e = expert_idx.reshape(N_PAIRS)
    onehot = (flat_e[:, None] == jnp.arange(N_EXPERTS, dtype=jnp.int32)[None, :]).astype(jnp.int32)
    csum = jnp.cumsum(onehot, axis=0)
    rank = jnp.take_along_axis(csum, flat_e[:, None], axis=1)[:, 0] - 1
    counts = csum[-1]
    padded = (counts + EXPERT_ROWS - 1) // EXPERT_ROWS * EXPERT_ROWS
    pad_ends = jnp.cumsum(padded)
    pad_starts = pad_ends - padded
    dest = (pad_starts[flat_e] + rank).astype(jnp.int32)
    n_used = (pad_ends[-1] // EXPERT_ROWS).astype(jnp.int32).reshape(1)
    block_start = jnp.arange(N_EXPERT_BLOCKS, dtype=jnp.int32) * EXPERT_ROWS
    block_e = jnp.sum((pad_ends[None, :] <= block_start[:, None]).astype(jnp.int32), axis=1)
    block_e = jnp.minimum(block_e, N_EXPERTS - 1)
    src = jnp.zeros((N_EXPERT_BLOCKS * EXPERT_ROWS,), jnp.int32).at[dest].set(
        jnp.arange(N_PAIRS, dtype=jnp.int32) // MOE_TOP_K)
    return dest, src, block_e, n_used


DMA_UNROLL = 8


def _experts_kernel(src_ref, be_ref, nu_ref, x_hbm, wgu_ref, wd_ref, o_ref, xbuf_ref, sem):
    i = pl.program_id(0)
    n_used = nu_ref[0]
    slot = lax.rem(i, 2)

    def row_copy(blk, buf, r):
        return pltpu.make_async_copy(x_hbm.at[pl.ds(src_ref[blk * EXPERT_ROWS + r], 1)],
                                     xbuf_ref.at[buf, pl.ds(r, 1)], sem.at[buf])

    def start_block(blk, buf):
        def body(c, carry):
            for u in range(DMA_UNROLL):
                row_copy(blk, buf, c * DMA_UNROLL + u).start(priority=u % 2)
            return carry
        lax.fori_loop(0, EXPERT_ROWS // DMA_UNROLL, body, 0)

    def wait_block(blk, buf):
        def body(c, carry):
            for u in range(DMA_UNROLL):
                row_copy(blk, buf, c * DMA_UNROLL + u).wait()
            return carry
        lax.fori_loop(0, EXPERT_ROWS // DMA_UNROLL, body, 0)

    @pl.when(jnp.logical_and(i == 0, n_used > 0))
    def _():
        start_block(0, 0)

    @pl.when(i + 1 < n_used)
    def _():
        start_block(i + 1, 1 - slot)

    @pl.when(i < n_used)
    def _():
        wait_block(i, slot)
        gu = _dot(xbuf_ref[slot].astype(BF16), wgu_ref[0, 0].astype(BF16))
        g = gu[:, :EXPERT_FF]
        h = (g * _sigmoid(g)) * gu[:, EXPERT_FF:]
        o_ref[...] = _dot(h.astype(BF16), wd_ref[0, 0].astype(BF16))

    @pl.when(i >= n_used)
    def _():
        o_ref[...] = jnp.zeros(o_ref.shape, o_ref.dtype)


def _experts(xn, src, block_e, n_used, w_gate_up, w_down, layer):
    return pl.pallas_call(
        _experts_kernel,
        grid_spec=pltpu.PrefetchScalarGridSpec(
            num_scalar_prefetch=3,
            grid=(N_EXPERT_BLOCKS,),
            in_specs=[
                pl.BlockSpec(memory_space=pl.ANY),
                pl.BlockSpec((1, 1, D_MODEL, 2 * EXPERT_FF), lambda i, src, be, nu: (layer, be[i], 0, 0)),
                pl.BlockSpec((1, 1, EXPERT_FF, D_MODEL), lambda i, src, be, nu: (layer, be[i], 0, 0)),
            ],
            out_specs=pl.BlockSpec((EXPERT_ROWS, D_MODEL), lambda i, src, be, nu: (i, 0)),
            scratch_shapes=[pltpu.VMEM((2, EXPERT_ROWS, D_MODEL), F32), pltpu.SemaphoreType.DMA((2,))],
        ),
        out_shape=jax.ShapeDtypeStruct((N_EXPERT_BLOCKS * EXPERT_ROWS, D_MODEL), F32),
        compiler_params=_cparams(("arbitrary",)),
        name="moe_experts",
    )(src, block_e, n_used, xn, w_gate_up, w_down)


def _combine_kernel(dest_ref, x_ref, gates_ref, y_hbm, o_ref, buf_ref, sem, *, rows):
    base = pl.program_id(0) * rows
    per_iter = DMA_UNROLL // MOE_TOP_K

    def row_copy(i, k):
        return pltpu.make_async_copy(y_hbm.at[pl.ds(dest_ref[(base + i) * MOE_TOP_K + k], 1)],
                                     buf_ref.at[k, pl.ds(i, 1)], sem)

    def start(c, carry):
        for u in range(per_iter):
            for k in range(MOE_TOP_K):
                row_copy(c * per_iter + u, k).start(priority=k)
        return carry

    def wait(c, carry):
        for u in range(per_iter):
            for k in range(MOE_TOP_K):
                row_copy(c * per_iter + u, k).wait()
        return carry

    lax.fori_loop(0, rows // per_iter, start, 0)
    lax.fori_loop(0, rows // per_iter, wait, 0)
    gates = gates_ref[...]
    y = gates[:, 0:1] * buf_ref[0] + gates[:, 1:2] * buf_ref[1]
    o_ref[...] = x_ref[...] + y


def _combine(x, gates, y_sorted, dest, rows):
    n, width = x.shape
    return pl.pallas_call(
        functools.partial(_combine_kernel, rows=rows),
        grid_spec=pltpu.PrefetchScalarGridSpec(
            num_scalar_prefetch=1,
            grid=(n // rows,),
            in_specs=[
                pl.BlockSpec((rows, width), lambda i, d: (i, 0)),
                pl.BlockSpec((rows, ROUTER_LANES), lambda i, d: (i, 0)),
                pl.BlockSpec(memory_space=pl.ANY),
            ],
            out_specs=pl.BlockSpec((rows, width), lambda i, d: (i, 0)),
            scratch_shapes=[pltpu.VMEM((MOE_TOP_K, rows, width), F32), pltpu.SemaphoreType.DMA(())],
        ),
        out_shape=jax.ShapeDtypeStruct((n, width), F32),
        compiler_params=_cparams(("arbitrary",)),
        name="moe_combine",
    )(dest, x, gates, y_sorted)


def _hier_moe(x, lp, w_gate_up, w_down, layer):
    w_router = jnp.concatenate(
        [lp['w_expert_router'], lp['w_group_router'],
         jnp.zeros((D_MODEL, ROUTER_LANES - N_EXPERTS - MOE_GROUPS), F32)], axis=1)
    b_router = jnp.concatenate(
        [lp['b_expert'], lp['b_group'], jnp.zeros((ROUTER_LANES - N_EXPERTS - MOE_GROUPS,), F32)]).reshape(1, -1)
    xn, ids, gates = _router(x, lp['norm_ffn'], w_router, b_router, 512)
    dest, src, block_e, n_used = _route_plan(ids[:, :MOE_TOP_K])
    y_sorted = _experts(xn, src, block_e, n_used, w_gate_up, w_down, layer)
    return _combine(x, gates, y_sorted, dest, 256)


def _cols(w, *ranges):
    return jnp.concatenate([w[:, a:b] for a, b in ranges], axis=1)


def _prompt_or_sample_block(i, tm):
    return jnp.where(i < N_PROMPT // tm, lax.rem(i, SEQ // tm), SEQ // tm)


def _layer(x, mem_prompt, lp, layer, caches, tables):
    (ckv_all, kpe_past_all, h0_re, h0_im, ret_s0_all, mem_k_all, mem_v_all, w_gate_up, w_down) = caches
    cos_mla, sin_mla, cos_past, sin_past, cos_ret, sin_ret = tables
    tm = 512
    w_in = lp['w_in']
    w_main = _cols(w_in, IN_GMIX, IN_S5, IN_RV, IN_RG, IN_QLAT, IN_KVLAT, IN_RQ, IN_RK).astype(BF16)
    w_kpe = jnp.pad(_cols(w_in, IN_KPE), ((0, 0), (0, LANES - MLA_ROPE))).astype(BF16)
    proj, kpe = _rms_proj(x, lp['norm_mix'], w_main, BF16, 1024, 1024, w2=w_kpe, out2_dtype=F32)

    s5_tabs = _s5_tables(lp)
    w_glu = lp['s5_w_glu'].astype(BF16)
    zeros_s5 = jnp.zeros((BATCH, S5_GROUPS, S5_STATE), F32)
    a, p_s5_re, p_s5_im = _s5_branch(proj, s5_tabs, w_glu, zeros_s5, zeros_s5, BATCH, SEQ, 256, 0)
    a, s_s5_re, s_s5_im = _s5_branch(proj, s5_tabs, w_glu, h0_re, h0_im, DEC_BATCH, DEC_SEQ, DEC_SEQ, N_PROMPT,
                                     into=a)

    w_uq = lp['mla_w_uq'].reshape(MLA_Q_LORA, MLA_HEADS, MLA_QK)
    w_uq = jnp.pad(w_uq, ((0, 0), (0, 0), (0, MLA_HEAD_PAD - MLA_QK))).reshape(MLA_Q_LORA, MLA_QK_PAD).astype(BF16)
    w_ukv = lp['mla_w_ukv'].astype(BF16)
    q_gain = _pad_head_vec(lp['mla_q_gain'])
    k_gain = _pad_head_vec(lp['mla_k_gain'])
    row_tab = lambda i: _prompt_or_sample_block(i, tm)
    q = _mla_q(proj, lp['mla_q_norm'], w_uq, q_gain, cos_mla, sin_mla, row_tab, tm)
    c_kv, k_new, v_new = _mla_kv(proj, 0, COL_KVLAT // MLA_KV_LORA, N_ROWS, lp['mla_kv_norm'], w_ukv, k_gain,
                                 kpe, 0, cos_mla, sin_mla, row_tab, tm, True)
    n_past = DEC_BATCH * PAST_LEN
    k_past, v_past = _mla_kv(ckv_all, layer * n_past, 0, n_past, lp['mla_kv_norm'], w_ukv, k_gain,
                             kpe_past_all, layer * n_past, cos_past, sin_past,
                             lambda i: lax.rem(i, PAST_LEN // tm), tm, False)
    b = _attn_prompt(q, k_new, v_new, BATCH, SEQ, 512)
    b = _attn_sample(q, k_new, v_new, k_past, v_past, N_PROMPT, into=b)

    zeros_ret = jnp.zeros((BATCH, RET_HEADS, RET_DK, RET_DV), F32)
    c, p_ret = _retention_branch(proj, cos_ret, sin_ret, lambda t: t, zeros_ret, 0, BATCH, SEQ, 256, 0)
    c, s_ret = _retention_branch(proj, cos_ret, sin_ret, lambda t: SEQ // DEC_SEQ, ret_s0_all, layer * DEC_BATCH,
                                 DEC_BATCH, DEC_SEQ, DEC_SEQ, N_PROMPT, into=c)

    merged = _merge(a, b, c, lp['w_branch'].astype(BF16), proj, 1024, 512)
    x = _res_matmul(merged, lp['w_out'].astype(BF16), x, 1024, 1024)

    mem_k_p, mem_v_p = _memkv(mem_prompt, lp['norm_mem'], lp['w_ck'].astype(BF16), lp['w_cv'].astype(BF16),
                              lp['cross_k_gain'], 256)
    qc = _rms_proj(x, lp['norm_cross'], lp['w_cq'].astype(BF16), BF16, 512, MEM_WIDTH)[0]
    o = _cross_attn(qc, mem_k_p.reshape(BATCH, N_MEM, MEM_WIDTH), mem_v_p.reshape(BATCH, N_MEM, MEM_WIDTH), 0,
                    lp['cross_q_gain'], BATCH, SEQ, 512, 0)
    o = _cross_attn(qc, mem_k_all, mem_v_all, layer * DEC_BATCH, lp['cross_q_gain'], DEC_BATCH, DEC_SEQ, DEC_SEQ,
                    N_PROMPT, into=o)
    x = _res_matmul(o, lp['w_co'].astype(BF16), x, 1024, 1024)

    x = _hier_moe(x, lp, w_gate_up, w_down, layer)

    kpe_rows = kpe[:, :MLA_ROPE]
    state_p = (c_kv[:N_PROMPT].reshape(BATCH, SEQ, MLA_KV_LORA), kpe_rows[:N_PROMPT].reshape(BATCH, SEQ, MLA_ROPE),
               p_s5_re, p_s5_im, p_ret,
               mem_k_p.reshape(BATCH, N_MEM, MEM_HEADS, MEM_HEAD_DIM),
               mem_v_p.reshape(BATCH, N_MEM, MEM_HEADS, MEM_HEAD_DIM))
    state_s = (c_kv[N_PROMPT:].reshape(DEC_BATCH, DEC_SEQ, MLA_KV_LORA),
               kpe_rows[N_PROMPT:].reshape(DEC_BATCH, DEC_SEQ, MLA_ROPE), s_s5_re, s_s5_im, s_ret)
    return x, state_p, state_s


_LAYER_PARAMS = (
    'norm_mix', 'w_in', 's5_a_re', 's5_a_im', 's5_log_dt', 's5_b_re', 's5_b_im', 's5_c_re', 's5_c_im', 's5_d',
    's5_w_glu', 'mla_q_norm', 'mla_w_uq', 'mla_kv_norm', 'mla_w_ukv', 'mla_q_gain', 'mla_k_gain', 'w_branch',
    'w_out', 'norm_cross', 'norm_mem', 'w_cq', 'w_ck', 'w_cv', 'cross_q_gain', 'cross_k_gain', 'w_co', 'norm_ffn',
    'w_group_router', 'b_group', 'w_expert_router', 'b_expert')


def kernel(x_prompt, x_sample, mem_prompt, cache_mla_ckv, cache_mla_kpe, state_s5_re, state_s5_im, state_ret,
           cache_mem_k, cache_mem_v, norm_mix, w_in, s5_a_re, s5_a_im, s5_log_dt, s5_b_re, s5_b_im, s5_c_re,
           s5_c_im, s5_d, s5_w_glu, mla_q_norm, mla_w_uq, mla_kv_norm, mla_w_ukv, mla_q_gain, mla_k_gain,
           w_branch, w_out, norm_cross, norm_mem, w_cq, w_ck, w_cv, cross_q_gain, cross_k_gain, w_co, norm_ffn,
           w_group_router, b_group, w_expert_router, b_expert, w_gate_up, w_down):
    params = dict(zip(_LAYER_PARAMS, (
        norm_mix, w_in, s5_a_re, s5_a_im, s5_log_dt, s5_b_re, s5_b_im, s5_c_re, s5_c_im, s5_d, s5_w_glu,
        mla_q_norm, mla_w_uq, mla_kv_norm, mla_w_ukv, mla_q_gain, mla_k_gain, w_branch, w_out, norm_cross,
        norm_mem, w_cq, w_ck, w_cv, cross_q_gain, cross_k_gain, w_co, norm_ffn, w_group_router, b_group,
        w_expert_router, b_expert)))
    assert x_prompt.shape == (BATCH, SEQ, D_MODEL) and x_sample.shape == (DEC_BATCH, DEC_SEQ, D_MODEL)
    assert cache_mla_ckv.shape == (DEPTH, DEC_BATCH, PAST_LEN, MLA_KV_LORA)

    pos_p = jnp.arange(SEQ, dtype=jnp.int32)
    pos_s = PAST_LEN + jnp.arange(DEC_SEQ, dtype=jnp.int32)
    tile_rows = 512
    tables = (_rope_tables(jnp.concatenate([pos_p, jnp.tile(pos_s, tile_rows // DEC_SEQ)]), MLA_ROPE)
              + _rope_tables(jnp.arange(PAST_LEN, dtype=jnp.int32), MLA_ROPE)
              + _rope_tables(jnp.concatenate([pos_p, pos_s]), RET_DK))

    x = jnp.concatenate([x_prompt.reshape(N_PROMPT, D_MODEL), x_sample.reshape(N_SAMPLE, D_MODEL)], axis=0)
    mem2d = mem_prompt.reshape(BATCH * N_MEM, D_MODEL)
    n_past = DEPTH * DEC_BATCH * PAST_LEN
    ckv_all = cache_mla_ckv.reshape(n_past, MLA_KV_LORA)
    kpe_past_all = jnp.pad(cache_mla_kpe.reshape(n_past, MLA_ROPE), ((0, 0), (0, LANES - MLA_ROPE)))
    ret_s0_all = state_ret.reshape(DEPTH * DEC_BATCH, RET_HEADS, RET_DK, RET_DV)
    mem_k_all = cache_mem_k.reshape(DEPTH * DEC_BATCH, N_MEM, MEM_HEADS, MEM_HEAD_DIM)
    mem_v_all = cache_mem_v.reshape(DEPTH * DEC_BATCH, N_MEM, MEM_HEADS, MEM_HEAD_DIM)
    outs_p, outs_s = [], []
    for l in range(DEPTH):
        lp = {name: value[l] for name, value in params.items()}
        caches = (ckv_all, kpe_past_all, state_s5_re[l], state_s5_im[l], ret_s0_all, mem_k_all, mem_v_all,
                  w_gate_up, w_down)
        x, st_p, st_s = _layer(x, mem2d, lp, l, caches, tables)
        outs_p.append(st_p)
        outs_s.append(st_s)
    stack = lambda outs, i: jnp.stack([o[i] for o in outs])
    return ((x[:N_PROMPT].reshape(BATCH, SEQ, D_MODEL), x[N_PROMPT:].reshape(DEC_BATCH, DEC_SEQ, D_MODEL))
            + tuple(stack(outs_p, i) for i in range(7)) + tuple(stack(outs_s, i) for i in range(5)))
```

```python
import functools
import math

import jax
import jax.numpy as jnp
from jax import lax
from jax.experimental import pallas as pl
from jax.experimental.pallas import tpu as pltpu

F32 = jnp.float32
BF16 = jnp.bfloat16

D_MODEL = 2048
BATCH = 4
SEQ = 2048
DEPTH = 2
DEC_BATCH = 32
DEC_SEQ = 32
PAST_LEN = 1024
CHUNK = 64
RMS_EPS = 1e-6
ROPE_THETA = 10000.0
NEG_INF = -1e30

S5_WIDTH = 1024
S5_GROUP_CH = 16
S5_GROUPS = 64
S5_STATE = 64
S5_COLS = S5_GROUPS * S5_STATE

MLA_HEADS = 8
MLA_NOPE = 128
MLA_ROPE = 64
MLA_QK = MLA_NOPE + MLA_ROPE
MLA_V = 128
MLA_Q_LORA = 512
MLA_KV_LORA = 512
MLA_HEAD_PAD = 256
MLA_QK_PAD = MLA_HEADS * MLA_HEAD_PAD

RET_HEADS = 4
RET_DK = 128
RET_DV = 256
BRANCH_WIDTH = 1024
N_BRANCH = 3

N_MEM = 256
MEM_HEADS = 4
MEM_HEAD_DIM = 128
MEM_WIDTH = MEM_HEADS * MEM_HEAD_DIM

MOE_GROUPS = 4
MOE_PER_GROUP = 8
N_EXPERTS = 32
MOE_TOP_K = 2
EXPERT_FF = 512

N_PROMPT = BATCH * SEQ
N_SAMPLE = DEC_BATCH * DEC_SEQ
N_ROWS = N_PROMPT + N_SAMPLE

COL_GMIX = 0
COL_S5 = 6144
COL_RV = 7168
COL_RG = 8192
COL_QLAT = 9216
COL_KVLAT = 9728
COL_RQ = 10240
COL_RK = 10752
PROJ_COLS = 11264
IN_S5 = (0, 1024)
IN_QLAT = (1024, 1536)
IN_KVLAT = (1536, 2048)
IN_KPE = (2048, 2112)
IN_RQ = (2112, 2624)
IN_RK = (2624, 3136)
IN_RV = (3136, 4160)
IN_RG = (4160, 5184)
IN_GMIX = (5184, 11328)

LANES = 128
EXPERT_ROWS = 256
N_PAIRS = N_ROWS * MOE_TOP_K
N_EXPERT_BLOCKS = -(-(N_PAIRS + N_EXPERTS * (EXPERT_ROWS - 1)) // EXPERT_ROWS)
VMEM_LIMIT = 56 * 1024 * 1024


def _cparams(sem):
    return pltpu.CompilerParams(dimension_semantics=sem, vmem_limit_bytes=VMEM_LIMIT)


def _rms_rows(x, g):
    r = lax.rsqrt(jnp.mean(x * x, axis=-1, keepdims=True) + RMS_EPS)
    return (x * r) * g


def _dot(a, b):
    return jnp.dot(a, b, preferred_element_type=F32)


def _rows_call(kernel_fn, into, *, in_specs, args, **kw):
    if into is None:
        out0 = kw['out_shape'][0] if isinstance(kw['out_shape'], (list, tuple)) else kw['out_shape']
        into = jnp.zeros(out0.shape, out0.dtype)

    def aliased_kernel(into_ref, *refs):
        del into_ref
        kernel_fn(*refs)

    return pl.pallas_call(aliased_kernel, in_specs=[pl.BlockSpec(memory_space=pl.ANY)] + in_specs,
                          input_output_aliases={0: 0}, **kw)(into, *args)


def _dot_nt(a, b):
    return lax.dot_general(a, b, (((1,), (1,)), ((), ())), preferred_element_type=F32)


def _rms_proj_kernel(x_ref, g_ref, w_ref, *refs, side):
    if side:
        w2_ref, o_ref, o2_ref, xn_ref = refs
    else:
        o_ref, xn_ref = refs

    @pl.when(pl.program_id(1) == 0)
    def _():
        xn_ref[...] = _rms_rows(x_ref[...], g_ref[...]).astype(BF16)
        if side:
            o2_ref[...] = _dot(xn_ref[...], w2_ref[...]).astype(o2_ref.dtype)

    o_ref[...] = _dot(xn_ref[...], w_ref[...]).astype(o_ref.dtype)


def _rms_proj(x, g, w, out_dtype, tm, tn, w2=None, out2_dtype=None):
    n, k = x.shape
    cols = w.shape[1]
    side = w2 is not None
    in_specs = [
        pl.BlockSpec((tm, k), lambda i, j: (i, 0)),
        pl.BlockSpec((1, k), lambda i, j: (0, 0)),
        pl.BlockSpec((k, tn), lambda i, j: (0, j)),
    ]
    out_specs = [pl.BlockSpec((tm, tn), lambda i, j: (i, j))]
    out_shape = [jax.ShapeDtypeStruct((n, cols), out_dtype)]
    args = [x, g.reshape(1, k), w]
    if side:
        cols2 = w2.shape[1]
        in_specs.append(pl.BlockSpec((k, cols2), lambda i, j: (0, 0)))
        out_specs.append(pl.BlockSpec((tm, cols2), lambda i, j: (i, 0)))
        out_shape.append(jax.ShapeDtypeStruct((n, cols2), out2_dtype))
        args.append(w2)
    return pl.pallas_call(
        functools.partial(_rms_proj_kernel, side=side),
        grid=(n // tm, cols // tn),
        in_specs=in_specs,
        out_specs=out_specs,
        out_shape=out_shape,
        scratch_shapes=[pltpu.VMEM((tm, k), BF16)],
        compiler_params=_cparams(("parallel", "arbitrary")),
        name="rms_proj_side" if side else "rms_proj",
    )(*args)


def _res_matmul_kernel(a_ref, w_ref, r_ref, o_ref):
    o_ref[...] = r_ref[...] + _dot(a_ref[...], w_ref[...])


def _res_matmul(a, w, res, tm, tn):
    n, k = a.shape
    cols = w.shape[1]
    return pl.pallas_call(
        _res_matmul_kernel,
        grid=(n // tm, cols // tn),
        in_specs=[
            pl.BlockSpec((tm, k), lambda i, j: (i, 0)),
            pl.BlockSpec((k, tn), lambda i, j: (0, j)),
            pl.BlockSpec((tm, tn), lambda i, j: (i, j)),
        ],
        out_specs=pl.BlockSpec((tm, tn), lambda i, j: (i, j)),
        out_shape=jax.ShapeDtypeStruct((n, cols), F32),
        compiler_params=_cparams(("parallel", "arbitrary")),
        name="res_matmul",
    )(a, w, res)


S5_K_SLAB = 256
S5_N_SLAB = 1024
S5_SLABS = S5_WIDTH // S5_K_SLAB
S5_SCAN_COLS = 1024
S5_SCAN_ROWS = 8


def _gelu_tanh(x):
    return 0.5 * x * (1.0 + jnp.tanh(math.sqrt(2.0 / math.pi) * (x + 0.044715 * (x * x * x))))


def _sigmoid(x):
    return 1.0 / (1.0 + jnp.exp(-x))


def _s5_kernel(u_ref, bre_ref, bim_ref, cre_ref, cim_ref, pw_ref, d_ref, wglu_ref,
               h0r_ref, h0i_ref, o_ref, hr_out_ref, hi_out_ref, sre_ref, sim_ref, cr_ref, ci_ref, *, tt):
    @pl.when(pl.program_id(1) == 0)
    def _():
        cr_ref[...] = h0r_ref[0]
        ci_ref[...] = h0i_ref[0]

    u = u_ref[...]
    for n in range(S5_SLABS):
        un = u[:, S5_K_SLAB * n:S5_K_SLAB * (n + 1)]
        sre_ref[:, S5_N_SLAB * n:S5_N_SLAB * (n + 1)] = _dot(un, bre_ref[n])
        sim_ref[:, S5_N_SLAB * n:S5_N_SLAB * (n + 1)] = _dot(un, bim_ref[n])

    for c in range(S5_COLS // S5_SCAN_COLS):
        sl = slice(c * S5_SCAN_COLS, (c + 1) * S5_SCAN_COLS)

        def body(blk, carry, sl=sl):
            cr, ci = carry
            rows = pl.ds(pl.multiple_of(blk * S5_SCAN_ROWS, S5_SCAN_ROWS), S5_SCAN_ROWS)
            xr = sre_ref[rows, sl]
            xi = sim_ref[rows, sl]
            for j in range(3):
                kr = pw_ref[j, 0, :, sl]
                ki = pw_ref[j, 1, :, sl]
                sr = pltpu.roll(xr, 1 << j, 0)
                si = pltpu.roll(xi, 1 << j, 0)
                xr, xi = xr + (kr * sr - ki * si), xi + (kr * si + ki * sr)
            pr = pw_ref[3, 0, :, sl]
            pi = pw_ref[3, 1, :, sl]
            hr = xr + (pr * cr - pi * ci)
            hi = xi + (pr * ci + pi * cr)
            sre_ref[rows, sl] = hr
            sim_ref[rows, sl] = hi
            last = S5_SCAN_ROWS - 1
            return (jnp.broadcast_to(hr[last:last + 1], hr.shape), jnp.broadcast_to(hi[last:last + 1], hi.shape))

        init = (jnp.broadcast_to(cr_ref[:, sl], (S5_SCAN_ROWS, S5_SCAN_COLS)),
                jnp.broadcast_to(ci_ref[:, sl], (S5_SCAN_ROWS, S5_SCAN_COLS)))
        hr, hi = lax.fori_loop(0, tt // S5_SCAN_ROWS, body, init)
        cr_ref[:, sl] = hr[0:1]
        ci_ref[:, sl] = hi[0:1]

    ys = []
    for n in range(S5_SLABS):
        hr_n = sre_ref[:, S5_N_SLAB * n:S5_N_SLAB * (n + 1)].astype(BF16)
        hi_n = sim_ref[:, S5_N_SLAB * n:S5_N_SLAB * (n + 1)].astype(BF16)
        ys.append(_dot(hr_n, cre_ref[n]) + _dot(hi_n, cim_ref[n]))
    y = jnp.concatenate(ys, axis=1) + d_ref[...] * u.astype(F32)
    z = _gelu_tanh(y)
    gate = _sigmoid(_dot(z.astype(BF16), wglu_ref[...]))
    o_ref[...] = (z * gate).astype(o_ref.dtype)
    hr_out_ref[0] = cr_ref[...]
    hi_out_ref[0] = ci_ref[...]


def _s5_tables(lp):
    a_re, a_im = lp['s5_a_re'], lp['s5_a_im']
    dt = jnp.exp(lp['s5_log_dt'])[:, None]
    mag = jnp.exp(a_re * dt)
    lb_re = mag * jnp.cos(a_im * dt)
    lb_im = mag * jnp.sin(a_im * dt)
    den = a_re * a_re + a_im * a_im
    n_re = lb_re - 1.0
    co_re = (n_re * a_re + lb_im * a_im) / den
    co_im = (lb_im * a_re - n_re * a_im) / den
    bb_re = co_re[..., None] * lp['s5_b_re'] - co_im[..., None] * lp['s5_b_im']
    bb_im = co_re[..., None] * lp['s5_b_im'] + co_im[..., None] * lp['s5_b_re']
    per_slab = S5_GROUPS // S5_SLABS
    eye = jnp.eye(per_slab, dtype=F32)

    def b_tiles(bb):
        bb = bb.reshape(S5_SLABS, per_slab, S5_STATE, S5_GROUP_CH)
        return (bb.transpose(0, 1, 3, 2)[:, :, :, None, :] * eye[None, :, None, :, None]).reshape(
            S5_SLABS, S5_K_SLAB, S5_N_SLAB).astype(BF16)

    def c_tiles(cc):
        cc = cc.reshape(S5_SLABS, per_slab, S5_GROUP_CH, S5_STATE)
        return (cc.transpose(0, 1, 3, 2)[:, :, :, None, :] * eye[None, :, None, :, None]).reshape(
            S5_SLABS, S5_N_SLAB, S5_K_SLAB).astype(BF16)

    def lam_pow(k):
        m = jnp.exp(k * (a_re * dt).reshape(1, S5_COLS))
        ang = k * (a_im * dt).reshape(1, S5_COLS)
        return jnp.stack([m * jnp.cos(ang), m * jnp.sin(ang)])

    t = jnp.arange(S5_SCAN_ROWS, dtype=F32)[:, None]
    steps = [jnp.where(t >= float(1 << j), lam_pow(jnp.full_like(t, float(1 << j))), 0.0) for j in range(3)]
    pw = jnp.stack(steps + [lam_pow(t + 1.0)])

    return dict(
        bre=b_tiles(bb_re), bim=b_tiles(bb_im),
        cre=c_tiles(lp['s5_c_re']), cim=c_tiles(-lp['s5_c_im']),
        pw=pw, d=lp['s5_d'].reshape(1, S5_WIDTH),
    )


def _s5_branch(proj, tabs, wglu, h0_re, h0_im, n_seq, seq_len, tt, row0, into=None):
    nt = seq_len // tt
    base = row0 // tt
    col = COL_S5 // S5_WIDTH
    const3 = lambda s, t: (0, 0, 0)
    const2 = lambda s, t: (0, 0)
    out, hr, hi = _rows_call(
        functools.partial(_s5_kernel, tt=tt), into,
        grid=(n_seq, nt),
        in_specs=[
            pl.BlockSpec((tt, S5_WIDTH), lambda s, t: (base + s * nt + t, col)),
            pl.BlockSpec((S5_SLABS, S5_K_SLAB, S5_N_SLAB), const3),
            pl.BlockSpec((S5_SLABS, S5_K_SLAB, S5_N_SLAB), const3),
            pl.BlockSpec((S5_SLABS, S5_N_SLAB, S5_K_SLAB), const3),
            pl.BlockSpec((S5_SLABS, S5_N_SLAB, S5_K_SLAB), const3),
            pl.BlockSpec((4, 2, S5_SCAN_ROWS, S5_COLS), lambda s, t: (0, 0, 0, 0)),
            pl.BlockSpec((1, S5_WIDTH), const2),
            pl.BlockSpec((S5_WIDTH, S5_WIDTH), const2),
            pl.BlockSpec((1, 1, S5_COLS), lambda s, t: (s, 0, 0)),
            pl.BlockSpec((1, 1, S5_COLS), lambda s, t: (s, 0, 0)),
        ],
        out_specs=[
            pl.BlockSpec((tt, S5_WIDTH), lambda s, t: (base + s * nt + t, 0)),
            pl.BlockSpec((1, 1, S5_COLS), lambda s, t: (s, 0, 0)),
            pl.BlockSpec((1, 1, S5_COLS), lambda s, t: (s, 0, 0)),
        ],
        out_shape=[
            jax.ShapeDtypeStruct((N_ROWS, S5_WIDTH), BF16),
            jax.ShapeDtypeStruct((n_seq, 1, S5_COLS), F32),
            jax.ShapeDtypeStruct((n_seq, 1, S5_COLS), F32),
        ],
        scratch_shapes=[
            pltpu.VMEM((tt, S5_COLS), F32), pltpu.VMEM((tt, S5_COLS), F32),
            pltpu.VMEM((1, S5_COLS), F32), pltpu.VMEM((1, S5_COLS), F32),
        ],
        compiler_params=_cparams(("parallel", "arbitrary")),
        name="s5_branch",
        args=(proj, tabs['bre'], tabs['bim'], tabs['cre'], tabs['cim'], tabs['pw'], tabs['d'], wglu,
              h0_re.reshape(n_seq, 1, S5_COLS), h0_im.reshape(n_seq, 1, S5_COLS)),
    )
    return (out, hr.reshape(n_seq, S5_GROUPS, S5_STATE), hi.reshape(n_seq, S5_GROUPS, S5_STATE))


def _rope_tables(pos, d):
    inv_freq = ROPE_THETA ** (-jnp.arange(0, d, 2, dtype=F32) / d)
    ang = pos.astype(F32)[:, None] * inv_freq[None, :]
    cos, sin = jnp.cos(ang), jnp.sin(ang)
    pad = jnp.zeros((pos.shape[0], LANES - d), F32)
    return (jnp.concatenate([cos, cos, pad], axis=1), jnp.concatenate([-sin, sin, pad], axis=1))


def _pad_head_vec(g):
    return jnp.concatenate([g, jnp.zeros((MLA_HEAD_PAD - MLA_QK,), F32)]).reshape(1, MLA_HEAD_PAD)


def _qk_head(nope, pe, gain, cos2, sin2, scale):
    ss = jnp.sum(nope * nope, axis=-1, keepdims=True) + jnp.sum(pe * pe, axis=-1, keepdims=True)
    r = lax.rsqrt(ss * (1.0 / MLA_QK) + RMS_EPS)
    nope = (nope * r) * gain[:, :MLA_NOPE]
    pe = (pe * r) * gain[:, MLA_NOPE:]
    lane = lax.broadcasted_iota(jnp.int32, pe.shape, 1)
    half = MLA_ROPE // 2
    swap = jnp.where(lane < half, pltpu.roll(pe, LANES - half, 1), pltpu.roll(pe, half, 1))
    pe = pe * cos2 + swap * sin2
    return nope * scale, pe * scale


def _mla_q_kernel(lat_ref, g_ref, w_ref, gain_ref, cos_ref, sin_ref, o_ref):
    xn = _rms_rows(lat_ref[...].astype(F32), g_ref[...]).astype(BF16)
    q = _dot(xn, w_ref[...])
    gain = gain_ref[...]
    cos2 = cos_ref[...]
    sin2 = sin_ref[...]
    for h in range(MLA_HEADS):
        c0 = h * MLA_HEAD_PAD
        nope, pe = _qk_head(q[:, c0:c0 + MLA_NOPE], q[:, c0 + MLA_NOPE:c0 + MLA_HEAD_PAD],
                            gain, cos2, sin2, MLA_QK ** -0.5)
        o_ref[:, c0:c0 + MLA_NOPE] = nope.astype(BF16)
        o_ref[:, c0 + MLA_NOPE:c0 + MLA_HEAD_PAD] = pe.astype(BF16)


def _mla_q(proj, g, w_uq_pad, gain_pad, cos2, sin2, tab_block, tm):
    n = proj.shape[0]
    col = COL_QLAT // MLA_Q_LORA
    c2 = lambda i: (0, 0)
    tab = lambda i: (tab_block(i), 0)
    return pl.pallas_call(
        _mla_q_kernel,
        grid=(n // tm,),
        in_specs=[
            pl.BlockSpec((tm, MLA_Q_LORA), lambda i: (i, col)),
            pl.BlockSpec((1, MLA_Q_LORA), c2),
            pl.BlockSpec((MLA_Q_LORA, MLA_QK_PAD), c2),
            pl.BlockSpec((1, MLA_HEAD_PAD), c2),
            pl.BlockSpec((tm, LANES), tab),
            pl.BlockSpec((tm, LANES), tab),
        ],
        out_specs=pl.BlockSpec((tm, MLA_QK_PAD), lambda i: (i, 0)),
        out_shape=jax.ShapeDtypeStruct((n, MLA_QK_PAD), BF16),
        compiler_params=_cparams(("parallel",)),
        name="mla_q",
    )(proj, g.reshape(1, -1), w_uq_pad, gain_pad, cos2, sin2)


def _mla_kv_kernel(lat_ref, g_ref, w_ref, gain_ref, kpe_ref, cos_ref, sin_ref, ckv_ref, k_ref, v_ref):
    lat = _rms_rows(lat_ref[...].astype(F32), g_ref[...])
    ckv_ref[...] = lat
    kv = _dot(lat.astype(BF16), w_ref[...])
    gain = gain_ref[...]
    kpe = kpe_ref[...]
    cos2 = cos_ref[...]
    sin2 = sin_ref[...]
    for h in range(MLA_HEADS):
        c0 = h * (MLA_NOPE + MLA_V)
        nope, pe = _qk_head(kv[:, c0:c0 + MLA_NOPE], kpe, gain, cos2, sin2, 1.0)
        k0 = h * MLA_HEAD_PAD
        k_ref[:, k0:k0 + MLA_NOPE] = nope.astype(BF16)
        k_ref[:, k0 + MLA_NOPE:k0 + MLA_HEAD_PAD] = pe.astype(BF16)
        v_ref[:, h * MLA_V:(h + 1) * MLA_V] = kv[:, c0 + MLA_NOPE:c0 + MLA_NOPE + MLA_V].astype(BF16)


def _mla_kv(proj, g, w_ukv, gain_pad, kpe_pad, cos2, sin2, tab_block, tm):
    n = proj.shape[0]
    c2 = lambda i: (0, 0)
    row = lambda i: (i, 0)
    tab = lambda i: (tab_block(i), 0)
    return pl.pallas_call(
        _mla_kv_kernel,
        grid=(n // tm,),
        in_specs=[
            pl.BlockSpec((tm, MLA_KV_LORA), lambda i: (i, COL_KVLAT // MLA_KV_LORA)),
            pl.BlockSpec((1, MLA_KV_LORA), c2),
            pl.BlockSpec((MLA_KV_LORA, MLA_HEADS * (MLA_NOPE + MLA_V)), c2),
            pl.BlockSpec((1, MLA_HEAD_PAD), c2),
            pl.BlockSpec((tm, LANES), row),
            pl.BlockSpec((tm, LANES), tab),
            pl.BlockSpec((tm, LANES), tab),
        ],
        out_specs=[pl.BlockSpec((tm, MLA_KV_LORA), row), pl.BlockSpec((tm, MLA_QK_PAD), row),
                   pl.BlockSpec((tm, MLA_HEADS * MLA_V), row)],
        out_shape=[jax.ShapeDtypeStruct((n, MLA_KV_LORA), F32), jax.ShapeDtypeStruct((n, MLA_QK_PAD), BF16),
                   jax.ShapeDtypeStruct((n, MLA_HEADS * MLA_V), BF16)],
        compiler_params=_cparams(("parallel",)),
        name="mla_kv",
    )(proj, g.reshape(1, -1), w_ukv, gain_pad, kpe_pad, cos2, sin2)


ATTN_HEADS_PER_STEP = 2


def _attn_prompt_kernel(q_ref, k_ref, v_ref, o_ref, *, tq):
    qi = pl.program_id(2)

    def key_tile(ki, state, diagonal):
        rows = pl.ds(pl.multiple_of(ki * tq, tq), tq)
        new_state = []
        for h in range(ATTN_HEADS_PER_STEP):
            qk = slice(h * MLA_HEAD_PAD, (h + 1) * MLA_HEAD_PAD)
            vs = slice(h * MLA_V, (h + 1) * MLA_V)
            s = _dot_nt(q_ref[:, qk], k_ref[rows, qk])
            if diagonal:
                row_chunk = lax.broadcasted_iota(jnp.int32, s.shape, 0) // CHUNK
                col_chunk = lax.broadcasted_iota(jnp.int32, s.shape, 1) // CHUNK
                s = jnp.where(col_chunk <= row_chunk, s, NEG_INF)
            m_old, l_old, acc_old = state[h]
            m_new = jnp.maximum(m_old, jnp.max(s, axis=-1, keepdims=True))
            alpha = jnp.exp(m_old - m_new)
            p = jnp.exp(s - m_new)
            l_new = alpha * l_old + jnp.sum(p, axis=-1, keepdims=True)
            acc_new = alpha * acc_old + _dot(p.astype(BF16), v_ref[rows, vs])
            new_state.append((m_new, l_new, acc_new))
        return tuple(new_state)

    init = tuple((jnp.full((tq, 1), NEG_INF, F32), jnp.zeros((tq, 1), F32), jnp.zeros((tq, MLA_V), F32))
                 for _ in range(ATTN_HEADS_PER_STEP))
    state = lax.fori_loop(0, qi, lambda ki, st: key_tile(ki, st, False), init)
    state = key_tile(qi, state, True)
    for h in range(ATTN_HEADS_PER_STEP):
        _, l_fin, acc_fin = state[h]
        o_ref[:, h * MLA_V:(h + 1) * MLA_V] = (acc_fin / l_fin).astype(o_ref.dtype)


def _attn_prompt(q, k, v, n_batch, seq_len, tq):
    assert tq % CHUNK == 0
    nq = seq_len // tq
    hp = ATTN_HEADS_PER_STEP
    return _rows_call(
        functools.partial(_attn_prompt_kernel, tq=tq), None,
        grid=(n_batch, MLA_HEADS // hp, nq),
        in_specs=[
            pl.BlockSpec((tq, hp * MLA_HEAD_PAD), lambda b, h, qi: (b * nq + qi, h)),
            pl.BlockSpec((seq_len, hp * MLA_HEAD_PAD), lambda b, h, qi: (b, h)),
            pl.BlockSpec((seq_len, hp * MLA_V), lambda b, h, qi: (b, h)),
        ],
        out_specs=pl.BlockSpec((tq, hp * MLA_V), lambda b, h, qi: (b * nq + qi, h)),
        out_shape=jax.ShapeDtypeStruct((N_ROWS, MLA_HEADS * MLA_V), BF16),
        compiler_params=_cparams(("parallel", "parallel", "arbitrary")),
        name="attn_prompt",
        args=(q, k, v),
    )


def _attn_sample_kernel(q_ref, kn_ref, vn_ref, c_ref, kpe_ref, cos_ref, sin_ref, wk_ref, wv_ref, gain_ref, sel_ref,
                        o_ref):
    c = c_ref[...].astype(BF16)
    k_raw = _dot(c, wk_ref[...])
    v_all = _dot(c, wv_ref[...]).astype(BF16)
    gain = gain_ref[...]
    gain_n = gain[:, :MLA_NOPE]
    kpe = kpe_ref[...]
    pe = kpe * gain[:, MLA_NOPE:]
    lane = lax.broadcasted_iota(jnp.int32, pe.shape, 1)
    half = MLA_ROPE // 2
    swap = jnp.where(lane < half, pltpu.roll(pe, LANES - half, 1), pltpu.roll(pe, half, 1))
    pe = (pe * cos_ref[...] + swap * sin_ref[...]).astype(BF16)
    ss = (_dot_nt(sel_ref[...], (k_raw * k_raw).astype(BF16))
          + _dot_nt(jnp.ones((MLA_HEADS, LANES), BF16), (kpe * kpe).astype(BF16)))
    r = lax.rsqrt(ss * (1.0 / MLA_QK) + RMS_EPS)
    k_raw = k_raw.astype(BF16)
    for h in range(MLA_HEADS):
        ks = slice(h * MLA_HEAD_PAD, (h + 1) * MLA_HEAD_PAD)
        vs = slice(h * MLA_V, (h + 1) * MLA_V)
        q = q_ref[:, ks]
        q_n = (q[:, :MLA_NOPE].astype(F32) * gain_n).astype(BF16)
        s_p = (_dot_nt(q_n, k_raw[:, vs]) + _dot_nt(q[:, MLA_NOPE:], pe)) * r[h:h + 1]
        s_n = _dot_nt(q, kn_ref[:, ks])
        m = jnp.maximum(jnp.max(s_p, axis=-1, keepdims=True), jnp.max(s_n, axis=-1, keepdims=True))
        p_p = jnp.exp(s_p - m)
        p_n = jnp.exp(s_n - m)
        l = jnp.sum(p_p, axis=-1, keepdims=True) + jnp.sum(p_n, axis=-1, keepdims=True)
        o = _dot(p_p.astype(BF16), v_all[:, vs]) + _dot(p_n.astype(BF16), vn_ref[:, vs])
        o_ref[:, vs] = (o / l).astype(o_ref.dtype)


def _attn_sample(q, k_new, v_new, ckv_all, kpe_all, past0, cos, sin, w_uk, w_uv, gain_pad, row0, into):
    assert (PAST_LEN + DEC_SEQ - 1) // CHUNK <= PAST_LEN // CHUNK
    base = row0 // DEC_SEQ
    new = lambda b: (base + b, 0)
    past = lambda b: (past0 + b, 0)
    c2 = lambda b: (0, 0)
    width = MLA_HEADS * MLA_NOPE
    sel = jnp.repeat(jnp.eye(MLA_HEADS, dtype=F32), MLA_NOPE, axis=1).astype(BF16)
    return _rows_call(
        _attn_sample_kernel, into,
        grid=(DEC_BATCH,),
        in_specs=[
            pl.BlockSpec((DEC_SEQ, MLA_QK_PAD), new),
            pl.BlockSpec((DEC_SEQ, MLA_QK_PAD), new),
            pl.BlockSpec((DEC_SEQ, MLA_HEADS * MLA_V), new),
            pl.BlockSpec((PAST_LEN, MLA_KV_LORA), past),
            pl.BlockSpec((PAST_LEN, LANES), past),
            pl.BlockSpec((PAST_LEN, LANES), c2),
            pl.BlockSpec((PAST_LEN, LANES), c2),
            pl.BlockSpec((MLA_KV_LORA, width), c2),
            pl.BlockSpec((MLA_KV_LORA, MLA_HEADS * MLA_V), c2),
            pl.BlockSpec((1, MLA_HEAD_PAD), c2),
            pl.BlockSpec((MLA_HEADS, width), c2),
        ],
        out_specs=pl.BlockSpec((DEC_SEQ, MLA_HEADS * MLA_V), new),
        out_shape=jax.ShapeDtypeStruct((N_ROWS, MLA_HEADS * MLA_V), BF16),
        compiler_params=_cparams(("parallel",)),
        name="attn_sample",
        args=(q, k_new, v_new, ckv_all, kpe_all, cos, sin, w_uk, w_uv, gain_pad, sel),
    )


def _retention_tables(block):
    log_g = jnp.log1p(-jnp.exp2(-5.0 - jnp.arange(RET_HEADS, dtype=F32)))
    idx = jnp.arange(block, dtype=F32)
    diff = idx[:, None] - idx[None, :]
    decay = jnp.where(diff >= 0, jnp.exp(log_g[:, None, None] * jnp.maximum(diff, 0.0)), 0.0)
    q_decay = jnp.exp(log_g[:, None] * (idx + 1.0))
    k_decay = jnp.exp(log_g[:, None] * (block - 1.0 - idx))
    blk_decay = jnp.exp(log_g * block)
    return (decay,
            jnp.broadcast_to(q_decay[:, :, None], (RET_HEADS, block, RET_DK)),
            jnp.broadcast_to(k_decay[:, :, None], (RET_HEADS, block, RET_DK)),
            jnp.broadcast_to(blk_decay[:, None, None], (RET_HEADS, 1, RET_DV)))


def _rope128(x, cos, sin):
    return x * cos + pltpu.roll(x, RET_DK // 2, 1) * sin


def _retention_kernel(rq_ref, rk_ref, rv_ref, rg_ref, cos_ref, sin_ref, dec_ref, qd_ref, kd_ref, bd_ref, s0_ref,
                      o_ref, s_out_ref, s_ref):
    @pl.when(pl.program_id(1) == 0)
    def _():
        s_ref[...] = s0_ref[0]

    cos = cos_ref[...]
    sin = sin_ref[...]
    for h in range(RET_HEADS):
        ks = slice(h * RET_DK, (h + 1) * RET_DK)
        vs = slice(h * RET_DV, (h + 1) * RET_DV)
        q = _rope128(rq_ref[:, ks].astype(F32), cos, sin) * (RET_DK ** -0.5)
        k = _rope128(rk_ref[:, ks].astype(F32), cos, sin)
        v = rv_ref[:, vs]
        s_old = s_ref[h]
        att = _dot_nt(q.astype(BF16), k.astype(BF16)) * dec_ref[h]
        o = _dot(att.astype(BF16), v) + _dot((q * qd_ref[h]).astype(BF16), s_old.astype(BF16))
        kd = (k * kd_ref[h]).astype(BF16)
        s_new = bd_ref[h] * s_old + lax.dot_general(kd, v, (((0,), (0,)), ((), ())), preferred_element_type=F32)
        s_ref[h] = s_new
        s_out_ref[0, h] = s_new
        mu = jnp.mean(o, axis=-1, keepdims=True)
        c = o - mu
        o = c * lax.rsqrt(jnp.mean(c * c, axis=-1, keepdims=True) + RMS_EPS)
        g = rg_ref[:, vs].astype(F32)
        o_ref[:, vs] = (o * (g * _sigmoid(g))).astype(o_ref.dtype)


def _retention_branch(proj, cos, sin, tab_block, s0, s0_base, n_seq, seq_len, block, row0, into=None):
    nblk = seq_len // block
    base = row0 // block
    dec, qd, kd, bd = _retention_tables(block)
    rows = lambda b, t: base + b * nblk + t
    tab = lambda b, t: (tab_block(t), 0)
    all3 = lambda b, t: (0, 0, 0)
    qk_w = RET_HEADS * RET_DK
    v_w = RET_HEADS * RET_DV
    out, s_last = _rows_call(
        _retention_kernel, into,
        grid=(n_seq, nblk),
        in_specs=[
            pl.BlockSpec((block, qk_w), lambda b, t: (rows(b, t), COL_RQ // qk_w)),
            pl.BlockSpec((block, qk_w), lambda b, t: (rows(b, t), COL_RK // qk_w)),
            pl.BlockSpec((block, v_w), lambda b, t: (rows(b, t), COL_RV // v_w)),
            pl.BlockSpec((block, v_w), lambda b, t: (rows(b, t), COL_RG // v_w)),
            pl.BlockSpec((block, LANES), tab),
            pl.BlockSpec((block, LANES), tab),
            pl.BlockSpec((RET_HEADS, block, block), all3),
            pl.BlockSpec((RET_HEADS, block, RET_DK), all3),
            pl.BlockSpec((RET_HEADS, block, RET_DK), all3),
            pl.BlockSpec((RET_HEADS, 1, RET_DV), all3),
            pl.BlockSpec((1, RET_HEADS, RET_DK, RET_DV), lambda b, t: (s0_base + b, 0, 0, 0)),
        ],
        out_specs=[
            pl.BlockSpec((block, v_w), lambda b, t: (rows(b, t), 0)),
            pl.BlockSpec((1, RET_HEADS, RET_DK, RET_DV), lambda b, t: (b, 0, 0, 0)),
        ],
        out_shape=[
            jax.ShapeDtypeStruct((N_ROWS, v_w), BF16),
            jax.ShapeDtypeStruct((n_seq, RET_HEADS, RET_DK, RET_DV), F32),
        ],
        scratch_shapes=[pltpu.VMEM((RET_HEADS, RET_DK, RET_DV), F32)],
        compiler_params=_cparams(("parallel", "arbitrary")),
        name="retention",
        args=(proj, proj, proj, proj, cos, sin, dec, qd, kd, bd, s0),
    )
    return out, s_last


def _merge_kernel(a_ref, b_ref, c_ref, w_ref, g0_ref, g1_ref, g2_ref, o_ref):
    acc = _sigmoid(g0_ref[...].astype(F32)) * _dot(a_ref[...], w_ref[0])
    acc += _sigmoid(g1_ref[...].astype(F32)) * _dot(b_ref[...], w_ref[1])
    acc += _sigmoid(g2_ref[...].astype(F32)) * _dot(c_ref[...], w_ref[2])
    o_ref[...] = acc.astype(o_ref.dtype)


def _merge(o_a, o_b, o_c, w_branch, proj, tm, tn):
    n = o_a.shape[0]
    nj = D_MODEL // tn
    br = pl.BlockSpec((tm, BRANCH_WIDTH), lambda i, j: (i, 0))
    gate = lambda b: pl.BlockSpec((tm, tn), lambda i, j: (i, COL_GMIX // tn + b * nj + j))
    return pl.pallas_call(
        _merge_kernel,
        grid=(n // tm, nj),
        in_specs=[br, br, br,
                  pl.BlockSpec((N_BRANCH, BRANCH_WIDTH, tn), lambda i, j: (0, 0, j)),
                  gate(0), gate(1), gate(2)],
        out_specs=pl.BlockSpec((tm, tn), lambda i, j: (i, j)),
        out_shape=jax.ShapeDtypeStruct((n, D_MODEL), BF16),
        compiler_params=_cparams(("parallel", "arbitrary")),
        name="branch_merge",
    )(o_a, o_b, o_c, w_branch, proj, proj, proj)


def _memkv_kernel(m_ref, g_ref, wk_ref, wv_ref, kg_ref, k_ref, v_ref):
    mn = _rms_rows(m_ref[...], g_ref[...]).astype(BF16)
    kk = _dot(mn, wk_ref[...])
    for h in range(MEM_HEADS):
        sl = slice(h * MEM_HEAD_DIM, (h + 1) * MEM_HEAD_DIM)
        k_ref[:, sl] = _rms_rows(kk[:, sl], kg_ref[...])
    v_ref[...] = _dot(mn, wv_ref[...])


def _memkv(mem, g, wk, wv, k_gain, tm):
    n, k = mem.shape
    c2 = lambda i: (0, 0)
    return pl.pallas_call(
        _memkv_kernel,
        grid=(n // tm,),
        in_specs=[
            pl.BlockSpec((tm, k), lambda i: (i, 0)),
            pl.BlockSpec((1, k), c2),
            pl.BlockSpec((k, MEM_WIDTH), c2),
            pl.BlockSpec((k, MEM_WIDTH), c2),
            pl.BlockSpec((1, MEM_HEAD_DIM), c2),
        ],
        out_specs=[pl.BlockSpec((tm, MEM_WIDTH), lambda i: (i, 0)), pl.BlockSpec((tm, MEM_WIDTH), lambda i: (i, 0))],
        out_shape=[jax.ShapeDtypeStruct((n, MEM_WIDTH), F32), jax.ShapeDtypeStruct((n, MEM_WIDTH), F32)],
        compiler_params=_cparams(("parallel",)),
        name="memory_kv",
    )(mem, g.reshape(1, k), wk, wv, k_gain.reshape(1, MEM_HEAD_DIM))


def _cross_attn_kernel(q_ref, k_ref, v_ref, qg_ref, o_ref, *, head_axis):
    for h in range(MEM_HEADS):
        sl = slice(h * MEM_HEAD_DIM, (h + 1) * MEM_HEAD_DIM)
        if head_axis:
            k, v = k_ref[0, :, h, :], v_ref[0, :, h, :]
        else:
            k, v = k_ref[0, :, sl], v_ref[0, :, sl]
        q = _rms_rows(q_ref[:, sl].astype(F32), qg_ref[...]) * (MEM_HEAD_DIM ** -0.5)
        s = _dot_nt(q.astype(BF16), k.astype(BF16))
        p = jnp.exp(s - jnp.max(s, axis=-1, keepdims=True))
        l = jnp.sum(p, axis=-1, keepdims=True)
        o = _dot(p.astype(BF16), v.astype(BF16))
        o_ref[:, sl] = (o / l).astype(o_ref.dtype)


def _cross_attn(qc, mem_k, mem_v, mem0, q_gain, n_batch, rows_per_batch, tm, row0, into=None):
    tiles = rows_per_batch // tm
    base = row0 // tm
    head_axis = mem_k.ndim == 4
    if head_axis:
        mem_spec = pl.BlockSpec((1, N_MEM, MEM_HEADS, MEM_HEAD_DIM), lambda b, t: (mem0 + b, 0, 0, 0))
    else:
        mem_spec = pl.BlockSpec((1, N_MEM, MEM_WIDTH), lambda b, t: (mem0 + b, 0, 0))
    return _rows_call(
        functools.partial(_cross_attn_kernel, head_axis=head_axis), into,
        grid=(n_batch, tiles),
        in_specs=[
            pl.BlockSpec((tm, MEM_WIDTH), lambda b, t: (base + b * tiles + t, 0)),
            mem_spec, mem_spec,
            pl.BlockSpec((1, MEM_HEAD_DIM), lambda b, t: (0, 0)),
        ],
        out_specs=pl.BlockSpec((tm, MEM_WIDTH), lambda b, t: (base + b * tiles + t, 0)),
        out_shape=jax.ShapeDtypeStruct((N_ROWS, MEM_WIDTH), BF16),
        compiler_params=_cparams(("parallel", "arbitrary")),
        name="cross_attn",
        args=(qc, mem_k, mem_v, q_gain.reshape(1, MEM_HEAD_DIM)),
    )


ROUTER_LANES = LANES
ROW_CHUNKS = D_MODEL // LANES
FAR_LANE = 4 * LANES


def _split_bf16(x):
    hi = x.astype(BF16)
    lo = (x - hi.astype(F32)).astype(BF16)
    return hi, lo


def _router_kernel(x_ref, g_ref, w_ref, b_ref, xn_ref, ids_ref, gates_ref):
    xn = _rms_rows(x_ref[...], g_ref[...])
    for c in range(ROW_CHUNKS):
        xn_ref[:, c, :] = xn[:, c * LANES:(c + 1) * LANES]
    xh, xl = _split_bf16(xn)
    wh, wl = _split_bf16(w_ref[...])
    logits = _dot(xh, wh) + (_dot(xh, wl) + _dot(xl, wh)) + b_ref[...]
    lane = lax.broadcasted_iota(jnp.int32, logits.shape, 1)

    def first_max(vals):
        top = jnp.max(vals, axis=-1, keepdims=True)
        idx = jnp.min(jnp.where(vals == top, lane, FAR_LANE), axis=-1, keepdims=True)
        return top, idx

    g_mask = jnp.logical_and(lane >= N_EXPERTS, lane < N_EXPERTS + MOE_GROUPS)
    g_top, g_lane = first_max(jnp.where(g_mask, logits, NEG_INF))
    p_group = 1.0 / jnp.sum(jnp.where(g_mask, jnp.exp(logits - g_top), 0.0), axis=-1, keepdims=True)
    g_sel = g_lane - N_EXPERTS
    e_mask = jnp.logical_and(lane < N_EXPERTS, lane // MOE_PER_GROUP == g_sel)
    e_vals = jnp.where(e_mask, logits, NEG_INF)
    e1_top, e1 = first_max(e_vals)
    e2_top, e2 = first_max(jnp.where(lane == e1, NEG_INF, e_vals))
    t = jnp.exp(e2_top - e1_top)
    gate1 = p_group / (1.0 + t)
    gate2 = p_group * t / (1.0 + t)
    ids_ref[...] = jnp.where(lane == 0, e1, jnp.where(lane == 1, e2, 0))
    gates_ref[...] = jnp.where(lane == 0, gate1, jnp.where(lane == 1, gate2, 0.0))


def _router(x, g, w_router, b_router, tm):
    n, k = x.shape
    c2 = lambda i: (0, 0)
    row = lambda i: (i, 0)
    return pl.pallas_call(
        _router_kernel,
        grid=(n // tm,),
        in_specs=[
            pl.BlockSpec((tm, k), row),
            pl.BlockSpec((1, k), c2),
            pl.BlockSpec((k, ROUTER_LANES), c2),
            pl.BlockSpec((1, ROUTER_LANES), c2),
        ],
        out_specs=[pl.BlockSpec((tm, ROW_CHUNKS, LANES), lambda i: (i, 0, 0)), pl.BlockSpec((tm, ROUTER_LANES), row),
                   pl.BlockSpec((tm, ROUTER_LANES), row)],
        out_shape=[jax.ShapeDtypeStruct((n, ROW_CHUNKS, LANES), F32),
                   jax.ShapeDtypeStruct((n, ROUTER_LANES), jnp.int32),
                   jax.ShapeDtypeStruct((n, ROUTER_LANES), F32)],
        compiler_params=_cparams(("parallel",)),
        name="moe_router",
    )(x, g.reshape(1, k), w_router, b_router)


def _route_plan(expert_idx):
    flat_e = expert_idx.reshape(N_PAIRS)
    onehot = (flat_e[:, None] == jnp.arange(N_EXPERTS, dtype=jnp.int32)[None, :]).astype(jnp.int32)
    csum = jnp.cumsum(onehot, axis=0)
    rank = jnp.take_along_axis(csum, flat_e[:, None], axis=1)[:, 0] - 1
    counts = csum[-1]
    padded = (counts + EXPERT_ROWS - 1) // EXPERT_ROWS * EXPERT_ROWS
    pad_ends = jnp.cumsum(padded)
    pad_starts = pad_ends - padded
    dest = (pad_starts[flat_e] + rank).astype(jnp.int32)
    n_used = (pad_ends[-1] // EXPERT_ROWS).astype(jnp.int32).reshape(1)
    block_start = jnp.arange(N_EXPERT_BLOCKS, dtype=jnp.int32) * EXPERT_ROWS
    block_e = jnp.sum((pad_ends[None, :] <= block_start[:, None]).astype(jnp.int32), axis=1)
    block_e = jnp.minimum(block_e, N_EXPERTS - 1)
    src = jnp.zeros((N_EXPERT_BLOCKS * EXPERT_ROWS,), jnp.int32).at[dest].set(
        jnp.arange(N_PAIRS, dtype=jnp.int32) // MOE_TOP_K)
    return dest, src, block_e, n_used


DMA_UNROLL = 8


def _experts_kernel(src_ref, be_ref, nu_ref, x_hbm, wgu_ref, wd_ref, o_ref, xbuf_ref, sem):
    i = pl.program_id(0)
    n_used = nu_ref[0]
    slot = lax.rem(i, 2)

    def row_copy(blk, buf, r):
        return pltpu.make_async_copy(x_hbm.at[pl.ds(src_ref[blk * EXPERT_ROWS + r], 1)],
                                     xbuf_ref.at[buf, pl.ds(r, 1)], sem.at[buf])

    def start_block(blk, buf):
        def body(c, carry):
            for u in range(DMA_UNROLL):
                row_copy(blk, buf, c * DMA_UNROLL + u).start(priority=u % 2)
            return carry
        lax.fori_loop(0, EXPERT_ROWS // DMA_UNROLL, body, 0)

    def wait_block(blk, buf):
        def body(c, carry):
            for u in range(DMA_UNROLL):
                row_copy(blk, buf, c * DMA_UNROLL + u).wait()
            return carry
        lax.fori_loop(0, EXPERT_ROWS // DMA_UNROLL, body, 0)

    @pl.when(jnp.logical_and(i == 0, n_used > 0))
    def _():
        start_block(0, 0)

    @pl.when(i + 1 < n_used)
    def _():
        start_block(i + 1, 1 - slot)

    @pl.when(i < n_used)
    def _():
        wait_block(i, slot)
        x = jnp.concatenate([xbuf_ref[slot, :, c, :] for c in range(ROW_CHUNKS)], axis=1)
        gu = _dot(x.astype(BF16), wgu_ref[0, 0].astype(BF16))
        g = gu[:, :EXPERT_FF]
        h = (g * _sigmoid(g)) * gu[:, EXPERT_FF:]
        y = _dot(h.astype(BF16), wd_ref[0, 0].astype(BF16))
        for c in range(ROW_CHUNKS):
            o_ref[:, c, :] = y[:, c * LANES:(c + 1) * LANES]

    @pl.when(i >= n_used)
    def _():
        o_ref[...] = jnp.zeros(o_ref.shape, o_ref.dtype)


def _experts(xn, src, block_e, n_used, w_gate_up, w_down, layer):
    return pl.pallas_call(
        _experts_kernel,
        grid_spec=pltpu.PrefetchScalarGridSpec(
            num_scalar_prefetch=3,
            grid=(N_EXPERT_BLOCKS,),
            in_specs=[
                pl.BlockSpec(memory_space=pl.ANY),
                pl.BlockSpec((1, 1, D_MODEL, 2 * EXPERT_FF), lambda i, src, be, nu: (layer, be[i], 0, 0)),
                pl.BlockSpec((1, 1, EXPERT_FF, D_MODEL), lambda i, src, be, nu: (layer, be[i], 0, 0)),
            ],
            out_specs=pl.BlockSpec((EXPERT_ROWS, ROW_CHUNKS, LANES), lambda i, src, be, nu: (i, 0, 0)),
            scratch_shapes=[pltpu.VMEM((2, EXPERT_ROWS, ROW_CHUNKS, LANES), F32), pltpu.SemaphoreType.DMA((2,))],
        ),
        out_shape=jax.ShapeDtypeStruct((N_EXPERT_BLOCKS * EXPERT_ROWS, ROW_CHUNKS, LANES), F32),
        compiler_params=_cparams(("arbitrary",)),
        name="moe_experts",
    )(src, block_e, n_used, xn, w_gate_up, w_down)


def _combine_kernel(dest_ref, x_ref, gates_ref, y_hbm, o_ref, buf_ref, sem, *, rows):
    base = pl.program_id(0) * rows
    per_iter = DMA_UNROLL // MOE_TOP_K

    def row_copy(i, k):
        return pltpu.make_async_copy(y_hbm.at[pl.ds(dest_ref[(base + i) * MOE_TOP_K + k], 1)],
                                     buf_ref.at[k, pl.ds(i, 1)], sem)

    def start(c, carry):
        for u in range(per_iter):
            for k in range(MOE_TOP_K):
                row_copy(c * per_iter + u, k).start(priority=k)
        return carry

    def wait(c, carry):
        for u in range(per_iter):
            for k in range(MOE_TOP_K):
                row_copy(c * per_iter + u, k).wait()
        return carry

    lax.fori_loop(0, rows // per_iter, start, 0)
    lax.fori_loop(0, rows // per_iter, wait, 0)
    gates = gates_ref[...]
    for c in range(ROW_CHUNKS):
        cols = slice(c * LANES, (c + 1) * LANES)
        o_ref[:, cols] = x_ref[:, cols] + (gates[:, 0:1] * buf_ref[0, :, c, :] + gates[:, 1:2] * buf_ref[1, :, c, :])


def _combine(x, gates, y_sorted, dest, rows):
    n, width = x.shape
    return pl.pallas_call(
        functools.partial(_combine_kernel, rows=rows),
        grid_spec=pltpu.PrefetchScalarGridSpec(
            num_scalar_prefetch=1,
            grid=(n // rows,),
            in_specs=[
                pl.BlockSpec((rows, width), lambda i, d: (i, 0)),
                pl.BlockSpec((rows, ROUTER_LANES), lambda i, d: (i, 0)),
                pl.BlockSpec(memory_space=pl.ANY),
            ],
            out_specs=pl.BlockSpec((rows, width), lambda i, d: (i, 0)),
            scratch_shapes=[pltpu.VMEM((MOE_TOP_K, rows, ROW_CHUNKS, LANES), F32), pltpu.SemaphoreType.DMA(())],
        ),
        out_shape=jax.ShapeDtypeStruct((n, width), F32),
        compiler_params=_cparams(("arbitrary",)),
        name="moe_combine",
    )(dest, x, gates, y_sorted)


def _hier_moe(x, lp, w_gate_up, w_down, layer):
    w_router = jnp.concatenate(
        [lp['w_expert_router'], lp['w_group_router'],
         jnp.zeros((D_MODEL, ROUTER_LANES - N_EXPERTS - MOE_GROUPS), F32)], axis=1)
    b_router = jnp.concatenate(
        [lp['b_expert'], lp['b_group'], jnp.zeros((ROUTER_LANES - N_EXPERTS - MOE_GROUPS,), F32)]).reshape(1, -1)
    xn, ids, gates = _router(x, lp['norm_ffn'], w_router, b_router, 512)
    dest, src, block_e, n_used = _route_plan(ids[:, :MOE_TOP_K])
    y_sorted = _experts(xn, src, block_e, n_used, w_gate_up, w_down, layer)
    return _combine(x, gates, y_sorted, dest, 256)


def _cols(w, *ranges):
    return jnp.concatenate([w[:, a:b] for a, b in ranges], axis=1)


def _prompt_or_sample_block(i, tm):
    return jnp.where(i < N_PROMPT // tm, lax.rem(i, SEQ // tm), SEQ // tm)


def _layer(x, mem_prompt, lp, layer, caches, tables):
    (ckv_all, kpe_past_all, h0_re, h0_im, ret_s0_all, mem_k_all, mem_v_all, w_gate_up, w_down) = caches
    cos_mla, sin_mla, cos_past, sin_past, cos_ret, sin_ret = tables
    tm = 512
    w_in = lp['w_in']
    w_main = _cols(w_in, IN_GMIX, IN_S5, IN_RV, IN_RG, IN_QLAT, IN_KVLAT, IN_RQ, IN_RK).astype(BF16)
    w_kpe = jnp.pad(_cols(w_in, IN_KPE), ((0, 0), (0, LANES - MLA_ROPE))).astype(BF16)
    proj, kpe = _rms_proj(x, lp['norm_mix'], w_main, BF16, 1024, 1024, w2=w_kpe, out2_dtype=F32)

    s5_tabs = _s5_tables(lp)
    w_glu = lp['s5_w_glu'].astype(BF16)
    zeros_s5 = jnp.zeros((BATCH, S5_GROUPS, S5_STATE), F32)
    a, p_s5_re, p_s5_im = _s5_branch(proj, s5_tabs, w_glu, zeros_s5, zeros_s5, BATCH, SEQ, 256, 0)
    a, s_s5_re, s_s5_im = _s5_branch(proj, s5_tabs, w_glu, h0_re, h0_im, DEC_BATCH, DEC_SEQ, DEC_SEQ, N_PROMPT,
                                     into=a)

    w_uq = lp['mla_w_uq'].reshape(MLA_Q_LORA, MLA_HEADS, MLA_QK)
    w_uq = jnp.pad(w_uq, ((0, 0), (0, 0), (0, MLA_HEAD_PAD - MLA_QK))).reshape(MLA_Q_LORA, MLA_QK_PAD).astype(BF16)
    w_ukv = lp['mla_w_ukv'].astype(BF16)
    q_gain = _pad_head_vec(lp['mla_q_gain'])
    k_gain = _pad_head_vec(lp['mla_k_gain'])
    row_tab = lambda i: _prompt_or_sample_block(i, tm)
    q = _mla_q(proj, lp['mla_q_norm'], w_uq, q_gain, cos_mla, sin_mla, row_tab, tm)
    c_kv, k_new, v_new = _mla_kv(proj, lp['mla_kv_norm'], w_ukv, k_gain, kpe, cos_mla, sin_mla, row_tab, tm)
    w_ukv_heads = w_ukv.reshape(MLA_KV_LORA, MLA_HEADS, MLA_NOPE + MLA_V)
    w_uk = w_ukv_heads[:, :, :MLA_NOPE].reshape(MLA_KV_LORA, MLA_HEADS * MLA_NOPE)
    w_uv = w_ukv_heads[:, :, MLA_NOPE:].reshape(MLA_KV_LORA, MLA_HEADS * MLA_V)
    b = _attn_prompt(q, k_new, v_new, BATCH, SEQ, 512)
    b = _attn_sample(q, k_new, v_new, ckv_all, kpe_past_all, layer * DEC_BATCH, cos_past, sin_past, w_uk, w_uv,
                     k_gain, N_PROMPT, into=b)

    zeros_ret = jnp.zeros((BATCH, RET_HEADS, RET_DK, RET_DV), F32)
    c, p_ret = _retention_branch(proj, cos_ret, sin_ret, lambda t: t, zeros_ret, 0, BATCH, SEQ, 256, 0)
    c, s_ret = _retention_branch(proj, cos_ret, sin_ret, lambda t: SEQ // DEC_SEQ, ret_s0_all, layer * DEC_BATCH,
                                 DEC_BATCH, DEC_SEQ, DEC_SEQ, N_PROMPT, into=c)

    merged = _merge(a, b, c, lp['w_branch'].astype(BF16), proj, 1024, 512)
    x = _res_matmul(merged, lp['w_out'].astype(BF16), x, 1024, 1024)

    mem_k_p, mem_v_p = _memkv(mem_prompt, lp['norm_mem'], lp['w_ck'].astype(BF16), lp['w_cv'].astype(BF16),
                              lp['cross_k_gain'], 256)
    qc = _rms_proj(x, lp['norm_cross'], lp['w_cq'].astype(BF16), BF16, 512, MEM_WIDTH)[0]
    o = _cross_attn(qc, mem_k_p.reshape(BATCH, N_MEM, MEM_WIDTH), mem_v_p.reshape(BATCH, N_MEM, MEM_WIDTH), 0,
                    lp['cross_q_gain'], BATCH, SEQ, 512, 0)
    o = _cross_attn(qc, mem_k_all, mem_v_all, layer * DEC_BATCH, lp['cross_q_gain'], DEC_BATCH, DEC_SEQ, DEC_SEQ,
                    N_PROMPT, into=o)
    x = _res_matmul(o, lp['w_co'].astype(BF16), x, 1024, 1024)

    x = _hier_moe(x, lp, w_gate_up, w_down, layer)

    kpe_rows = kpe[:, :MLA_ROPE]
    state_p = (c_kv[:N_PROMPT].reshape(BATCH, SEQ, MLA_KV_LORA), kpe_rows[:N_PROMPT].reshape(BATCH, SEQ, MLA_ROPE),
               p_s5_re, p_s5_im, p_ret,
               mem_k_p.reshape(BATCH, N_MEM, MEM_HEADS, MEM_HEAD_DIM),
               mem_v_p.reshape(BATCH, N_MEM, MEM_HEADS, MEM_HEAD_DIM))
    state_s = (c_kv[N_PROMPT:].reshape(DEC_BATCH, DEC_SEQ, MLA_KV_LORA),
               kpe_rows[N_PROMPT:].reshape(DEC_BATCH, DEC_SEQ, MLA_ROPE), s_s5_re, s_s5_im, s_ret)
    return x, state_p, state_s


_LAYER_PARAMS = (
    'norm_mix', 'w_in', 's5_a_re', 's5_a_im', 's5_log_dt', 's5_b_re', 's5_b_im', 's5_c_re', 's5_c_im', 's5_d',
    's5_w_glu', 'mla_q_norm', 'mla_w_uq', 'mla_kv_norm', 'mla_w_ukv', 'mla_q_gain', 'mla_k_gain', 'w_branch',
    'w_out', 'norm_cross', 'norm_mem', 'w_cq', 'w_ck', 'w_cv', 'cross_q_gain', 'cross_k_gain', 'w_co', 'norm_ffn',
    'w_group_router', 'b_group', 'w_expert_router', 'b_expert')


def kernel(x_prompt, x_sample, mem_prompt, cache_mla_ckv, cache_mla_kpe, state_s5_re, state_s5_im, state_ret,
           cache_mem_k, cache_mem_v, norm_mix, w_in, s5_a_re, s5_a_im, s5_log_dt, s5_b_re, s5_b_im, s5_c_re,
           s5_c_im, s5_d, s5_w_glu, mla_q_norm, mla_w_uq, mla_kv_norm, mla_w_ukv, mla_q_gain, mla_k_gain,
           w_branch, w_out, norm_cross, norm_mem, w_cq, w_ck, w_cv, cross_q_gain, cross_k_gain, w_co, norm_ffn,
           w_group_router, b_group, w_expert_router, b_expert, w_gate_up, w_down):
    params = dict(zip(_LAYER_PARAMS, (
        norm_mix, w_in, s5_a_re, s5_a_im, s5_log_dt, s5_b_re, s5_b_im, s5_c_re, s5_c_im, s5_d, s5_w_glu,
        mla_q_norm, mla_w_uq, mla_kv_norm, mla_w_ukv, mla_q_gain, mla_k_gain, w_branch, w_out, norm_cross,
        norm_mem, w_cq, w_ck, w_cv, cross_q_gain, cross_k_gain, w_co, norm_ffn, w_group_router, b_group,
        w_expert_router, b_expert)))
    assert x_prompt.shape == (BATCH, SEQ, D_MODEL) and x_sample.shape == (DEC_BATCH, DEC_SEQ, D_MODEL)
    assert cache_mla_ckv.shape == (DEPTH, DEC_BATCH, PAST_LEN, MLA_KV_LORA)

    pos_p = jnp.arange(SEQ, dtype=jnp.int32)
    pos_s = PAST_LEN + jnp.arange(DEC_SEQ, dtype=jnp.int32)
    tile_rows = 512
    tables = (_rope_tables(jnp.concatenate([pos_p, jnp.tile(pos_s, tile_rows // DEC_SEQ)]), MLA_ROPE)
              + _rope_tables(jnp.arange(PAST_LEN, dtype=jnp.int32), MLA_ROPE)
              + _rope_tables(jnp.concatenate([pos_p, pos_s]), RET_DK))

    x = jnp.concatenate([x_prompt.reshape(N_PROMPT, D_MODEL), x_sample.reshape(N_SAMPLE, D_MODEL)], axis=0)
    mem2d = mem_prompt.reshape(BATCH * N_MEM, D_MODEL)
    n_past = DEPTH * DEC_BATCH * PAST_LEN
    ckv_all = cache_mla_ckv.reshape(n_past, MLA_KV_LORA)
    kpe_past_all = jnp.pad(cache_mla_kpe.reshape(n_past, MLA_ROPE), ((0, 0), (0, LANES - MLA_ROPE)))
    ret_s0_all = state_ret.reshape(DEPTH * DEC_BATCH, RET_HEADS, RET_DK, RET_DV)
    mem_k_all = cache_mem_k.reshape(DEPTH * DEC_BATCH, N_MEM, MEM_HEADS, MEM_HEAD_DIM)
    mem_v_all = cache_mem_v.reshape(DEPTH * DEC_BATCH, N_MEM, MEM_HEADS, MEM_HEAD_DIM)
    outs_p, outs_s = [], []
    for l in range(DEPTH):
        lp = {name: value[l] for name, value in params.items()}
        caches = (ckv_all, kpe_past_all, state_s5_re[l], state_s5_im[l], ret_s0_all, mem_k_all, mem_v_all,
                  w_gate_up, w_down)
        x, st_p, st_s = _layer(x, mem2d, lp, l, caches, tables)
        outs_p.append(st_p)
        outs_s.append(st_s)
    stack = lambda outs, i: jnp.stack([o[i] for o in outs])
    return ((x[:N_PROMPT].reshape(BATCH, SEQ, D_MODEL), x[N_PROMPT:].reshape(DEC_BATCH, DEC_SEQ, D_MODEL))
            + tuple(stack(outs_p, i) for i in range(7)) + tuple(stack(outs_s, i) for i in range(5)))
```

```python
import functools
import math

import jax
import jax.numpy as jnp
from jax import lax
from jax.experimental import pallas as pl
from jax.experimental.pallas import tpu as pltpu

F32 = jnp.float32
BF16 = jnp.bfloat16

D_MODEL = 2048
BATCH = 4
SEQ = 2048
DEPTH = 2
DEC_BATCH = 32
DEC_SEQ = 32
PAST_LEN = 1024
CHUNK = 64
RMS_EPS = 1e-6
ROPE_THETA = 10000.0
NEG_INF = -1e30

S5_WIDTH = 1024
S5_GROUP_CH = 16
S5_GROUPS = 64
S5_STATE = 64
S5_COLS = S5_GROUPS * S5_STATE

MLA_HEADS = 8
MLA_NOPE = 128
MLA_ROPE = 64
MLA_QK = MLA_NOPE + MLA_ROPE
MLA_V = 128
MLA_Q_LORA = 512
MLA_KV_LORA = 512
MLA_HEAD_PAD = 256
MLA_QK_PAD = MLA_HEADS * MLA_HEAD_PAD

RET_HEADS = 4
RET_DK = 128
RET_DV = 256
BRANCH_WIDTH = 1024
N_BRANCH = 3

N_MEM = 256
MEM_HEADS = 4
MEM_HEAD_DIM = 128
MEM_WIDTH = MEM_HEADS * MEM_HEAD_DIM

MOE_GROUPS = 4
MOE_PER_GROUP = 8
N_EXPERTS = 32
MOE_TOP_K = 2
EXPERT_FF = 512

N_PROMPT = BATCH * SEQ
N_SAMPLE = DEC_BATCH * DEC_SEQ
N_ROWS = N_PROMPT + N_SAMPLE

COL_GMIX = 0
COL_S5 = 6144
COL_RV = 7168
COL_RG = 8192
COL_QLAT = 9216
COL_KVLAT = 9728
COL_RQ = 10240
COL_RK = 10752
PROJ_COLS = 11264
IN_S5 = (0, 1024)
IN_QLAT = (1024, 1536)
IN_KVLAT = (1536, 2048)
IN_KPE = (2048, 2112)
IN_RQ = (2112, 2624)
IN_RK = (2624, 3136)
IN_RV = (3136, 4160)
IN_RG = (4160, 5184)
IN_GMIX = (5184, 11328)

LANES = 128
EXPERT_ROWS = 256
N_PAIRS = N_ROWS * MOE_TOP_K
N_EXPERT_BLOCKS = (N_PAIRS + N_EXPERTS * (EXPERT_ROWS - 1)) // EXPERT_ROWS + 1
VMEM_LIMIT = 56 * 1024 * 1024


def _cparams(sem):
    return pltpu.CompilerParams(dimension_semantics=sem, vmem_limit_bytes=VMEM_LIMIT)


def _rms_rows(x, g):
    r = lax.rsqrt(jnp.mean(x * x, axis=-1, keepdims=True) + RMS_EPS)
    return (x * r) * g


def _dot(a, b):
    return jnp.dot(a, b, preferred_element_type=F32)


def _rows_call(kernel_fn, into, *, in_specs, args, **kw):
    if into is None:
        out0 = kw['out_shape'][0] if isinstance(kw['out_shape'], (list, tuple)) else kw['out_shape']
        into = jnp.zeros(out0.shape, out0.dtype)

    def aliased_kernel(into_ref, *refs):
        del into_ref
        kernel_fn(*refs)

    return pl.pallas_call(aliased_kernel, in_specs=[pl.BlockSpec(memory_space=pl.ANY)] + in_specs,
                          input_output_aliases={0: 0}, **kw)(into, *args)


def _dot_nt(a, b):
    return lax.dot_general(a, b, (((1,), (1,)), ((), ())), preferred_element_type=F32)


def _rms_proj_kernel(x_ref, g_ref, w_ref, *refs, side):
    if side:
        w2_ref, o_ref, o2_ref, xn_ref = refs
    else:
        o_ref, xn_ref = refs

    @pl.when(pl.program_id(1) == 0)
    def _():
        xn_ref[...] = _rms_rows(x_ref[...], g_ref[...]).astype(BF16)
        if side:
            o2_ref[...] = _dot(xn_ref[...], w2_ref[...]).astype(o2_ref.dtype)

    o_ref[...] = _dot(xn_ref[...], w_ref[...]).astype(o_ref.dtype)


def _rms_proj(x, g, w, out_dtype, tm, tn, w2=None, out2_dtype=None):
    n, k = x.shape
    cols = w.shape[1]
    side = w2 is not None
    in_specs = [
        pl.BlockSpec((tm, k), lambda i, j: (i, 0)),
        pl.BlockSpec((1, k), lambda i, j: (0, 0)),
        pl.BlockSpec((k, tn), lambda i, j: (0, j)),
    ]
    out_specs = [pl.BlockSpec((tm, tn), lambda i, j: (i, j))]
    out_shape = [jax.ShapeDtypeStruct((n, cols), out_dtype)]
    args = [x, g.reshape(1, k), w]
    if side:
        cols2 = w2.shape[1]
        in_specs.append(pl.BlockSpec((k, cols2), lambda i, j: (0, 0)))
        out_specs.append(pl.BlockSpec((tm, cols2), lambda i, j: (i, 0)))
        out_shape.append(jax.ShapeDtypeStruct((n, cols2), out2_dtype))
        args.append(w2)
    return pl.pallas_call(
        functools.partial(_rms_proj_kernel, side=side),
        grid=(n // tm, cols // tn),
        in_specs=in_specs,
        out_specs=out_specs,
        out_shape=out_shape,
        scratch_shapes=[pltpu.VMEM((tm, k), BF16)],
        compiler_params=_cparams(("parallel", "arbitrary")),
        name="rms_proj_side" if side else "rms_proj",
    )(*args)


def _res_matmul_kernel(a_ref, w_ref, r_ref, o_ref):
    o_ref[...] = r_ref[...] + _dot(a_ref[...], w_ref[...])


def _res_matmul(a, w, res, tm, tn):
    n, k = a.shape
    cols = w.shape[1]
    return pl.pallas_call(
        _res_matmul_kernel,
        grid=(n // tm, cols // tn),
        in_specs=[
            pl.BlockSpec((tm, k), lambda i, j: (i, 0)),
            pl.BlockSpec((k, tn), lambda i, j: (0, j)),
            pl.BlockSpec((tm, tn), lambda i, j: (i, j)),
        ],
        out_specs=pl.BlockSpec((tm, tn), lambda i, j: (i, j)),
        out_shape=jax.ShapeDtypeStruct((n, cols), F32),
        compiler_params=_cparams(("parallel", "arbitrary")),
        name="res_matmul",
    )(a, w, res)


S5_K_SLAB = 256
S5_N_SLAB = 1024
S5_SLABS = S5_WIDTH // S5_K_SLAB
S5_SCAN_COLS = 1024
S5_SCAN_ROWS = 8


def _gelu_tanh(x):
    return 0.5 * x * (1.0 + jnp.tanh(math.sqrt(2.0 / math.pi) * (x + 0.044715 * (x * x * x))))


def _sigmoid(x):
    return 1.0 / (1.0 + jnp.exp(-x))


def _s5_kernel(u_ref, bre_ref, bim_ref, cre_ref, cim_ref, pw_ref, d_ref, wglu_ref,
               h0r_ref, h0i_ref, o_ref, hr_out_ref, hi_out_ref, sre_ref, sim_ref, cr_ref, ci_ref, *, tt):
    @pl.when(pl.program_id(1) == 0)
    def _():
        cr_ref[...] = h0r_ref[0]
        ci_ref[...] = h0i_ref[0]

    u = u_ref[...]
    for n in range(S5_SLABS):
        un = u[:, S5_K_SLAB * n:S5_K_SLAB * (n + 1)]
        sre_ref[:, S5_N_SLAB * n:S5_N_SLAB * (n + 1)] = _dot(un, bre_ref[n])
        sim_ref[:, S5_N_SLAB * n:S5_N_SLAB * (n + 1)] = _dot(un, bim_ref[n])

    for c in range(S5_COLS // S5_SCAN_COLS):
        sl = slice(c * S5_SCAN_COLS, (c + 1) * S5_SCAN_COLS)

        def body(blk, carry, sl=sl):
            cr, ci = carry
            rows = pl.ds(pl.multiple_of(blk * S5_SCAN_ROWS, S5_SCAN_ROWS), S5_SCAN_ROWS)
            xr = sre_ref[rows, sl]
            xi = sim_ref[rows, sl]
            for j in range(3):
                kr = pw_ref[j, 0, :, sl]
                ki = pw_ref[j, 1, :, sl]
                sr = pltpu.roll(xr, 1 << j, 0)
                si = pltpu.roll(xi, 1 << j, 0)
                xr, xi = xr + (kr * sr - ki * si), xi + (kr * si + ki * sr)
            pr = pw_ref[3, 0, :, sl]
            pi = pw_ref[3, 1, :, sl]
            hr = xr + (pr * cr - pi * ci)
            hi = xi + (pr * ci + pi * cr)
            sre_ref[rows, sl] = hr
            sim_ref[rows, sl] = hi
            last = S5_SCAN_ROWS - 1
            return (jnp.broadcast_to(hr[last:last + 1], hr.shape), jnp.broadcast_to(hi[last:last + 1], hi.shape))

        init = (jnp.broadcast_to(cr_ref[:, sl], (S5_SCAN_ROWS, S5_SCAN_COLS)),
                jnp.broadcast_to(ci_ref[:, sl], (S5_SCAN_ROWS, S5_SCAN_COLS)))
        hr, hi = lax.fori_loop(0, tt // S5_SCAN_ROWS, body, init)
        cr_ref[:, sl] = hr[0:1]
        ci_ref[:, sl] = hi[0:1]

    ys = []
    for n in range(S5_SLABS):
        hr_n = sre_ref[:, S5_N_SLAB * n:S5_N_SLAB * (n + 1)].astype(BF16)
        hi_n = sim_ref[:, S5_N_SLAB * n:S5_N_SLAB * (n + 1)].astype(BF16)
        ys.append(_dot(hr_n, cre_ref[n]) + _dot(hi_n, cim_ref[n]))
    y = jnp.concatenate(ys, axis=1) + d_ref[...] * u.astype(F32)
    z = _gelu_tanh(y)
    gate = _sigmoid(_dot(z.astype(BF16), wglu_ref[...]))
    o_ref[...] = (z * gate).astype(o_ref.dtype)
    hr_out_ref[0] = cr_ref[...]
    hi_out_ref[0] = ci_ref[...]


def _s5_tables(lp):
    a_re, a_im = lp['s5_a_re'], lp['s5_a_im']
    dt = jnp.exp(lp['s5_log_dt'])[:, None]
    mag = jnp.exp(a_re * dt)
    lb_re = mag * jnp.cos(a_im * dt)
    lb_im = mag * jnp.sin(a_im * dt)
    den = a_re * a_re + a_im * a_im
    n_re = lb_re - 1.0
    co_re = (n_re * a_re + lb_im * a_im) / den
    co_im = (lb_im * a_re - n_re * a_im) / den
    bb_re = co_re[..., None] * lp['s5_b_re'] - co_im[..., None] * lp['s5_b_im']
    bb_im = co_re[..., None] * lp['s5_b_im'] + co_im[..., None] * lp['s5_b_re']
    per_slab = S5_GROUPS // S5_SLABS
    eye = jnp.eye(per_slab, dtype=F32)

    def b_tiles(bb):
        bb = bb.reshape(S5_SLABS, per_slab, S5_STATE, S5_GROUP_CH)
        return (bb.transpose(0, 1, 3, 2)[:, :, :, None, :] * eye[None, :, None, :, None]).reshape(
            S5_SLABS, S5_K_SLAB, S5_N_SLAB).astype(BF16)

    def c_tiles(cc):
        cc = cc.reshape(S5_SLABS, per_slab, S5_GROUP_CH, S5_STATE)
        return (cc.transpose(0, 1, 3, 2)[:, :, :, None, :] * eye[None, :, None, :, None]).reshape(
            S5_SLABS, S5_N_SLAB, S5_K_SLAB).astype(BF16)

    def lam_pow(k):
        m = jnp.exp(k * (a_re * dt).reshape(1, S5_COLS))
        ang = k * (a_im * dt).reshape(1, S5_COLS)
        return jnp.stack([m * jnp.cos(ang), m * jnp.sin(ang)])

    t = jnp.arange(S5_SCAN_ROWS, dtype=F32)[:, None]
    steps = [jnp.where(t >= float(1 << j), lam_pow(jnp.full_like(t, float(1 << j))), 0.0) for j in range(3)]
    pw = jnp.stack(steps + [lam_pow(t + 1.0)])

    return dict(
        bre=b_tiles(bb_re), bim=b_tiles(bb_im),
        cre=c_tiles(lp['s5_c_re']), cim=c_tiles(-lp['s5_c_im']),
        pw=pw, d=lp['s5_d'].reshape(1, S5_WIDTH),
    )


def _s5_branch(proj, tabs, wglu, h0_re, h0_im, n_seq, seq_len, tt, row0, into=None):
    nt = seq_len // tt
    base = row0 // tt
    col = COL_S5 // S5_WIDTH
    const3 = lambda s, t: (0, 0, 0)
    const2 = lambda s, t: (0, 0)
    out, hr, hi = _rows_call(
        functools.partial(_s5_kernel, tt=tt), into,
        grid=(n_seq, nt),
        in_specs=[
            pl.BlockSpec((tt, S5_WIDTH), lambda s, t: (base + s * nt + t, col)),
            pl.BlockSpec((S5_SLABS, S5_K_SLAB, S5_N_SLAB), const3),
            pl.BlockSpec((S5_SLABS, S5_K_SLAB, S5_N_SLAB), const3),
            pl.BlockSpec((S5_SLABS, S5_N_SLAB, S5_K_SLAB), const3),
            pl.BlockSpec((S5_SLABS, S5_N_SLAB, S5_K_SLAB), const3),
            pl.BlockSpec((4, 2, S5_SCAN_ROWS, S5_COLS), lambda s, t: (0, 0, 0, 0)),
            pl.BlockSpec((1, S5_WIDTH), const2),
            pl.BlockSpec((S5_WIDTH, S5_WIDTH), const2),
            pl.BlockSpec((1, 1, S5_COLS), lambda s, t: (s, 0, 0)),
            pl.BlockSpec((1, 1, S5_COLS), lambda s, t: (s, 0, 0)),
        ],
        out_specs=[
            pl.BlockSpec((tt, S5_WIDTH), lambda s, t: (base + s * nt + t, 0)),
            pl.BlockSpec((1, 1, S5_COLS), lambda s, t: (s, 0, 0)),
            pl.BlockSpec((1, 1, S5_COLS), lambda s, t: (s, 0, 0)),
        ],
        out_shape=[
            jax.ShapeDtypeStruct((N_ROWS, S5_WIDTH), BF16),
            jax.ShapeDtypeStruct((n_seq, 1, S5_COLS), F32),
            jax.ShapeDtypeStruct((n_seq, 1, S5_COLS), F32),
        ],
        scratch_shapes=[
            pltpu.VMEM((tt, S5_COLS), F32), pltpu.VMEM((tt, S5_COLS), F32),
            pltpu.VMEM((1, S5_COLS), F32), pltpu.VMEM((1, S5_COLS), F32),
        ],
        compiler_params=_cparams(("parallel", "arbitrary")),
        name="s5_branch",
        args=(proj, tabs['bre'], tabs['bim'], tabs['cre'], tabs['cim'], tabs['pw'], tabs['d'], wglu,
              h0_re.reshape(n_seq, 1, S5_COLS), h0_im.reshape(n_seq, 1, S5_COLS)),
    )
    return (out, hr.reshape(n_seq, S5_GROUPS, S5_STATE), hi.reshape(n_seq, S5_GROUPS, S5_STATE))


def _rope_tables(pos, d):
    inv_freq = ROPE_THETA ** (-jnp.arange(0, d, 2, dtype=F32) / d)
    ang = pos.astype(F32)[:, None] * inv_freq[None, :]
    cos, sin = jnp.cos(ang), jnp.sin(ang)
    pad = jnp.zeros((pos.shape[0], LANES - d), F32)
    return (jnp.concatenate([cos, cos, pad], axis=1), jnp.concatenate([-sin, sin, pad], axis=1))


def _pad_head_vec(g):
    return jnp.concatenate([g, jnp.zeros((MLA_HEAD_PAD - MLA_QK,), F32)]).reshape(1, MLA_HEAD_PAD)


def _qk_head(nope, pe, gain, cos2, sin2, scale):
    ss = jnp.sum(nope * nope, axis=-1, keepdims=True) + jnp.sum(pe * pe, axis=-1, keepdims=True)
    r = lax.rsqrt(ss * (1.0 / MLA_QK) + RMS_EPS)
    nope = (nope * r) * gain[:, :MLA_NOPE]
    pe = (pe * r) * gain[:, MLA_NOPE:]
    lane = lax.broadcasted_iota(jnp.int32, pe.shape, 1)
    half = MLA_ROPE // 2
    swap = jnp.where(lane < half, pltpu.roll(pe, LANES - half, 1), pltpu.roll(pe, half, 1))
    pe = pe * cos2 + swap * sin2
    return nope * scale, pe * scale


def _mla_q_kernel(lat_ref, g_ref, w_ref, gain_ref, cos_ref, sin_ref, o_ref):
    xn = _rms_rows(lat_ref[...].astype(F32), g_ref[...]).astype(BF16)
    q = _dot(xn, w_ref[...])
    gain = gain_ref[...]
    cos2 = cos_ref[...]
    sin2 = sin_ref[...]
    for h in range(MLA_HEADS):
        c0 = h * MLA_HEAD_PAD
        nope, pe = _qk_head(q[:, c0:c0 + MLA_NOPE], q[:, c0 + MLA_NOPE:c0 + MLA_HEAD_PAD],
                            gain, cos2, sin2, MLA_QK ** -0.5)
        o_ref[:, c0:c0 + MLA_NOPE] = nope.astype(BF16)
        o_ref[:, c0 + MLA_NOPE:c0 + MLA_HEAD_PAD] = pe.astype(BF16)


def _mla_q(proj, g, w_uq_pad, gain_pad, cos2, sin2, tab_block, tm):
    n = proj.shape[0]
    col = COL_QLAT // MLA_Q_LORA
    c2 = lambda i: (0, 0)
    tab = lambda i: (tab_block(i), 0)
    return pl.pallas_call(
        _mla_q_kernel,
        grid=(n // tm,),
        in_specs=[
            pl.BlockSpec((tm, MLA_Q_LORA), lambda i: (i, col)),
            pl.BlockSpec((1, MLA_Q_LORA), c2),
            pl.BlockSpec((MLA_Q_LORA, MLA_QK_PAD), c2),
            pl.BlockSpec((1, MLA_HEAD_PAD), c2),
            pl.BlockSpec((tm, LANES), tab),
            pl.BlockSpec((tm, LANES), tab),
        ],
        out_specs=pl.BlockSpec((tm, MLA_QK_PAD), lambda i: (i, 0)),
        out_shape=jax.ShapeDtypeStruct((n, MLA_QK_PAD), BF16),
        compiler_params=_cparams(("parallel",)),
        name="mla_q",
    )(proj, g.reshape(1, -1), w_uq_pad, gain_pad, cos2, sin2)


def _mla_kv_kernel(lat_ref, g_ref, w_ref, gain_ref, kpe_ref, cos_ref, sin_ref, ckv_ref, k_ref, v_ref):
    lat = _rms_rows(lat_ref[...].astype(F32), g_ref[...])
    ckv_ref[...] = lat
    kv = _dot(lat.astype(BF16), w_ref[...])
    gain = gain_ref[...]
    kpe = kpe_ref[...]
    cos2 = cos_ref[...]
    sin2 = sin_ref[...]
    for h in range(MLA_HEADS):
        c0 = h * (MLA_NOPE + MLA_V)
        nope, pe = _qk_head(kv[:, c0:c0 + MLA_NOPE], kpe, gain, cos2, sin2, 1.0)
        k0 = h * MLA_HEAD_PAD
        k_ref[:, k0:k0 + MLA_NOPE] = nope.astype(BF16)
        k_ref[:, k0 + MLA_NOPE:k0 + MLA_HEAD_PAD] = pe.astype(BF16)
        v_ref[:, h * MLA_V:(h + 1) * MLA_V] = kv[:, c0 + MLA_NOPE:c0 + MLA_NOPE + MLA_V].astype(BF16)


def _mla_kv(proj, g, w_ukv, gain_pad, kpe_pad, cos2, sin2, tab_block, tm):
    n = proj.shape[0]
    c2 = lambda i: (0, 0)
    row = lambda i: (i, 0)
    tab = lambda i: (tab_block(i), 0)
    return pl.pallas_call(
        _mla_kv_kernel,
        grid=(n // tm,),
        in_specs=[
            pl.BlockSpec((tm, MLA_KV_LORA), lambda i: (i, COL_KVLAT // MLA_KV_LORA)),
            pl.BlockSpec((1, MLA_KV_LORA), c2),
            pl.BlockSpec((MLA_KV_LORA, MLA_HEADS * (MLA_NOPE + MLA_V)), c2),
            pl.BlockSpec((1, MLA_HEAD_PAD), c2),
            pl.BlockSpec((tm, LANES), row),
            pl.BlockSpec((tm, LANES), tab),
            pl.BlockSpec((tm, LANES), tab),
        ],
        out_specs=[pl.BlockSpec((tm, MLA_KV_LORA), row), pl.BlockSpec((tm, MLA_QK_PAD), row),
                   pl.BlockSpec((tm, MLA_HEADS * MLA_V), row)],
        out_shape=[jax.ShapeDtypeStruct((n, MLA_KV_LORA), F32), jax.ShapeDtypeStruct((n, MLA_QK_PAD), BF16),
                   jax.ShapeDtypeStruct((n, MLA_HEADS * MLA_V), BF16)],
        compiler_params=_cparams(("parallel",)),
        name="mla_kv",
    )(proj, g.reshape(1, -1), w_ukv, gain_pad, kpe_pad, cos2, sin2)


ATTN_HEADS_PER_STEP = 2


def _attn_prompt_kernel(q_ref, k_ref, v_ref, o_ref, *, tq):
    qi = pl.program_id(2)

    def key_tile(ki, state, diagonal):
        rows = pl.ds(pl.multiple_of(ki * tq, tq), tq)
        new_state = []
        for h in range(ATTN_HEADS_PER_STEP):
            qk = slice(h * MLA_HEAD_PAD, (h + 1) * MLA_HEAD_PAD)
            vs = slice(h * MLA_V, (h + 1) * MLA_V)
            s = _dot_nt(q_ref[:, qk], k_ref[rows, qk])
            if diagonal:
                row_chunk = lax.broadcasted_iota(jnp.int32, s.shape, 0) // CHUNK
                col_chunk = lax.broadcasted_iota(jnp.int32, s.shape, 1) // CHUNK
                s = jnp.where(col_chunk <= row_chunk, s, NEG_INF)
            m_old, l_old, acc_old = state[h]
            m_new = jnp.maximum(m_old, jnp.max(s, axis=-1, keepdims=True))
            alpha = jnp.exp(m_old - m_new)
            p = jnp.exp(s - m_new)
            l_new = alpha * l_old + jnp.sum(p, axis=-1, keepdims=True)
            acc_new = alpha * acc_old + _dot(p.astype(BF16), v_ref[rows, vs])
            new_state.append((m_new, l_new, acc_new))
        return tuple(new_state)

    init = tuple((jnp.full((tq, 1), NEG_INF, F32), jnp.zeros((tq, 1), F32), jnp.zeros((tq, MLA_V), F32))
                 for _ in range(ATTN_HEADS_PER_STEP))
    state = lax.fori_loop(0, qi, lambda ki, st: key_tile(ki, st, False), init)
    state = key_tile(qi, state, True)
    for h in range(ATTN_HEADS_PER_STEP):
        _, l_fin, acc_fin = state[h]
        o_ref[:, h * MLA_V:(h + 1) * MLA_V] = (acc_fin / l_fin).astype(o_ref.dtype)


def _attn_prompt(q, k, v, n_batch, seq_len, tq):
    assert tq % CHUNK == 0
    nq = seq_len // tq
    hp = ATTN_HEADS_PER_STEP
    return _rows_call(
        functools.partial(_attn_prompt_kernel, tq=tq), None,
        grid=(n_batch, MLA_HEADS // hp, nq),
        in_specs=[
            pl.BlockSpec((tq, hp * MLA_HEAD_PAD), lambda b, h, qi: (b * nq + qi, h)),
            pl.BlockSpec((seq_len, hp * MLA_HEAD_PAD), lambda b, h, qi: (b, h)),
            pl.BlockSpec((seq_len, hp * MLA_V), lambda b, h, qi: (b, h)),
        ],
        out_specs=pl.BlockSpec((tq, hp * MLA_V), lambda b, h, qi: (b * nq + qi, h)),
        out_shape=jax.ShapeDtypeStruct((N_ROWS, MLA_HEADS * MLA_V), BF16),
        compiler_params=_cparams(("parallel", "parallel", "arbitrary")),
        name="attn_prompt",
        args=(q, k, v),
    )


def _attn_sample_kernel(q_ref, kn_ref, vn_ref, c_ref, kpe_ref, cos_ref, sin_ref, wk_ref, wv_ref, gain_ref, sel_ref,
                        o_ref):
    c = c_ref[...].astype(BF16)
    k_raw = _dot(c, wk_ref[...])
    v_all = _dot(c, wv_ref[...]).astype(BF16)
    gain = gain_ref[...]
    gain_n = gain[:, :MLA_NOPE]
    kpe = kpe_ref[...]
    pe = kpe * gain[:, MLA_NOPE:]
    lane = lax.broadcasted_iota(jnp.int32, pe.shape, 1)
    half = MLA_ROPE // 2
    swap = jnp.where(lane < half, pltpu.roll(pe, LANES - half, 1), pltpu.roll(pe, half, 1))
    pe = (pe * cos_ref[...] + swap * sin_ref[...]).astype(BF16)
    ss = (_dot_nt(sel_ref[...], (k_raw * k_raw).astype(BF16))
          + _dot_nt(jnp.ones((MLA_HEADS, LANES), BF16), (kpe * kpe).astype(BF16)))
    r = lax.rsqrt(ss * (1.0 / MLA_QK) + RMS_EPS)
    k_raw = k_raw.astype(BF16)
    for h in range(MLA_HEADS):
        ks = slice(h * MLA_HEAD_PAD, (h + 1) * MLA_HEAD_PAD)
        vs = slice(h * MLA_V, (h + 1) * MLA_V)
        q = q_ref[:, ks]
        q_n = (q[:, :MLA_NOPE].astype(F32) * gain_n).astype(BF16)
        s_p = (_dot_nt(q_n, k_raw[:, vs]) + _dot_nt(q[:, MLA_NOPE:], pe)) * r[h:h + 1]
        s_n = _dot_nt(q, kn_ref[:, ks])
        m = jnp.maximum(jnp.max(s_p, axis=-1, keepdims=True), jnp.max(s_n, axis=-1, keepdims=True))
        p_p = jnp.exp(s_p - m)
        p_n = jnp.exp(s_n - m)
        l = jnp.sum(p_p, axis=-1, keepdims=True) + jnp.sum(p_n, axis=-1, keepdims=True)
        o = _dot(p_p.astype(BF16), v_all[:, vs]) + _dot(p_n.astype(BF16), vn_ref[:, vs])
        o_ref[:, vs] = (o / l).astype(o_ref.dtype)


def _attn_sample(q, k_new, v_new, ckv_all, kpe_all, past0, cos, sin, w_uk, w_uv, gain_pad, row0, into):
    assert (PAST_LEN + DEC_SEQ - 1) // CHUNK <= PAST_LEN // CHUNK
    base = row0 // DEC_SEQ
    new = lambda b: (base + b, 0)
    past = lambda b: (past0 + b, 0)
    c2 = lambda b: (0, 0)
    width = MLA_HEADS * MLA_NOPE
    sel = jnp.repeat(jnp.eye(MLA_HEADS, dtype=F32), MLA_NOPE, axis=1).astype(BF16)
    return _rows_call(
        _attn_sample_kernel, into,
        grid=(DEC_BATCH,),
        in_specs=[
            pl.BlockSpec((DEC_SEQ, MLA_QK_PAD), new),
            pl.BlockSpec((DEC_SEQ, MLA_QK_PAD), new),
            pl.BlockSpec((DEC_SEQ, MLA_HEADS * MLA_V), new),
            pl.BlockSpec((PAST_LEN, MLA_KV_LORA), past),
            pl.BlockSpec((PAST_LEN, LANES), past),
            pl.BlockSpec((PAST_LEN, LANES), c2),
            pl.BlockSpec((PAST_LEN, LANES), c2),
            pl.BlockSpec((MLA_KV_LORA, width), c2),
            pl.BlockSpec((MLA_KV_LORA, MLA_HEADS * MLA_V), c2),
            pl.BlockSpec((1, MLA_HEAD_PAD), c2),
            pl.BlockSpec((MLA_HEADS, width), c2),
        ],
        out_specs=pl.BlockSpec((DEC_SEQ, MLA_HEADS * MLA_V), new),
        out_shape=jax.ShapeDtypeStruct((N_ROWS, MLA_HEADS * MLA_V), BF16),
        compiler_params=_cparams(("parallel",)),
        name="attn_sample",
        args=(q, k_new, v_new, ckv_all, kpe_all, cos, sin, w_uk, w_uv, gain_pad, sel),
    )


def _retention_tables(block):
    log_g = jnp.log1p(-jnp.exp2(-5.0 - jnp.arange(RET_HEADS, dtype=F32)))
    idx = jnp.arange(block, dtype=F32)
    diff = idx[:, None] - idx[None, :]
    decay = jnp.where(diff >= 0, jnp.exp(log_g[:, None, None] * jnp.maximum(diff, 0.0)), 0.0)
    q_decay = jnp.exp(log_g[:, None] * (idx + 1.0))
    k_decay = jnp.exp(log_g[:, None] * (block - 1.0 - idx))
    blk_decay = jnp.exp(log_g * block)
    return (decay,
            jnp.broadcast_to(q_decay[:, :, None], (RET_HEADS, block, RET_DK)),
            jnp.broadcast_to(k_decay[:, :, None], (RET_HEADS, block, RET_DK)),
            jnp.broadcast_to(blk_decay[:, None, None], (RET_HEADS, 1, RET_DV)))


def _rope128(x, cos, sin):
    return x * cos + pltpu.roll(x, RET_DK // 2, 1) * sin


def _retention_kernel(rq_ref, rk_ref, rv_ref, rg_ref, cos_ref, sin_ref, dec_ref, qd_ref, kd_ref, bd_ref, s0_ref,
                      o_ref, s_out_ref, s_ref):
    @pl.when(pl.program_id(1) == 0)
    def _():
        s_ref[...] = s0_ref[0]

    cos = cos_ref[...]
    sin = sin_ref[...]
    for h in range(RET_HEADS):
        ks = slice(h * RET_DK, (h + 1) * RET_DK)
        vs = slice(h * RET_DV, (h + 1) * RET_DV)
        q = _rope128(rq_ref[:, ks].astype(F32), cos, sin) * (RET_DK ** -0.5)
        k = _rope128(rk_ref[:, ks].astype(F32), cos, sin)
        v = rv_ref[:, vs]
        s_old = s_ref[h]
        att = _dot_nt(q.astype(BF16), k.astype(BF16)) * dec_ref[h]
        o = _dot(att.astype(BF16), v) + _dot((q * qd_ref[h]).astype(BF16), s_old.astype(BF16))
        kd = (k * kd_ref[h]).astype(BF16)
        s_new = bd_ref[h] * s_old + lax.dot_general(kd, v, (((0,), (0,)), ((), ())), preferred_element_type=F32)
        s_ref[h] = s_new
        s_out_ref[0, h] = s_new
        mu = jnp.mean(o, axis=-1, keepdims=True)
        c = o - mu
        o = c * lax.rsqrt(jnp.mean(c * c, axis=-1, keepdims=True) + RMS_EPS)
        g = rg_ref[:, vs].astype(F32)
        o_ref[:, vs] = (o * (g * _sigmoid(g))).astype(o_ref.dtype)


def _retention_branch(proj, cos, sin, tab_block, s0, s0_base, n_seq, seq_len, block, row0, into=None):
    nblk = seq_len // block
    base = row0 // block
    dec, qd, kd, bd = _retention_tables(block)
    rows = lambda b, t: base + b * nblk + t
    tab = lambda b, t: (tab_block(t), 0)
    all3 = lambda b, t: (0, 0, 0)
    qk_w = RET_HEADS * RET_DK
    v_w = RET_HEADS * RET_DV
    out, s_last = _rows_call(
        _retention_kernel, into,
        grid=(n_seq, nblk),
        in_specs=[
            pl.BlockSpec((block, qk_w), lambda b, t: (rows(b, t), COL_RQ // qk_w)),
            pl.BlockSpec((block, qk_w), lambda b, t: (rows(b, t), COL_RK // qk_w)),
            pl.BlockSpec((block, v_w), lambda b, t: (rows(b, t), COL_RV // v_w)),
            pl.BlockSpec((block, v_w), lambda b, t: (rows(b, t), COL_RG // v_w)),
            pl.BlockSpec((block, LANES), tab),
            pl.BlockSpec((block, LANES), tab),
            pl.BlockSpec((RET_HEADS, block, block), all3),
            pl.BlockSpec((RET_HEADS, block, RET_DK), all3),
            pl.BlockSpec((RET_HEADS, block, RET_DK), all3),
            pl.BlockSpec((RET_HEADS, 1, RET_DV), all3),
            pl.BlockSpec((1, RET_HEADS, RET_DK, RET_DV), lambda b, t: (s0_base + b, 0, 0, 0)),
        ],
        out_specs=[
            pl.BlockSpec((block, v_w), lambda b, t: (rows(b, t), 0)),
            pl.BlockSpec((1, RET_HEADS, RET_DK, RET_DV), lambda b, t: (b, 0, 0, 0)),
        ],
        out_shape=[
            jax.ShapeDtypeStruct((N_ROWS, v_w), BF16),
            jax.ShapeDtypeStruct((n_seq, RET_HEADS, RET_DK, RET_DV), F32),
        ],
        scratch_shapes=[pltpu.VMEM((RET_HEADS, RET_DK, RET_DV), F32)],
        compiler_params=_cparams(("parallel", "arbitrary")),
        name="retention",
        args=(proj, proj, proj, proj, cos, sin, dec, qd, kd, bd, s0),
    )
    return out, s_last


def _merge_kernel(a_ref, b_ref, c_ref, w_ref, g0_ref, g1_ref, g2_ref, o_ref):
    acc = _sigmoid(g0_ref[...].astype(F32)) * _dot(a_ref[...], w_ref[0])
    acc += _sigmoid(g1_ref[...].astype(F32)) * _dot(b_ref[...], w_ref[1])
    acc += _sigmoid(g2_ref[...].astype(F32)) * _dot(c_ref[...], w_ref[2])
    o_ref[...] = acc.astype(o_ref.dtype)


def _merge(o_a, o_b, o_c, w_branch, proj, tm, tn):
    n = o_a.shape[0]
    nj = D_MODEL // tn
    br = pl.BlockSpec((tm, BRANCH_WIDTH), lambda i, j: (i, 0))
    gate = lambda b: pl.BlockSpec((tm, tn), lambda i, j: (i, COL_GMIX // tn + b * nj + j))
    return pl.pallas_call(
        _merge_kernel,
        grid=(n // tm, nj),
        in_specs=[br, br, br,
                  pl.BlockSpec((N_BRANCH, BRANCH_WIDTH, tn), lambda i, j: (0, 0, j)),
                  gate(0), gate(1), gate(2)],
        out_specs=pl.BlockSpec((tm, tn), lambda i, j: (i, j)),
        out_shape=jax.ShapeDtypeStruct((n, D_MODEL), BF16),
        compiler_params=_cparams(("parallel", "arbitrary")),
        name="branch_merge",
    )(o_a, o_b, o_c, w_branch, proj, proj, proj)


def _memkv_kernel(m_ref, g_ref, wk_ref, wv_ref, kg_ref, k_ref, v_ref):
    mn = _rms_rows(m_ref[...], g_ref[...]).astype(BF16)
    kk = _dot(mn, wk_ref[...])
    for h in range(MEM_HEADS):
        sl = slice(h * MEM_HEAD_DIM, (h + 1) * MEM_HEAD_DIM)
        k_ref[:, sl] = _rms_rows(kk[:, sl], kg_ref[...])
    v_ref[...] = _dot(mn, wv_ref[...])


def _memkv(mem, g, wk, wv, k_gain, tm):
    n, k = mem.shape
    c2 = lambda i: (0, 0)
    return pl.pallas_call(
        _memkv_kernel,
        grid=(n // tm,),
        in_specs=[
            pl.BlockSpec((tm, k), lambda i: (i, 0)),
            pl.BlockSpec((1, k), c2),
            pl.BlockSpec((k, MEM_WIDTH), c2),
            pl.BlockSpec((k, MEM_WIDTH), c2),
            pl.BlockSpec((1, MEM_HEAD_DIM), c2),
        ],
        out_specs=[pl.BlockSpec((tm, MEM_WIDTH), lambda i: (i, 0)), pl.BlockSpec((tm, MEM_WIDTH), lambda i: (i, 0))],
        out_shape=[jax.ShapeDtypeStruct((n, MEM_WIDTH), F32), jax.ShapeDtypeStruct((n, MEM_WIDTH), F32)],
        compiler_params=_cparams(("parallel",)),
        name="memory_kv",
    )(mem, g.reshape(1, k), wk, wv, k_gain.reshape(1, MEM_HEAD_DIM))


def _cross_attn_kernel(q_ref, k_ref, v_ref, qg_ref, o_ref, *, head_axis):
    for h in range(MEM_HEADS):
        sl = slice(h * MEM_HEAD_DIM, (h + 1) * MEM_HEAD_DIM)
        if head_axis:
            k, v = k_ref[0, :, h, :], v_ref[0, :, h, :]
        else:
            k, v = k_ref[0, :, sl], v_ref[0, :, sl]
        q = _rms_rows(q_ref[:, sl].astype(F32), qg_ref[...]) * (MEM_HEAD_DIM ** -0.5)
        s = _dot_nt(q.astype(BF16), k.astype(BF16))
        p = jnp.exp(s - jnp.max(s, axis=-1, keepdims=True))
        l = jnp.sum(p, axis=-1, keepdims=True)
        o = _dot(p.astype(BF16), v.astype(BF16))
        o_ref[:, sl] = (o / l).astype(o_ref.dtype)


def _cross_attn(qc, mem_k, mem_v, mem0, q_gain, n_batch, rows_per_batch, tm, row0, into=None):
    tiles = rows_per_batch // tm
    base = row0 // tm
    head_axis = mem_k.ndim == 4
    if head_axis:
        mem_spec = pl.BlockSpec((1, N_MEM, MEM_HEADS, MEM_HEAD_DIM), lambda b, t: (mem0 + b, 0, 0, 0))
    else:
        mem_spec = pl.BlockSpec((1, N_MEM, MEM_WIDTH), lambda b, t: (mem0 + b, 0, 0))
    return _rows_call(
        functools.partial(_cross_attn_kernel, head_axis=head_axis), into,
        grid=(n_batch, tiles),
        in_specs=[
            pl.BlockSpec((tm, MEM_WIDTH), lambda b, t: (base + b * tiles + t, 0)),
            mem_spec, mem_spec,
            pl.BlockSpec((1, MEM_HEAD_DIM), lambda b, t: (0, 0)),
        ],
        out_specs=pl.BlockSpec((tm, MEM_WIDTH), lambda b, t: (base + b * tiles + t, 0)),
        out_shape=jax.ShapeDtypeStruct((N_ROWS, MEM_WIDTH), BF16),
        compiler_params=_cparams(("parallel", "arbitrary")),
        name="cross_attn",
        args=(qc, mem_k, mem_v, q_gain.reshape(1, MEM_HEAD_DIM)),
    )


ROUTER_LANES = LANES
FAR_LANE = 4 * LANES


def _split_bf16(x):
    hi = x.astype(BF16)
    lo = (x - hi.astype(F32)).astype(BF16)
    return hi, lo


def _router_kernel(x_ref, g_ref, w_ref, b_ref, xn_ref, ids_ref, gates_ref):
    xn = _rms_rows(x_ref[...], g_ref[...])
    xn_ref[...] = xn
    xh, xl = _split_bf16(xn)
    wh, wl = _split_bf16(w_ref[...])
    logits = _dot(xh, wh) + (_dot(xh, wl) + _dot(xl, wh)) + b_ref[...]
    lane = lax.broadcasted_iota(jnp.int32, logits.shape, 1)

    def first_max(vals):
        top = jnp.max(vals, axis=-1, keepdims=True)
        idx = jnp.min(jnp.where(vals == top, lane, FAR_LANE), axis=-1, keepdims=True)
        return top, idx

    g_mask = jnp.logical_and(lane >= N_EXPERTS, lane < N_EXPERTS + MOE_GROUPS)
    g_top, g_lane = first_max(jnp.where(g_mask, logits, NEG_INF))
    p_group = 1.0 / jnp.sum(jnp.where(g_mask, jnp.exp(logits - g_top), 0.0), axis=-1, keepdims=True)
    g_sel = g_lane - N_EXPERTS
    e_mask = jnp.logical_and(lane < N_EXPERTS, lane // MOE_PER_GROUP == g_sel)
    e_vals = jnp.where(e_mask, logits, NEG_INF)
    e1_top, e1 = first_max(e_vals)
    e2_top, e2 = first_max(jnp.where(lane == e1, NEG_INF, e_vals))
    t = jnp.exp(e2_top - e1_top)
    gate1 = p_group / (1.0 + t)
    gate2 = p_group * t / (1.0 + t)
    ids_ref[...] = jnp.where(lane == 0, e1, jnp.where(lane == 1, e2, 0))
    gates_ref[...] = jnp.where(lane == 0, gate1, jnp.where(lane == 1, gate2, 0.0))


def _router(x, g, w_router, b_router, tm):
    n, k = x.shape
    c2 = lambda i: (0, 0)
    row = lambda i: (i, 0)
    return pl.pallas_call(
        _router_kernel,
        grid=(n // tm,),
        in_specs=[
            pl.BlockSpec((tm, k), row),
            pl.BlockSpec((1, k), c2),
            pl.BlockSpec((k, ROUTER_LANES), c2),
            pl.BlockSpec((1, ROUTER_LANES), c2),
        ],
        out_specs=[pl.BlockSpec((tm, k), row), pl.BlockSpec((tm, ROUTER_LANES), row),
                   pl.BlockSpec((tm, ROUTER_LANES), row)],
        out_shape=[jax.ShapeDtypeStruct((n, k), F32), jax.ShapeDtypeStruct((n, ROUTER_LANES), jnp.int32),
                   jax.ShapeDtypeStruct((n, ROUTER_LANES), F32)],
        compiler_params=_cparams(("parallel",)),
        name="moe_router",
    )(x, g.reshape(1, k), w_router, b_router)


def _route_plan(expert_idx):
    flat_e = expert_idx.reshape(N_PAIRS)
    onehot = (flat_e[:, None] == jnp.arange(N_EXPERTS, dtype=jnp.int32)[None, :]).astype(jnp.int32)
    csum = jnp.cumsum(onehot, axis=0)
    rank = jnp.take_along_axis(csum, flat_e[:, None], axis=1)[:, 0] - 1
    counts = csum[-1]
    padded = (counts + EXPERT_ROWS - 1) // EXPERT_ROWS * EXPERT_ROWS
    pad_ends = jnp.cumsum(padded)
    pad_starts = pad_ends - padded
    dest = (pad_starts[flat_e] + rank).astype(jnp.int32)
    n_used = (pad_ends[-1] // EXPERT_ROWS).astype(jnp.int32).reshape(1)
    block_start = jnp.arange(N_EXPERT_BLOCKS, dtype=jnp.int32) * EXPERT_ROWS
    block_e = jnp.sum((pad_ends[None, :] <= block_start[:, None]).astype(jnp.int32), axis=1)
    block_e = jnp.minimum(block_e, N_EXPERTS - 1)
    src = jnp.zeros((N_EXPERT_BLOCKS * EXPERT_ROWS,), jnp.int32).at[dest].set(
        jnp.arange(N_PAIRS, dtype=jnp.int32) // MOE_TOP_K)
    return dest, src, block_e, n_used


DMA_UNROLL = 8


def _experts_kernel(src_ref, be_ref, nu_ref, x_hbm, wgu_ref, wd_ref, o_ref, xbuf_ref, sem):
    i = pl.program_id(0)
    n_used = nu_ref[0]
    slot = lax.rem(i, 2)

    def row_copy(blk, buf, r):
        return pltpu.make_async_copy(x_hbm.at[pl.ds(src_ref[blk * EXPERT_ROWS + r], 1)],
                                     xbuf_ref.at[buf, pl.ds(r, 1)], sem.at[buf])

    def start_block(blk, buf):
        for r in range(EXPERT_ROWS):
            row_copy(blk, buf, r).start(priority=r % 2)

    def wait_block(blk, buf):
        for r in range(EXPERT_ROWS):
            row_copy(blk, buf, r).wait()

    @pl.when(jnp.logical_and(i == 0, n_used > 0))
    def _():
        start_block(0, 0)

    @pl.when(i < n_used)
    def _():
        wait_block(i, slot)
        start_block(i + 1, 1 - slot)
        gu = _dot(xbuf_ref[slot].astype(BF16), wgu_ref[0, 0].astype(BF16))
        g = gu[:, :EXPERT_FF]
        h = (g * _sigmoid(g)) * gu[:, EXPERT_FF:]
        o_ref[...] = _dot(h.astype(BF16), wd_ref[0, 0].astype(BF16))

    @pl.when(i == n_used)
    def _():
        wait_block(i, slot)

    @pl.when(i >= n_used)
    def _():
        o_ref[...] = jnp.zeros(o_ref.shape, o_ref.dtype)


def _experts(xn, src, block_e, n_used, w_gate_up, w_down, layer):
    return pl.pallas_call(
        _experts_kernel,
        grid_spec=pltpu.PrefetchScalarGridSpec(
            num_scalar_prefetch=3,
            grid=(N_EXPERT_BLOCKS,),
            in_specs=[
                pl.BlockSpec(memory_space=pl.ANY),
                pl.BlockSpec((1, 1, D_MODEL, 2 * EXPERT_FF), lambda i, src, be, nu: (layer, be[i], 0, 0)),
                pl.BlockSpec((1, 1, EXPERT_FF, D_MODEL), lambda i, src, be, nu: (layer, be[i], 0, 0)),
            ],
            out_specs=pl.BlockSpec((EXPERT_ROWS, D_MODEL), lambda i, src, be, nu: (i, 0)),
            scratch_shapes=[pltpu.VMEM((2, EXPERT_ROWS, D_MODEL), F32), pltpu.SemaphoreType.DMA((2,))],
        ),
        out_shape=jax.ShapeDtypeStruct((N_EXPERT_BLOCKS * EXPERT_ROWS, D_MODEL), F32),
        compiler_params=_cparams(("arbitrary",)),
        name="moe_experts",
    )(src, block_e, n_used, xn, w_gate_up, w_down)


def _combine_kernel(dest_ref, x_ref, gates_ref, y_hbm, o_ref, buf_ref, sem, *, rows):
    i = pl.program_id(0)
    slot = lax.rem(i, 2)
    per_iter = DMA_UNROLL // MOE_TOP_K

    def row_copy(tile, buf, r, k):
        return pltpu.make_async_copy(y_hbm.at[pl.ds(dest_ref[(tile * rows + r) * MOE_TOP_K + k], 1)],
                                     buf_ref.at[buf, k, pl.ds(r, 1)], sem.at[buf])

    def start_tile(tile, buf):
        def body(c, carry):
            for u in range(per_iter):
                for k in range(MOE_TOP_K):
                    row_copy(tile, buf, c * per_iter + u, k).start(priority=k)
            return carry
        lax.fori_loop(0, rows // per_iter, body, 0)

    def wait_tile(tile, buf):
        def body(c, carry):
            for u in range(per_iter):
                for k in range(MOE_TOP_K):
                    row_copy(tile, buf, c * per_iter + u, k).wait()
            return carry
        lax.fori_loop(0, rows // per_iter, body, 0)

    @pl.when(i == 0)
    def _():
        start_tile(0, 0)

    @pl.when(i + 1 < pl.num_programs(0))
    def _():
        start_tile(i + 1, 1 - slot)

    wait_tile(i, slot)
    gates = gates_ref[...]
    y = gates[:, 0:1] * buf_ref[slot, 0] + gates[:, 1:2] * buf_ref[slot, 1]
    o_ref[...] = x_ref[...] + y


def _combine(x, gates, y_sorted, dest, rows):
    n, width = x.shape
    return pl.pallas_call(
        functools.partial(_combine_kernel, rows=rows),
        grid_spec=pltpu.PrefetchScalarGridSpec(
            num_scalar_prefetch=1,
            grid=(n // rows,),
            in_specs=[
                pl.BlockSpec((rows, width), lambda i, d: (i, 0)),
                pl.BlockSpec((rows, ROUTER_LANES), lambda i, d: (i, 0)),
                pl.BlockSpec(memory_space=pl.ANY),
            ],
            out_specs=pl.BlockSpec((rows, width), lambda i, d: (i, 0)),
            scratch_shapes=[pltpu.VMEM((2, MOE_TOP_K, rows, width), F32), pltpu.SemaphoreType.DMA((2,))],
        ),
        out_shape=jax.ShapeDtypeStruct((n, width), F32),
        compiler_params=_cparams(("arbitrary",)),
        name="moe_combine",
    )(dest, x, gates, y_sorted)


def _hier_moe(x, lp, w_gate_up, w_down, layer):
    w_router = jnp.concatenate(
        [lp['w_expert_router'], lp['w_group_router'],
         jnp.zeros((D_MODEL, ROUTER_LANES - N_EXPERTS - MOE_GROUPS), F32)], axis=1)
    b_router = jnp.concatenate(
        [lp['b_expert'], lp['b_group'], jnp.zeros((ROUTER_LANES - N_EXPERTS - MOE_GROUPS,), F32)]).reshape(1, -1)
    xn, ids, gates = _router(x, lp['norm_ffn'], w_router, b_router, 512)
    dest, src, block_e, n_used = _route_plan(ids[:, :MOE_TOP_K])
    y_sorted = _experts(xn, src, block_e, n_used, w_gate_up, w_down, layer)
    return _combine(x, gates, y_sorted, dest, 256)


def _cols(w, *ranges):
    return jnp.concatenate([w[:, a:b] for a, b in ranges], axis=1)


def _prompt_or_sample_block(i, tm):
    return jnp.where(i < N_PROMPT // tm, lax.rem(i, SEQ // tm), SEQ // tm)


def _layer(x, mem_prompt, lp, layer, caches, tables):
    (ckv_all, kpe_past_all, h0_re, h0_im, ret_s0_all, mem_k_all, mem_v_all, w_gate_up, w_down) = caches
    cos_mla, sin_mla, cos_past, sin_past, cos_ret, sin_ret = tables
    tm = 512
    w_in = lp['w_in']
    w_main = _cols(w_in, IN_GMIX, IN_S5, IN_RV, IN_RG, IN_QLAT, IN_KVLAT, IN_RQ, IN_RK).astype(BF16)
    w_kpe = jnp.pad(_cols(w_in, IN_KPE), ((0, 0), (0, LANES - MLA_ROPE))).astype(BF16)
    proj, kpe = _rms_proj(x, lp['norm_mix'], w_main, BF16, 1024, 1024, w2=w_kpe, out2_dtype=F32)

    s5_tabs = _s5_tables(lp)
    w_glu = lp['s5_w_glu'].astype(BF16)
    zeros_s5 = jnp.zeros((BATCH, S5_GROUPS, S5_STATE), F32)
    a, p_s5_re, p_s5_im = _s5_branch(proj, s5_tabs, w_glu, zeros_s5, zeros_s5, BATCH, SEQ, 256, 0)
    a, s_s5_re, s_s5_im = _s5_branch(proj, s5_tabs, w_glu, h0_re, h0_im, DEC_BATCH, DEC_SEQ, DEC_SEQ, N_PROMPT,
                                     into=a)

    w_uq = lp['mla_w_uq'].reshape(MLA_Q_LORA, MLA_HEADS, MLA_QK)
    w_uq = jnp.pad(w_uq, ((0, 0), (0, 0), (0, MLA_HEAD_PAD - MLA_QK))).reshape(MLA_Q_LORA, MLA_QK_PAD).astype(BF16)
    w_ukv = lp['mla_w_ukv'].astype(BF16)
    q_gain = _pad_head_vec(lp['mla_q_gain'])
    k_gain = _pad_head_vec(lp['mla_k_gain'])
    row_tab = lambda i: _prompt_or_sample_block(i, tm)
    q = _mla_q(proj, lp['mla_q_norm'], w_uq, q_gain, cos_mla, sin_mla, row_tab, tm)
    c_kv, k_new, v_new = _mla_kv(proj, lp['mla_kv_norm'], w_ukv, k_gain, kpe, cos_mla, sin_mla, row_tab, tm)
    w_ukv_heads = w_ukv.reshape(MLA_KV_LORA, MLA_HEADS, MLA_NOPE + MLA_V)
    w_uk = w_ukv_heads[:, :, :MLA_NOPE].reshape(MLA_KV_LORA, MLA_HEADS * MLA_NOPE)
    w_uv = w_ukv_heads[:, :, MLA_NOPE:].reshape(MLA_KV_LORA, MLA_HEADS * MLA_V)
    b = _attn_prompt(q, k_new, v_new, BATCH, SEQ, 512)
    b = _attn_sample(q, k_new, v_new, ckv_all, kpe_past_all, layer * DEC_BATCH, cos_past, sin_past, w_uk, w_uv,
                     k_gain, N_PROMPT, into=b)

    zeros_ret = jnp.zeros((BATCH, RET_HEADS, RET_DK, RET_DV), F32)
    c, p_ret = _retention_branch(proj, cos_ret, sin_ret, lambda t: t, zeros_ret, 0, BATCH, SEQ, 256, 0)
    c, s_ret = _retention_branch(proj, cos_ret, sin_ret, lambda t: SEQ // DEC_SEQ, ret_s0_all, layer * DEC_BATCH,
                                 DEC_BATCH, DEC_SEQ, DEC_SEQ, N_PROMPT, into=c)

    merged = _merge(a, b, c, lp['w_branch'].astype(BF16), proj, 1024, 512)
    x = _res_matmul(merged, lp['w_out'].astype(BF16), x, 1024, 1024)

    mem_k_p, mem_v_p = _memkv(mem_prompt, lp['norm_mem'], lp['w_ck'].astype(BF16), lp['w_cv'].astype(BF16),
                              lp['cross_k_gain'], 256)
    qc = _rms_proj(x, lp['norm_cross'], lp['w_cq'].astype(BF16), BF16, 512, MEM_WIDTH)[0]
    o = _cross_attn(qc, mem_k_p.reshape(BATCH, N_MEM, MEM_WIDTH), mem_v_p.reshape(BATCH, N_MEM, MEM_WIDTH), 0,
                    lp['cross_q_gain'], BATCH, SEQ, 512, 0)
    o = _cross_attn(qc, mem_k_all, mem_v_all, layer * DEC_BATCH, lp['cross_q_gain'], DEC_BATCH, DEC_SEQ, DEC_SEQ,
                    N_PROMPT, into=o)
    x = _res_matmul(o, lp['w_co'].astype(BF16), x, 1024, 1024)

    x = _hier_moe(x, lp, w_gate_up, w_down, layer)

    kpe_rows = kpe[:, :MLA_ROPE]
    state_p = (c_kv[:N_PROMPT].reshape(BATCH, SEQ, MLA_KV_LORA), kpe_rows[:N_PROMPT].reshape(BATCH, SEQ, MLA_ROPE),
               p_s5_re, p_s5_im, p_ret,
               mem_k_p.reshape(BATCH, N_MEM, MEM_HEADS, MEM_HEAD_DIM),
               mem_v_p.reshape(BATCH, N_MEM, MEM_HEADS, MEM_HEAD_DIM))
    state_s = (c_kv[N_PROMPT:].reshape(DEC_BATCH, DEC_SEQ, MLA_KV_LORA),
               kpe_rows[N_PROMPT:].reshape(DEC_BATCH, DEC_SEQ, MLA_ROPE), s_s5_re, s_s5_im, s_ret)
    return x, state_p, state_s


_LAYER_PARAMS = (
    'norm_mix', 'w_in', 's5_a_re', 's5_a_im', 's5_log_dt', 's5_b_re', 's5_b_im', 's5_c_re', 's5_c_im', 's5_d',
    's5_w_glu', 'mla_q_norm', 'mla_w_uq', 'mla_kv_norm', 'mla_w_ukv', 'mla_q_gain', 'mla_k_gain', 'w_branch',
    'w_out', 'norm_cross', 'norm_mem', 'w_cq', 'w_ck', 'w_cv', 'cross_q_gain', 'cross_k_gain', 'w_co', 'norm_ffn',
    'w_group_router', 'b_group', 'w_expert_router', 'b_expert')


def kernel(x_prompt, x_sample, mem_prompt, cache_mla_ckv, cache_mla_kpe, state_s5_re, state_s5_im, state_ret,
           cache_mem_k, cache_mem_v, norm_mix, w_in, s5_a_re, s5_a_im, s5_log_dt, s5_b_re, s5_b_im, s5_c_re,
           s5_c_im, s5_d, s5_w_glu, mla_q_norm, mla_w_uq, mla_kv_norm, mla_w_ukv, mla_q_gain, mla_k_gain,
           w_branch, w_out, norm_cross, norm_mem, w_cq, w_ck, w_cv, cross_q_gain, cross_k_gain, w_co, norm_ffn,
           w_group_router, b_group, w_expert_router, b_expert, w_gate_up, w_down):
    params = dict(zip(_LAYER_PARAMS, (
        norm_mix, w_in, s5_a_re, s5_a_im, s5_log_dt, s5_b_re, s5_b_im, s5_c_re, s5_c_im, s5_d, s5_w_glu,
        mla_q_norm, mla_w_uq, mla_kv_norm, mla_w_ukv, mla_q_gain, mla_k_gain, w_branch, w_out, norm_cross,
        norm_mem, w_cq, w_ck, w_cv, cross_q_gain, cross_k_gain, w_co, norm_ffn, w_group_router, b_group,
        w_expert_router, b_expert)))
    assert x_prompt.shape == (BATCH, SEQ, D_MODEL) and x_sample.shape == (DEC_BATCH, DEC_SEQ, D_MODEL)
    assert cache_mla_ckv.shape == (DEPTH, DEC_BATCH, PAST_LEN, MLA_KV_LORA)

    pos_p = jnp.arange(SEQ, dtype=jnp.int32)
    pos_s = PAST_LEN + jnp.arange(DEC_SEQ, dtype=jnp.int32)
    tile_rows = 512
    tables = (_rope_tables(jnp.concatenate([pos_p, jnp.tile(pos_s, tile_rows // DEC_SEQ)]), MLA_ROPE)
              + _rope_tables(jnp.arange(PAST_LEN, dtype=jnp.int32), MLA_ROPE)
              + _rope_tables(jnp.concatenate([pos_p, pos_s]), RET_DK))

    x = jnp.concatenate([x_prompt.reshape(N_PROMPT, D_MODEL), x_sample.reshape(N_SAMPLE, D_MODEL)], axis=0)
    mem2d = mem_prompt.reshape(BATCH * N_MEM, D_MODEL)
    n_past = DEPTH * DEC_BATCH * PAST_LEN
    ckv_all = cache_mla_ckv.reshape(n_past, MLA_KV_LORA)
    kpe_past_all = jnp.pad(cache_mla_kpe.reshape(n_past, MLA_ROPE), ((0, 0), (0, LANES - MLA_ROPE)))
    ret_s0_all = state_ret.reshape(DEPTH * DEC_BATCH, RET_HEADS, RET_DK, RET_DV)
    mem_k_all = cache_mem_k.reshape(DEPTH * DEC_BATCH, N_MEM, MEM_HEADS, MEM_HEAD_DIM)
    mem_v_all = cache_mem_v.reshape(DEPTH * DEC_BATCH, N_MEM, MEM_HEADS, MEM_HEAD_DIM)
    outs_p, outs_s = [], []
    for l in range(DEPTH):
        lp = {name: value[l] for name, value in params.items()}
        caches = (ckv_all, kpe_past_all, state_s5_re[l], state_s5_im[l], ret_s0_all, mem_k_all, mem_v_all,
                  w_gate_up, w_down)
        x, st_p, st_s = _layer(x, mem2d, lp, l, caches, tables)
        outs_p.append(st_p)
        outs_s.append(st_s)
    stack = lambda outs, i: jnp.stack([o[i] for o in outs])
    return ((x[:N_PROMPT].reshape(BATCH, SEQ, D_MODEL), x[N_PROMPT:].reshape(DEC_BATCH, DEC_SEQ, D_MODEL))
            + tuple(stack(outs_p, i) for i in range(7)) + tuple(stack(outs_s, i) for i in range(5)))
```

```python
import functools
import math

import jax
import jax.numpy as jnp
from jax import lax
from jax.experimental import pallas as pl
from jax.experimental.pallas import tpu as pltpu

F32 = jnp.float32
BF16 = jnp.bfloat16

D_MODEL = 2048
BATCH = 4
SEQ = 2048
DEPTH = 2
DEC_BATCH = 32
DEC_SEQ = 32
PAST_LEN = 1024
CHUNK = 64
RMS_EPS = 1e-6
ROPE_THETA = 10000.0
NEG_INF = -1e30

S5_WIDTH = 1024
S5_GROUP_CH = 16
S5_GROUPS = 64
S5_STATE = 64
S5_COLS = S5_GROUPS * S5_STATE

MLA_HEADS = 8
MLA_NOPE = 128
MLA_ROPE = 64
MLA_QK = MLA_NOPE + MLA_ROPE
MLA_V = 128
MLA_Q_LORA = 512
MLA_KV_LORA = 512
MLA_HEAD_PAD = 256
MLA_QK_PAD = MLA_HEADS * MLA_HEAD_PAD

RET_HEADS = 4
RET_DK = 128
RET_DV = 256
BRANCH_WIDTH = 1024
N_BRANCH = 3

N_MEM = 256
MEM_HEADS = 4
MEM_HEAD_DIM = 128
MEM_WIDTH = MEM_HEADS * MEM_HEAD_DIM

MOE_GROUPS = 4
MOE_PER_GROUP = 8
N_EXPERTS = 32
MOE_TOP_K = 2
EXPERT_FF = 512

N_PROMPT = BATCH * SEQ
N_SAMPLE = DEC_BATCH * DEC_SEQ
N_ROWS = N_PROMPT + N_SAMPLE

COL_GMIX = 0
COL_S5 = 6144
COL_RV = 7168
COL_RG = 8192
COL_QLAT = 9216
COL_KVLAT = 9728
COL_RQ = 10240
COL_RK = 10752
PROJ_COLS = 11264
IN_S5 = (0, 1024)
IN_QLAT = (1024, 1536)
IN_KVLAT = (1536, 2048)
IN_KPE = (2048, 2112)
IN_RQ = (2112, 2624)
IN_RK = (2624, 3136)
IN_RV = (3136, 4160)
IN_RG = (4160, 5184)
IN_GMIX = (5184, 11328)

LANES = 128
EXPERT_ROWS = 256
N_PAIRS = N_ROWS * MOE_TOP_K
N_EXPERT_BLOCKS = (N_PAIRS + N_EXPERTS * (EXPERT_ROWS - 1)) // EXPERT_ROWS + 1
VMEM_LIMIT = 56 * 1024 * 1024


def _cparams(sem):
    return pltpu.CompilerParams(dimension_semantics=sem, vmem_limit_bytes=VMEM_LIMIT)


def _rms_rows(x, g):
    r = lax.rsqrt(jnp.mean(x * x, axis=-1, keepdims=True) + RMS_EPS)
    return (x * r) * g


def _dot(a, b):
    return jnp.dot(a, b, preferred_element_type=F32)


def _rows_call(kernel_fn, into, *, in_specs, args, **kw):
    if into is None:
        out0 = kw['out_shape'][0] if isinstance(kw['out_shape'], (list, tuple)) else kw['out_shape']
        into = jnp.zeros(out0.shape, out0.dtype)

    def aliased_kernel(into_ref, *refs):
        del into_ref
        kernel_fn(*refs)

    return pl.pallas_call(aliased_kernel, in_specs=[pl.BlockSpec(memory_space=pl.ANY)] + in_specs,
                          input_output_aliases={0: 0}, **kw)(into, *args)


def _dot_nt(a, b):
    return lax.dot_general(a, b, (((1,), (1,)), ((), ())), preferred_element_type=F32)


def _rms_proj_kernel(x_ref, g_ref, w_ref, *refs, side):
    if side:
        w2_ref, o_ref, o2_ref, xn_ref = refs
    else:
        o_ref, xn_ref = refs

    @pl.when(pl.program_id(1) == 0)
    def _():
        xn_ref[...] = _rms_rows(x_ref[...], g_ref[...]).astype(BF16)
        if side:
            o2_ref[...] = _dot(xn_ref[...], w2_ref[...]).astype(o2_ref.dtype)

    o_ref[...] = _dot(xn_ref[...], w_ref[...]).astype(o_ref.dtype)


def _rms_proj(x, g, w, out_dtype, tm, tn, w2=None, out2_dtype=None):
    n, k = x.shape
    cols = w.shape[1]
    side = w2 is not None
    in_specs = [
        pl.BlockSpec((tm, k), lambda i, j: (i, 0)),
        pl.BlockSpec((1, k), lambda i, j: (0, 0)),
        pl.BlockSpec((k, tn), lambda i, j: (0, j)),
    ]
    out_specs = [pl.BlockSpec((tm, tn), lambda i, j: (i, j))]
    out_shape = [jax.ShapeDtypeStruct((n, cols), out_dtype)]
    args = [x, g.reshape(1, k), w]
    if side:
        cols2 = w2.shape[1]
        in_specs.append(pl.BlockSpec((k, cols2), lambda i, j: (0, 0)))
        out_specs.append(pl.BlockSpec((tm, cols2), lambda i, j: (i, 0)))
        out_shape.append(jax.ShapeDtypeStruct((n, cols2), out2_dtype))
        args.append(w2)
    return pl.pallas_call(
        functools.partial(_rms_proj_kernel, side=side),
        grid=(n // tm, cols // tn),
        in_specs=in_specs,
        out_specs=out_specs,
        out_shape=out_shape,
        scratch_shapes=[pltpu.VMEM((tm, k), BF16)],
        compiler_params=_cparams(("parallel", "arbitrary")),
        name="rms_proj_side" if side else "rms_proj",
    )(*args)


def _res_matmul_kernel(a_ref, w_ref, r_ref, o_ref):
    o_ref[...] = r_ref[...] + _dot(a_ref[...], w_ref[...])


def _res_matmul(a, w, res, tm, tn):
    n, k = a.shape
    cols = w.shape[1]
    return pl.pallas_call(
        _res_matmul_kernel,
        grid=(n // tm, cols // tn),
        in_specs=[
            pl.BlockSpec((tm, k), lambda i, j: (i, 0)),
            pl.BlockSpec((k, tn), lambda i, j: (0, j)),
            pl.BlockSpec((tm, tn), lambda i, j: (i, j)),
        ],
        out_specs=pl.BlockSpec((tm, tn), lambda i, j: (i, j)),
        out_shape=jax.ShapeDtypeStruct((n, cols), F32),
        compiler_params=_cparams(("parallel", "arbitrary")),
        name="res_matmul",
    )(a, w, res)


S5_K_SLAB = 256
S5_N_SLAB = 1024
S5_SLABS = S5_WIDTH // S5_K_SLAB
S5_SCAN_COLS = 1024
S5_SCAN_ROWS = 8


def _gelu_tanh(x):
    return 0.5 * x * (1.0 + jnp.tanh(math.sqrt(2.0 / math.pi) * (x + 0.044715 * (x * x * x))))


def _sigmoid(x):
    return 1.0 / (1.0 + jnp.exp(-x))


def _s5_kernel(u_ref, bre_ref, bim_ref, cre_ref, cim_ref, pw_ref, d_ref, wglu_ref,
               h0r_ref, h0i_ref, o_ref, hr_out_ref, hi_out_ref, sre_ref, sim_ref, cr_ref, ci_ref, *, seqs, seq_rows):
    @pl.when(pl.program_id(1) == 0)
    def _():
        cr_ref[...] = h0r_ref[...]
        ci_ref[...] = h0i_ref[...]

    u = u_ref[...]
    for n in range(S5_SLABS):
        un = u[:, S5_K_SLAB * n:S5_K_SLAB * (n + 1)]
        sre_ref[:, S5_N_SLAB * n:S5_N_SLAB * (n + 1)] = _dot(un, bre_ref[n])
        sim_ref[:, S5_N_SLAB * n:S5_N_SLAB * (n + 1)] = _dot(un, bim_ref[n])

    for c, q in [(c, q) for c in range(S5_COLS // S5_SCAN_COLS) for q in range(seqs)]:
        sl = slice(c * S5_SCAN_COLS, (c + 1) * S5_SCAN_COLS)

        def body(blk, carry, sl=sl, q=q):
            cr, ci = carry
            rows = pl.ds(pl.multiple_of(q * seq_rows + blk * S5_SCAN_ROWS, S5_SCAN_ROWS), S5_SCAN_ROWS)
            xr = sre_ref[rows, sl]
            xi = sim_ref[rows, sl]
            for j in range(3):
                kr = pw_ref[j, 0, :, sl]
                ki = pw_ref[j, 1, :, sl]
                sr = pltpu.roll(xr, 1 << j, 0)
                si = pltpu.roll(xi, 1 << j, 0)
                xr, xi = xr + (kr * sr - ki * si), xi + (kr * si + ki * sr)
            pr = pw_ref[3, 0, :, sl]
            pi = pw_ref[3, 1, :, sl]
            hr = xr + (pr * cr - pi * ci)
            hi = xi + (pr * ci + pi * cr)
            sre_ref[rows, sl] = hr
            sim_ref[rows, sl] = hi
            last = S5_SCAN_ROWS - 1
            return (jnp.broadcast_to(hr[last:last + 1], hr.shape), jnp.broadcast_to(hi[last:last + 1], hi.shape))

        init = (jnp.broadcast_to(cr_ref[q, :, sl], (S5_SCAN_ROWS, S5_SCAN_COLS)),
                jnp.broadcast_to(ci_ref[q, :, sl], (S5_SCAN_ROWS, S5_SCAN_COLS)))
        hr, hi = lax.fori_loop(0, seq_rows // S5_SCAN_ROWS, body, init)
        cr_ref[q, :, sl] = hr[0:1]
        ci_ref[q, :, sl] = hi[0:1]

    ys = []
    for n in range(S5_SLABS):
        hr_n = sre_ref[:, S5_N_SLAB * n:S5_N_SLAB * (n + 1)].astype(BF16)
        hi_n = sim_ref[:, S5_N_SLAB * n:S5_N_SLAB * (n + 1)].astype(BF16)
        ys.append(_dot(hr_n, cre_ref[n]) + _dot(hi_n, cim_ref[n]))
    y = jnp.concatenate(ys, axis=1) + d_ref[...] * u.astype(F32)
    z = _gelu_tanh(y)
    gate = _sigmoid(_dot(z.astype(BF16), wglu_ref[...]))
    o_ref[...] = (z * gate).astype(o_ref.dtype)
    hr_out_ref[...] = cr_ref[...]
    hi_out_ref[...] = ci_ref[...]


def _s5_tables(lp):
    a_re, a_im = lp['s5_a_re'], lp['s5_a_im']
    dt = jnp.exp(lp['s5_log_dt'])[:, None]
    mag = jnp.exp(a_re * dt)
    lb_re = mag * jnp.cos(a_im * dt)
    lb_im = mag * jnp.sin(a_im * dt)
    den = a_re * a_re + a_im * a_im
    n_re = lb_re - 1.0
    co_re = (n_re * a_re + lb_im * a_im) / den
    co_im = (lb_im * a_re - n_re * a_im) / den
    bb_re = co_re[..., None] * lp['s5_b_re'] - co_im[..., None] * lp['s5_b_im']
    bb_im = co_re[..., None] * lp['s5_b_im'] + co_im[..., None] * lp['s5_b_re']
    per_slab = S5_GROUPS // S5_SLABS
    eye = jnp.eye(per_slab, dtype=F32)

    def b_tiles(bb):
        bb = bb.reshape(S5_SLABS, per_slab, S5_STATE, S5_GROUP_CH)
        return (bb.transpose(0, 1, 3, 2)[:, :, :, None, :] * eye[None, :, None, :, None]).reshape(
            S5_SLABS, S5_K_SLAB, S5_N_SLAB).astype(BF16)

    def c_tiles(cc):
        cc = cc.reshape(S5_SLABS, per_slab, S5_GROUP_CH, S5_STATE)
        return (cc.transpose(0, 1, 3, 2)[:, :, :, None, :] * eye[None, :, None, :, None]).reshape(
            S5_SLABS, S5_N_SLAB, S5_K_SLAB).astype(BF16)

    def lam_pow(k):
        m = jnp.exp(k * (a_re * dt).reshape(1, S5_COLS))
        ang = k * (a_im * dt).reshape(1, S5_COLS)
        return jnp.stack([m * jnp.cos(ang), m * jnp.sin(ang)])

    t = jnp.arange(S5_SCAN_ROWS, dtype=F32)[:, None]
    steps = [jnp.where(t >= float(1 << j), lam_pow(jnp.full_like(t, float(1 << j))), 0.0) for j in range(3)]
    pw = jnp.stack(steps + [lam_pow(t + 1.0)])

    return dict(
        bre=b_tiles(bb_re), bim=b_tiles(bb_im),
        cre=c_tiles(lp['s5_c_re']), cim=c_tiles(-lp['s5_c_im']),
        pw=pw, d=lp['s5_d'].reshape(1, S5_WIDTH),
    )


def _s5_branch(proj, tabs, wglu, h0_re, h0_im, n_seq, seq_len, tt, row0, into=None):
    seqs = max(tt // seq_len, 1)
    seq_rows = min(tt, seq_len)
    nt = seq_len // seq_rows
    base = row0 // tt
    col = COL_S5 // S5_WIDTH
    const3 = lambda s, t: (0, 0, 0)
    const2 = lambda s, t: (0, 0)
    state = pl.BlockSpec((seqs, 1, S5_COLS), lambda s, t: (s, 0, 0))
    out, hr, hi = _rows_call(
        functools.partial(_s5_kernel, seqs=seqs, seq_rows=seq_rows), into,
        grid=(n_seq // seqs, nt),
        in_specs=[
            pl.BlockSpec((tt, S5_WIDTH), lambda s, t: (base + s * nt + t, col)),
            pl.BlockSpec((S5_SLABS, S5_K_SLAB, S5_N_SLAB), const3),
            pl.BlockSpec((S5_SLABS, S5_K_SLAB, S5_N_SLAB), const3),
            pl.BlockSpec((S5_SLABS, S5_N_SLAB, S5_K_SLAB), const3),
            pl.BlockSpec((S5_SLABS, S5_N_SLAB, S5_K_SLAB), const3),
            pl.BlockSpec((4, 2, S5_SCAN_ROWS, S5_COLS), lambda s, t: (0, 0, 0, 0)),
            pl.BlockSpec((1, S5_WIDTH), const2),
            pl.BlockSpec((S5_WIDTH, S5_WIDTH), const2),
            state, state,
        ],
        out_specs=[pl.BlockSpec((tt, S5_WIDTH), lambda s, t: (base + s * nt + t, 0)), state, state],
        out_shape=[
            jax.ShapeDtypeStruct((N_ROWS, S5_WIDTH), BF16),
            jax.ShapeDtypeStruct((n_seq, 1, S5_COLS), F32),
            jax.ShapeDtypeStruct((n_seq, 1, S5_COLS), F32),
        ],
        scratch_shapes=[
            pltpu.VMEM((tt, S5_COLS), F32), pltpu.VMEM((tt, S5_COLS), F32),
            pltpu.VMEM((seqs, 1, S5_COLS), F32), pltpu.VMEM((seqs, 1, S5_COLS), F32),
        ],
        compiler_params=_cparams(("parallel", "arbitrary")),
        name="s5_branch",
        args=(proj, tabs['bre'], tabs['bim'], tabs['cre'], tabs['cim'], tabs['pw'], tabs['d'], wglu,
              h0_re.reshape(n_seq, 1, S5_COLS), h0_im.reshape(n_seq, 1, S5_COLS)),
    )
    return (out, hr.reshape(n_seq, S5_GROUPS, S5_STATE), hi.reshape(n_seq, S5_GROUPS, S5_STATE))


def _rope_tables(pos, d):
    inv_freq = ROPE_THETA ** (-jnp.arange(0, d, 2, dtype=F32) / d)
    ang = pos.astype(F32)[:, None] * inv_freq[None, :]
    cos, sin = jnp.cos(ang), jnp.sin(ang)
    pad = jnp.zeros((pos.shape[0], LANES - d), F32)
    return (jnp.concatenate([cos, cos, pad], axis=1), jnp.concatenate([-sin, sin, pad], axis=1))


def _pad_head_vec(g):
    return jnp.concatenate([g, jnp.zeros((MLA_HEAD_PAD - MLA_QK,), F32)]).reshape(1, MLA_HEAD_PAD)


def _qk_head(nope, pe, gain, cos2, sin2, scale):
    ss = jnp.sum(nope * nope, axis=-1, keepdims=True) + jnp.sum(pe * pe, axis=-1, keepdims=True)
    r = lax.rsqrt(ss * (1.0 / MLA_QK) + RMS_EPS)
    nope = (nope * r) * gain[:, :MLA_NOPE]
    pe = (pe * r) * gain[:, MLA_NOPE:]
    lane = lax.broadcasted_iota(jnp.int32, pe.shape, 1)
    half = MLA_ROPE // 2
    swap = jnp.where(lane < half, pltpu.roll(pe, LANES - half, 1), pltpu.roll(pe, half, 1))
    pe = pe * cos2 + swap * sin2
    return nope * scale, pe * scale


def _mla_q_kernel(lat_ref, g_ref, w_ref, gain_ref, cos_ref, sin_ref, o_ref):
    xn = _rms_rows(lat_ref[...].astype(F32), g_ref[...]).astype(BF16)
    q = _dot(xn, w_ref[...])
    gain = gain_ref[...]
    cos2 = cos_ref[...]
    sin2 = sin_ref[...]
    for h in range(MLA_HEADS):
        c0 = h * MLA_HEAD_PAD
        nope, pe = _qk_head(q[:, c0:c0 + MLA_NOPE], q[:, c0 + MLA_NOPE:c0 + MLA_HEAD_PAD],
                            gain, cos2, sin2, MLA_QK ** -0.5)
        o_ref[:, c0:c0 + MLA_NOPE] = nope.astype(BF16)
        o_ref[:, c0 + MLA_NOPE:c0 + MLA_HEAD_PAD] = pe.astype(BF16)


def _mla_q(proj, g, w_uq_pad, gain_pad, cos2, sin2, tab_block, tm):
    n = proj.shape[0]
    col = COL_QLAT // MLA_Q_LORA
    c2 = lambda i: (0, 0)
    tab = lambda i: (tab_block(i), 0)
    return pl.pallas_call(
        _mla_q_kernel,
        grid=(n // tm,),
        in_specs=[
            pl.BlockSpec((tm, MLA_Q_LORA), lambda i: (i, col)),
            pl.BlockSpec((1, MLA_Q_LORA), c2),
            pl.BlockSpec((MLA_Q_LORA, MLA_QK_PAD), c2),
            pl.BlockSpec((1, MLA_HEAD_PAD), c2),
            pl.BlockSpec((tm, LANES), tab),
            pl.BlockSpec((tm, LANES), tab),
        ],
        out_specs=pl.BlockSpec((tm, MLA_QK_PAD), lambda i: (i, 0)),
        out_shape=jax.ShapeDtypeStruct((n, MLA_QK_PAD), BF16),
        compiler_params=_cparams(("parallel",)),
        name="mla_q",
    )(proj, g.reshape(1, -1), w_uq_pad, gain_pad, cos2, sin2)


def _mla_kv_kernel(lat_ref, g_ref, w_ref, gain_ref, kpe_ref, cos_ref, sin_ref, ckv_ref, k_ref, v_ref):
    lat = _rms_rows(lat_ref[...].astype(F32), g_ref[...])
    ckv_ref[...] = lat
    kv = _dot(lat.astype(BF16), w_ref[...])
    gain = gain_ref[...]
    kpe = kpe_ref[...]
    cos2 = cos_ref[...]
    sin2 = sin_ref[...]
    for h in range(MLA_HEADS):
        c0 = h * (MLA_NOPE + MLA_V)
        nope, pe = _qk_head(kv[:, c0:c0 + MLA_NOPE], kpe, gain, cos2, sin2, 1.0)
        k0 = h * MLA_HEAD_PAD
        k_ref[:, k0:k0 + MLA_NOPE] = nope.astype(BF16)
        k_ref[:, k0 + MLA_NOPE:k0 + MLA_HEAD_PAD] = pe.astype(BF16)
        v_ref[:, h * MLA_V:(h + 1) * MLA_V] = kv[:, c0 + MLA_NOPE:c0 + MLA_NOPE + MLA_V].astype(BF16)


def _mla_kv(proj, g, w_ukv, gain_pad, kpe_pad, cos2, sin2, tab_block, tm):
    n = proj.shape[0]
    c2 = lambda i: (0, 0)
    row = lambda i: (i, 0)
    tab = lambda i: (tab_block(i), 0)
    return pl.pallas_call(
        _mla_kv_kernel,
        grid=(n // tm,),
        in_specs=[
            pl.BlockSpec((tm, MLA_KV_LORA), lambda i: (i, COL_KVLAT // MLA_KV_LORA)),
            pl.BlockSpec((1, MLA_KV_LORA), c2),
            pl.BlockSpec((MLA_KV_LORA, MLA_HEADS * (MLA_NOPE + MLA_V)), c2),
            pl.BlockSpec((1, MLA_HEAD_PAD), c2),
            pl.BlockSpec((tm, LANES), row),
            pl.BlockSpec((tm, LANES), tab),
            pl.BlockSpec((tm, LANES), tab),
        ],
        out_specs=[pl.BlockSpec((tm, MLA_KV_LORA), row), pl.BlockSpec((tm, MLA_QK_PAD), row),
                   pl.BlockSpec((tm, MLA_HEADS * MLA_V), row)],
        out_shape=[jax.ShapeDtypeStruct((n, MLA_KV_LORA), F32), jax.ShapeDtypeStruct((n, MLA_QK_PAD), BF16),
                   jax.ShapeDtypeStruct((n, MLA_HEADS * MLA_V), BF16)],
        compiler_params=_cparams(("parallel",)),
        name="mla_kv",
    )(proj, g.reshape(1, -1), w_ukv, gain_pad, kpe_pad, cos2, sin2)


ATTN_HEADS_PER_STEP = 2


def _attn_prompt_kernel(q_ref, k_ref, v_ref, o_ref, *, tq):
    qi = pl.program_id(2)

    def key_tile(ki, state, diagonal):
        rows = pl.ds(pl.multiple_of(ki * tq, tq), tq)
        new_state = []
        for h in range(ATTN_HEADS_PER_STEP):
            qk = slice(h * MLA_HEAD_PAD, (h + 1) * MLA_HEAD_PAD)
            vs = slice(h * MLA_V, (h + 1) * MLA_V)
            s = _dot_nt(q_ref[:, qk], k_ref[rows, qk])
            if diagonal:
                row_chunk = lax.broadcasted_iota(jnp.int32, s.shape, 0) // CHUNK
                col_chunk = lax.broadcasted_iota(jnp.int32, s.shape, 1) // CHUNK
                s = jnp.where(col_chunk <= row_chunk, s, NEG_INF)
            m_old, l_old, acc_old = state[h]
            m_new = jnp.maximum(m_old, jnp.max(s, axis=-1, keepdims=True))
            alpha = jnp.exp(m_old - m_new)
            p = jnp.exp(s - m_new)
            l_new = alpha * l_old + jnp.sum(p, axis=-1, keepdims=True)
            acc_new = alpha * acc_old + _dot(p.astype(BF16), v_ref[rows, vs])
            new_state.append((m_new, l_new, acc_new))
        return tuple(new_state)

    init = tuple((jnp.full((tq, 1), NEG_INF, F32), jnp.zeros((tq, 1), F32), jnp.zeros((tq, MLA_V), F32))
                 for _ in range(ATTN_HEADS_PER_STEP))
    state = lax.fori_loop(0, qi, lambda ki, st: key_tile(ki, st, False), init)
    state = key_tile(qi, state, True)
    for h in range(ATTN_HEADS_PER_STEP):
        _, l_fin, acc_fin = state[h]
        o_ref[:, h * MLA_V:(h + 1) * MLA_V] = (acc_fin / l_fin).astype(o_ref.dtype)


def _attn_prompt(q, k, v, n_batch, seq_len, tq):
    assert tq % CHUNK == 0
    nq = seq_len // tq
    hp = ATTN_HEADS_PER_STEP
    return _rows_call(
        functools.partial(_attn_prompt_kernel, tq=tq), None,
        grid=(n_batch, MLA_HEADS // hp, nq),
        in_specs=[
            pl.BlockSpec((tq, hp * MLA_HEAD_PAD), lambda b, h, qi: (b * nq + qi, h)),
            pl.BlockSpec((seq_len, hp * MLA_HEAD_PAD), lambda b, h, qi: (b, h)),
            pl.BlockSpec((seq_len, hp * MLA_V), lambda b, h, qi: (b, h)),
        ],
        out_specs=pl.BlockSpec((tq, hp * MLA_V), lambda b, h, qi: (b * nq + qi, h)),
        out_shape=jax.ShapeDtypeStruct((N_ROWS, MLA_HEADS * MLA_V), BF16),
        compiler_params=_cparams(("parallel", "parallel", "arbitrary")),
        name="attn_prompt",
        args=(q, k, v),
    )


def _attn_sample_kernel(q_ref, kn_ref, vn_ref, c_ref, kpe_ref, cos_ref, sin_ref, wk_ref, wv_ref, gain_ref, sel_ref,
                        o_ref):
    c = c_ref[...].astype(BF16)
    k_raw = _dot(c, wk_ref[...])
    v_all = _dot(c, wv_ref[...]).astype(BF16)
    gain = gain_ref[...]
    gain_n = gain[:, :MLA_NOPE]
    kpe = kpe_ref[...]
    pe = kpe * gain[:, MLA_NOPE:]
    lane = lax.broadcasted_iota(jnp.int32, pe.shape, 1)
    half = MLA_ROPE // 2
    swap = jnp.where(lane < half, pltpu.roll(pe, LANES - half, 1), pltpu.roll(pe, half, 1))
    pe = (pe * cos_ref[...] + swap * sin_ref[...]).astype(BF16)
    ss = (_dot_nt(sel_ref[...], (k_raw * k_raw).astype(BF16))
          + _dot_nt(jnp.ones((MLA_HEADS, LANES), BF16), (kpe * kpe).astype(BF16)))
    r = lax.rsqrt(ss * (1.0 / MLA_QK) + RMS_EPS)
    k_raw = k_raw.astype(BF16)
    for h in range(MLA_HEADS):
        ks = slice(h * MLA_HEAD_PAD, (h + 1) * MLA_HEAD_PAD)
        vs = slice(h * MLA_V, (h + 1) * MLA_V)
        q = q_ref[:, ks]
        q_n = (q[:, :MLA_NOPE].astype(F32) * gain_n).astype(BF16)
        s_p = (_dot_nt(q_n, k_raw[:, vs]) + _dot_nt(q[:, MLA_NOPE:], pe)) * r[h:h + 1]
        s_n = _dot_nt(q, kn_ref[:, ks])
        m = jnp.maximum(jnp.max(s_p, axis=-1, keepdims=True), jnp.max(s_n, axis=-1, keepdims=True))
        p_p = jnp.exp(s_p - m)
        p_n = jnp.exp(s_n - m)
        l = jnp.sum(p_p, axis=-1, keepdims=True) + jnp.sum(p_n, axis=-1, keepdims=True)
        o = _dot(p_p.astype(BF16), v_all[:, vs]) + _dot(p_n.astype(BF16), vn_ref[:, vs])
        o_ref[:, vs] = (o / l).astype(o_ref.dtype)


def _attn_sample(q, k_new, v_new, ckv_all, kpe_all, past0, cos, sin, w_uk, w_uv, gain_pad, row0, into):
    assert (PAST_LEN + DEC_SEQ - 1) // CHUNK <= PAST_LEN // CHUNK
    base = row0 // DEC_SEQ
    new = lambda b: (base + b, 0)
    past = lambda b: (past0 + b, 0)
    c2 = lambda b: (0, 0)
    width = MLA_HEADS * MLA_NOPE
    sel = jnp.repeat(jnp.eye(MLA_HEADS, dtype=F32), MLA_NOPE, axis=1).astype(BF16)
    return _rows_call(
        _attn_sample_kernel, into,
        grid=(DEC_BATCH,),
        in_specs=[
            pl.BlockSpec((DEC_SEQ, MLA_QK_PAD), new),
            pl.BlockSpec((DEC_SEQ, MLA_QK_PAD), new),
            pl.BlockSpec((DEC_SEQ, MLA_HEADS * MLA_V), new),
            pl.BlockSpec((PAST_LEN, MLA_KV_LORA), past),
            pl.BlockSpec((PAST_LEN, LANES), past),
            pl.BlockSpec((PAST_LEN, LANES), c2),
            pl.BlockSpec((PAST_LEN, LANES), c2),
            pl.BlockSpec((MLA_KV_LORA, width), c2),
            pl.BlockSpec((MLA_KV_LORA, MLA_HEADS * MLA_V), c2),
            pl.BlockSpec((1, MLA_HEAD_PAD), c2),
            pl.BlockSpec((MLA_HEADS, width), c2),
        ],
        out_specs=pl.BlockSpec((DEC_SEQ, MLA_HEADS * MLA_V), new),
        out_shape=jax.ShapeDtypeStruct((N_ROWS, MLA_HEADS * MLA_V), BF16),
        compiler_params=_cparams(("parallel",)),
        name="attn_sample",
        args=(q, k_new, v_new, ckv_all, kpe_all, cos, sin, w_uk, w_uv, gain_pad, sel),
    )


def _retention_tables(block):
    log_g = jnp.log1p(-jnp.exp2(-5.0 - jnp.arange(RET_HEADS, dtype=F32)))
    idx = jnp.arange(block, dtype=F32)
    diff = idx[:, None] - idx[None, :]
    decay = jnp.where(diff >= 0, jnp.exp(log_g[:, None, None] * jnp.maximum(diff, 0.0)), 0.0)
    q_decay = jnp.exp(log_g[:, None] * (idx + 1.0))
    k_decay = jnp.exp(log_g[:, None] * (block - 1.0 - idx))
    blk_decay = jnp.exp(log_g * block)
    return (decay,
            jnp.broadcast_to(q_decay[:, :, None], (RET_HEADS, block, RET_DK)),
            jnp.broadcast_to(k_decay[:, :, None], (RET_HEADS, block, RET_DK)),
            jnp.broadcast_to(blk_decay[:, None, None], (RET_HEADS, 1, RET_DV)))


def _rope128(x, cos, sin):
    return x * cos + pltpu.roll(x, RET_DK // 2, 1) * sin


def _retention_kernel(rq_ref, rk_ref, rv_ref, rg_ref, cos_ref, sin_ref, dec_ref, qd_ref, kd_ref, bd_ref, s0_ref,
                      o_ref, s_out_ref, s_ref):
    @pl.when(pl.program_id(1) == 0)
    def _():
        s_ref[...] = s0_ref[0]

    cos = cos_ref[...]
    sin = sin_ref[...]
    for h in range(RET_HEADS):
        ks = slice(h * RET_DK, (h + 1) * RET_DK)
        vs = slice(h * RET_DV, (h + 1) * RET_DV)
        q = _rope128(rq_ref[:, ks].astype(F32), cos, sin) * (RET_DK ** -0.5)
        k = _rope128(rk_ref[:, ks].astype(F32), cos, sin)
        v = rv_ref[:, vs]
        s_old = s_ref[h]
        att = _dot_nt(q.astype(BF16), k.astype(BF16)) * dec_ref[h]
        o = _dot(att.astype(BF16), v) + _dot((q * qd_ref[h]).astype(BF16), s_old.astype(BF16))
        kd = (k * kd_ref[h]).astype(BF16)
        s_new = bd_ref[h] * s_old + lax.dot_general(kd, v, (((0,), (0,)), ((), ())), preferred_element_type=F32)
        s_ref[h] = s_new
        s_out_ref[0, h] = s_new
        mu = jnp.mean(o, axis=-1, keepdims=True)
        c = o - mu
        o = c * lax.rsqrt(jnp.mean(c * c, axis=-1, keepdims=True) + RMS_EPS)
        g = rg_ref[:, vs].astype(F32)
        o_ref[:, vs] = (o * (g * _sigmoid(g))).astype(o_ref.dtype)


def _retention_branch(proj, cos, sin, tab_block, s0, s0_base, n_seq, seq_len, block, row0, into=None):
    nblk = seq_len // block
    base = row0 // block
    dec, qd, kd, bd = _retention_tables(block)
    rows = lambda b, t: base + b * nblk + t
    tab = lambda b, t: (tab_block(t), 0)
    all3 = lambda b, t: (0, 0, 0)
    qk_w = RET_HEADS * RET_DK
    v_w = RET_HEADS * RET_DV
    out, s_last = _rows_call(
        _retention_kernel, into,
        grid=(n_seq, nblk),
        in_specs=[
            pl.BlockSpec((block, qk_w), lambda b, t: (rows(b, t), COL_RQ // qk_w)),
            pl.BlockSpec((block, qk_w), lambda b, t: (rows(b, t), COL_RK // qk_w)),
            pl.BlockSpec((block, v_w), lambda b, t: (rows(b, t), COL_RV // v_w)),
            pl.BlockSpec((block, v_w), lambda b, t: (rows(b, t), COL_RG // v_w)),
            pl.BlockSpec((block, LANES), tab),
            pl.BlockSpec((block, LANES), tab),
            pl.BlockSpec((RET_HEADS, block, block), all3),
            pl.BlockSpec((RET_HEADS, block, RET_DK), all3),
            pl.BlockSpec((RET_HEADS, block, RET_DK), all3),
            pl.BlockSpec((RET_HEADS, 1, RET_DV), all3),
            pl.BlockSpec((1, RET_HEADS, RET_DK, RET_DV), lambda b, t: (s0_base + b, 0, 0, 0)),
        ],
        out_specs=[
            pl.BlockSpec((block, v_w), lambda b, t: (rows(b, t), 0)),
            pl.BlockSpec((1, RET_HEADS, RET_DK, RET_DV), lambda b, t: (b, 0, 0, 0)),
        ],
        out_shape=[
            jax.ShapeDtypeStruct((N_ROWS, v_w), BF16),
            jax.ShapeDtypeStruct((n_seq, RET_HEADS, RET_DK, RET_DV), F32),
        ],
        scratch_shapes=[pltpu.VMEM((RET_HEADS, RET_DK, RET_DV), F32)],
        compiler_params=_cparams(("parallel", "arbitrary")),
        name="retention",
        args=(proj, proj, proj, proj, cos, sin, dec, qd, kd, bd, s0),
    )
    return out, s_last


def _merge_kernel(a_ref, b_ref, c_ref, w_ref, g0_ref, g1_ref, g2_ref, o_ref):
    acc = _sigmoid(g0_ref[...].astype(F32)) * _dot(a_ref[...], w_ref[0])
    acc += _sigmoid(g1_ref[...].astype(F32)) * _dot(b_ref[...], w_ref[1])
    acc += _sigmoid(g2_ref[...].astype(F32)) * _dot(c_ref[...], w_ref[2])
    o_ref[...] = acc.astype(o_ref.dtype)


def _merge(o_a, o_b, o_c, w_branch, proj, tm, tn):
    n = o_a.shape[0]
    nj = D_MODEL // tn
    br = pl.BlockSpec((tm, BRANCH_WIDTH), lambda i, j: (i, 0))
    gate = lambda b: pl.BlockSpec((tm, tn), lambda i, j: (i, COL_GMIX // tn + b * nj + j))
    return pl.pallas_call(
        _merge_kernel,
        grid=(n // tm, nj),
        in_specs=[br, br, br,
                  pl.BlockSpec((N_BRANCH, BRANCH_WIDTH, tn), lambda i, j: (0, 0, j)),
                  gate(0), gate(1), gate(2)],
        out_specs=pl.BlockSpec((tm, tn), lambda i, j: (i, j)),
        out_shape=jax.ShapeDtypeStruct((n, D_MODEL), BF16),
        compiler_params=_cparams(("parallel", "arbitrary")),
        name="branch_merge",
    )(o_a, o_b, o_c, w_branch, proj, proj, proj)


def _memkv_kernel(m_ref, g_ref, wk_ref, wv_ref, kg_ref, k_ref, v_ref):
    mn = _rms_rows(m_ref[...], g_ref[...]).astype(BF16)
    kk = _dot(mn, wk_ref[...])
    for h in range(MEM_HEADS):
        sl = slice(h * MEM_HEAD_DIM, (h + 1) * MEM_HEAD_DIM)
        k_ref[:, sl] = _rms_rows(kk[:, sl], kg_ref[...])
    v_ref[...] = _dot(mn, wv_ref[...])


def _memkv(mem, g, wk, wv, k_gain, tm):
    n, k = mem.shape
    c2 = lambda i: (0, 0)
    return pl.pallas_call(
        _memkv_kernel,
        grid=(n // tm,),
        in_specs=[
            pl.BlockSpec((tm, k), lambda i: (i, 0)),
            pl.BlockSpec((1, k), c2),
            pl.BlockSpec((k, MEM_WIDTH), c2),
            pl.BlockSpec((k, MEM_WIDTH), c2),
            pl.BlockSpec((1, MEM_HEAD_DIM), c2),
        ],
        out_specs=[pl.BlockSpec((tm, MEM_WIDTH), lambda i: (i, 0)), pl.BlockSpec((tm, MEM_WIDTH), lambda i: (i, 0))],
        out_shape=[jax.ShapeDtypeStruct((n, MEM_WIDTH), F32), jax.ShapeDtypeStruct((n, MEM_WIDTH), F32)],
        compiler_params=_cparams(("parallel",)),
        name="memory_kv",
    )(mem, g.reshape(1, k), wk, wv, k_gain.reshape(1, MEM_HEAD_DIM))


def _cross_attn_kernel(q_ref, k_ref, v_ref, qg_ref, o_ref, *, head_axis):
    for h in range(MEM_HEADS):
        sl = slice(h * MEM_HEAD_DIM, (h + 1) * MEM_HEAD_DIM)
        if head_axis:
            k, v = k_ref[0, :, h, :], v_ref[0, :, h, :]
        else:
            k, v = k_ref[0, :, sl], v_ref[0, :, sl]
        q = _rms_rows(q_ref[:, sl].astype(F32), qg_ref[...]) * (MEM_HEAD_DIM ** -0.5)
        s = _dot_nt(q.astype(BF16), k.astype(BF16))
        p = jnp.exp(s - jnp.max(s, axis=-1, keepdims=True))
        l = jnp.sum(p, axis=-1, keepdims=True)
        o = _dot(p.astype(BF16), v.astype(BF16))
        o_ref[:, sl] = (o / l).astype(o_ref.dtype)


def _cross_attn(qc, mem_k, mem_v, mem0, q_gain, n_batch, rows_per_batch, tm, row0, into=None):
    tiles = rows_per_batch // tm
    base = row0 // tm
    head_axis = mem_k.ndim == 4
    if head_axis:
        mem_spec = pl.BlockSpec((1, N_MEM, MEM_HEADS, MEM_HEAD_DIM), lambda b, t: (mem0 + b, 0, 0, 0))
    else:
        mem_spec = pl.BlockSpec((1, N_MEM, MEM_WIDTH), lambda b, t: (mem0 + b, 0, 0))
    return _rows_call(
        functools.partial(_cross_attn_kernel, head_axis=head_axis), into,
        grid=(n_batch, tiles),
        in_specs=[
            pl.BlockSpec((tm, MEM_WIDTH), lambda b, t: (base + b * tiles + t, 0)),
            mem_spec, mem_spec,
            pl.BlockSpec((1, MEM_HEAD_DIM), lambda b, t: (0, 0)),
        ],
        out_specs=pl.BlockSpec((tm, MEM_WIDTH), lambda b, t: (base + b * tiles + t, 0)),
        out_shape=jax.ShapeDtypeStruct((N_ROWS, MEM_WIDTH), BF16),
        compiler_params=_cparams(("parallel", "arbitrary")),
        name="cross_attn",
        args=(qc, mem_k, mem_v, q_gain.reshape(1, MEM_HEAD_DIM)),
    )


ROUTER_LANES = LANES
FAR_LANE = 4 * LANES


def _split_bf16(x):
    hi = x.astype(BF16)
    lo = (x - hi.astype(F32)).astype(BF16)
    return hi, lo


def _router_kernel(x_ref, g_ref, w_ref, b_ref, xn_ref, ids_ref, gates_ref):
    xn = _rms_rows(x_ref[...], g_ref[...])
    xn_ref[...] = xn
    xh, xl = _split_bf16(xn)
    wh, wl = _split_bf16(w_ref[...])
    logits = _dot(xh, wh) + (_dot(xh, wl) + _dot(xl, wh)) + b_ref[...]
    lane = lax.broadcasted_iota(jnp.int32, logits.shape, 1)

    def first_max(vals):
        top = jnp.max(vals, axis=-1, keepdims=True)
        idx = jnp.min(jnp.where(vals == top, lane, FAR_LANE), axis=-1, keepdims=True)
        return top, idx

    g_mask = jnp.logical_and(lane >= N_EXPERTS, lane < N_EXPERTS + MOE_GROUPS)
    g_top, g_lane = first_max(jnp.where(g_mask, logits, NEG_INF))
    p_group = 1.0 / jnp.sum(jnp.where(g_mask, jnp.exp(logits - g_top), 0.0), axis=-1, keepdims=True)
    g_sel = g_lane - N_EXPERTS
    e_mask = jnp.logical_and(lane < N_EXPERTS, lane // MOE_PER_GROUP == g_sel)
    e_vals = jnp.where(e_mask, logits, NEG_INF)
    e1_top, e1 = first_max(e_vals)
    e2_top, e2 = first_max(jnp.where(lane == e1, NEG_INF, e_vals))
    t = jnp.exp(e2_top - e1_top)
    gate1 = p_group / (1.0 + t)
    gate2 = p_group * t / (1.0 + t)
    ids_ref[...] = jnp.where(lane == 0, e1, jnp.where(lane == 1, e2, 0))
    gates_ref[...] = jnp.where(lane == 0, gate1, jnp.where(lane == 1, gate2, 0.0))


def _router(x, g, w_router, b_router, tm):
    n, k = x.shape
    c2 = lambda i: (0, 0)
    row = lambda i: (i, 0)
    return pl.pallas_call(
        _router_kernel,
        grid=(n // tm,),
        in_specs=[
            pl.BlockSpec((tm, k), row),
            pl.BlockSpec((1, k), c2),
            pl.BlockSpec((k, ROUTER_LANES), c2),
            pl.BlockSpec((1, ROUTER_LANES), c2),
        ],
        out_specs=[pl.BlockSpec((tm, k), row), pl.BlockSpec((tm, ROUTER_LANES), row),
                   pl.BlockSpec((tm, ROUTER_LANES), row)],
        out_shape=[jax.ShapeDtypeStruct((n, k), F32), jax.ShapeDtypeStruct((n, ROUTER_LANES), jnp.int32),
                   jax.ShapeDtypeStruct((n, ROUTER_LANES), F32)],
        compiler_params=_cparams(("parallel",)),
        name="moe_router",
    )(x, g.reshape(1, k), w_router, b_router)


def _route_plan(expert_idx):
    flat_e = expert_idx.reshape(N_PAIRS)
    onehot = (flat_e[:, None] == jnp.arange(N_EXPERTS, dtype=jnp.int32)[None, :]).astype(jnp.int32)
    csum = jnp.cumsum(onehot, axis=0)
    rank = jnp.take_along_axis(csum, flat_e[:, None], axis=1)[:, 0] - 1
    counts = csum[-1]
    padded = (counts + EXPERT_ROWS - 1) // EXPERT_ROWS * EXPERT_ROWS
    pad_ends = jnp.cumsum(padded)
    pad_starts = pad_ends - padded
    dest = (pad_starts[flat_e] + rank).astype(jnp.int32)
    n_used = (pad_ends[-1] // EXPERT_ROWS).astype(jnp.int32).reshape(1)
    block_start = jnp.arange(N_EXPERT_BLOCKS, dtype=jnp.int32) * EXPERT_ROWS
    block_e = jnp.sum((pad_ends[None, :] <= block_start[:, None]).astype(jnp.int32), axis=1)
    block_e = jnp.minimum(block_e, N_EXPERTS - 1)
    src = jnp.zeros((N_EXPERT_BLOCKS * EXPERT_ROWS,), jnp.int32).at[dest].set(
        jnp.arange(N_PAIRS, dtype=jnp.int32) // MOE_TOP_K)
    return dest, src, block_e, n_used


DMA_UNROLL = 8


def _experts_kernel(src_ref, be_ref, nu_ref, x_hbm, wgu_ref, wd_ref, o_ref, xbuf_ref, sem):
    i = pl.program_id(0)
    n_used = nu_ref[0]
    slot = lax.rem(i, 2)

    def row_copy(blk, buf, c, u):
        return pltpu.make_async_copy(x_hbm.at[pl.ds(src_ref[blk * EXPERT_ROWS + c * DMA_UNROLL + u], 1)],
                                     xbuf_ref.at[buf, c, pl.ds(u, 1)], sem.at[buf])

    def start_block(blk, buf):
        def body(c, carry):
            for u in range(DMA_UNROLL):
                row_copy(blk, buf, c, u).start(priority=u % 2)
            return carry
        lax.fori_loop(0, EXPERT_ROWS // DMA_UNROLL, body, 0)

    def wait_block(blk, buf):
        def body(c, carry):
            for u in range(DMA_UNROLL):
                row_copy(blk, buf, c, u).wait()
            return carry
        lax.fori_loop(0, EXPERT_ROWS // DMA_UNROLL, body, 0)

    @pl.when(jnp.logical_and(i == 0, n_used > 0))
    def _():
        start_block(0, 0)

    @pl.when(i + 1 < n_used)
    def _():
        start_block(i + 1, 1 - slot)

    @pl.when(i < n_used)
    def _():
        wait_block(i, slot)
        x = xbuf_ref[slot].reshape(EXPERT_ROWS, D_MODEL)
        gu = _dot(x.astype(BF16), wgu_ref[0, 0].astype(BF16))
        g = gu[:, :EXPERT_FF]
        h = (g * _sigmoid(g)) * gu[:, EXPERT_FF:]
        o_ref[...] = _dot(h.astype(BF16), wd_ref[0, 0].astype(BF16))

    @pl.when(i >= n_used)
    def _():
        o_ref[...] = jnp.zeros(o_ref.shape, o_ref.dtype)


def _experts(xn, src, block_e, n_used, w_gate_up, w_down, layer):
    return pl.pallas_call(
        _experts_kernel,
        grid_spec=pltpu.PrefetchScalarGridSpec(
            num_scalar_prefetch=3,
            grid=(N_EXPERT_BLOCKS,),
            in_specs=[
                pl.BlockSpec(memory_space=pl.ANY),
                pl.BlockSpec((1, 1, D_MODEL, 2 * EXPERT_FF), lambda i, src, be, nu: (layer, be[i], 0, 0)),
                pl.BlockSpec((1, 1, EXPERT_FF, D_MODEL), lambda i, src, be, nu: (layer, be[i], 0, 0)),
            ],
            out_specs=pl.BlockSpec((EXPERT_ROWS, D_MODEL), lambda i, src, be, nu: (i, 0)),
            scratch_shapes=[pltpu.VMEM((2, EXPERT_ROWS // DMA_UNROLL, DMA_UNROLL, D_MODEL), F32),
                            pltpu.SemaphoreType.DMA((2,))],
        ),
        out_shape=jax.ShapeDtypeStruct((N_EXPERT_BLOCKS * EXPERT_ROWS, D_MODEL), F32),
        compiler_params=_cparams(("arbitrary",)),
        name="moe_experts",
    )(src, block_e, n_used, xn, w_gate_up, w_down)


def _combine_kernel(dest_ref, x_ref, gates_ref, y_hbm, o_ref, buf_ref, sem, *, rows):
    i = pl.program_id(0)
    slot = lax.rem(i, 2)
    per_iter = DMA_UNROLL

    def row_copy(tile, buf, c, u, k):
        return pltpu.make_async_copy(
            y_hbm.at[pl.ds(dest_ref[(tile * rows + c * per_iter + u) * MOE_TOP_K + k], 1)],
            buf_ref.at[buf, k, c, pl.ds(u, 1)], sem.at[buf])

    def start_tile(tile, buf):
        def body(c, carry):
            for u in range(per_iter):
                for k in range(MOE_TOP_K):
                    row_copy(tile, buf, c, u, k).start(priority=k)
            return carry
        lax.fori_loop(0, rows // per_iter, body, 0)

    def wait_tile(tile, buf):
        def body(c, carry):
            for u in range(per_iter):
                for k in range(MOE_TOP_K):
                    row_copy(tile, buf, c, u, k).wait()
            return carry
        lax.fori_loop(0, rows // per_iter, body, 0)

    @pl.when(i == 0)
    def _():
        start_tile(0, 0)

    @pl.when(i + 1 < pl.num_programs(0))
    def _():
        start_tile(i + 1, 1 - slot)

    wait_tile(i, slot)
    gates = gates_ref[...]
    y0 = buf_ref[slot, 0].reshape(x_ref.shape)
    y1 = buf_ref[slot, 1].reshape(x_ref.shape)
    o_ref[...] = x_ref[...] + (gates[:, 0:1] * y0 + gates[:, 1:2] * y1)


def _combine(x, gates, y_sorted, dest, rows):
    n, width = x.shape
    return pl.pallas_call(
        functools.partial(_combine_kernel, rows=rows),
        grid_spec=pltpu.PrefetchScalarGridSpec(
            num_scalar_prefetch=1,
            grid=(n // rows,),
            in_specs=[
                pl.BlockSpec((rows, width), lambda i, d: (i, 0)),
                pl.BlockSpec((rows, ROUTER_LANES), lambda i, d: (i, 0)),
                pl.BlockSpec(memory_space=pl.ANY),
            ],
            out_specs=pl.BlockSpec((rows, width), lambda i, d: (i, 0)),
            scratch_shapes=[pltpu.VMEM((2, MOE_TOP_K, rows // DMA_UNROLL, DMA_UNROLL, width), F32),
                            pltpu.SemaphoreType.DMA((2,))],
        ),
        out_shape=jax.ShapeDtypeStruct((n, width), F32),
        compiler_params=_cparams(("arbitrary",)),
        name="moe_combine",
    )(dest, x, gates, y_sorted)


def _hier_moe(x, lp, w_gate_up, w_down, layer):
    w_router = jnp.concatenate(
        [lp['w_expert_router'], lp['w_group_router'],
         jnp.zeros((D_MODEL, ROUTER_LANES - N_EXPERTS - MOE_GROUPS), F32)], axis=1)
    b_router = jnp.concatenate(
        [lp['b_expert'], lp['b_group'], jnp.zeros((ROUTER_LANES - N_EXPERTS - MOE_GROUPS,), F32)]).reshape(1, -1)
    xn, ids, gates = _router(x, lp['norm_ffn'], w_router, b_router, 512)
    dest, src, block_e, n_used = _route_plan(ids[:, :MOE_TOP_K])
    y_sorted = _experts(xn, src, block_e, n_used, w_gate_up, w_down, layer)
    return _combine(x, gates, y_sorted, dest, 256)


def _cols(w, *ranges):
    return jnp.concatenate([w[:, a:b] for a, b in ranges], axis=1)


def _prompt_or_sample_block(i, tm):
    return jnp.where(i < N_PROMPT // tm, lax.rem(i, SEQ // tm), SEQ // tm)


def _layer(x, mem_prompt, lp, layer, caches, tables):
    (ckv_all, kpe_past_all, h0_re, h0_im, ret_s0_all, mem_k_all, mem_v_all, w_gate_up, w_down) = caches
    cos_mla, sin_mla, cos_past, sin_past, cos_ret, sin_ret = tables
    tm = 512
    w_in = lp['w_in']
    w_main = _cols(w_in, IN_GMIX, IN_S5, IN_RV, IN_RG, IN_QLAT, IN_KVLAT, IN_RQ, IN_RK).astype(BF16)
    w_kpe = jnp.pad(_cols(w_in, IN_KPE), ((0, 0), (0, LANES - MLA_ROPE))).astype(BF16)
    proj, kpe = _rms_proj(x, lp['norm_mix'], w_main, BF16, 1024, 1024, w2=w_kpe, out2_dtype=F32)

    s5_tabs = _s5_tables(lp)
    w_glu = lp['s5_w_glu'].astype(BF16)
    zeros_s5 = jnp.zeros((BATCH, S5_GROUPS, S5_STATE), F32)
    a, p_s5_re, p_s5_im = _s5_branch(proj, s5_tabs, w_glu, zeros_s5, zeros_s5, BATCH, SEQ, 256, 0)
    a, s_s5_re, s_s5_im = _s5_branch(proj, s5_tabs, w_glu, h0_re, h0_im, DEC_BATCH, DEC_SEQ, 256, N_PROMPT, into=a)

    w_uq = lp['mla_w_uq'].reshape(MLA_Q_LORA, MLA_HEADS, MLA_QK)
    w_uq = jnp.pad(w_uq, ((0, 0), (0, 0), (0, MLA_HEAD_PAD - MLA_QK))).reshape(MLA_Q_LORA, MLA_QK_PAD).astype(BF16)
    w_ukv = lp['mla_w_ukv'].astype(BF16)
    q_gain = _pad_head_vec(lp['mla_q_gain'])
    k_gain = _pad_head_vec(lp['mla_k_gain'])
    row_tab = lambda i: _prompt_or_sample_block(i, tm)
    q = _mla_q(proj, lp['mla_q_norm'], w_uq, q_gain, cos_mla, sin_mla, row_tab, tm)
    c_kv, k_new, v_new = _mla_kv(proj, lp['mla_kv_norm'], w_ukv, k_gain, kpe, cos_mla, sin_mla, row_tab, tm)
    w_ukv_heads = w_ukv.reshape(MLA_KV_LORA, MLA_HEADS, MLA_NOPE + MLA_V)
    w_uk = w_ukv_heads[:, :, :MLA_NOPE].reshape(MLA_KV_LORA, MLA_HEADS * MLA_NOPE)
    w_uv = w_ukv_heads[:, :, MLA_NOPE:].reshape(MLA_KV_LORA, MLA_HEADS * MLA_V)
    b = _attn_prompt(q, k_new, v_new, BATCH, SEQ, 512)
    b = _attn_sample(q, k_new, v_new, ckv_all, kpe_past_all, layer * DEC_BATCH, cos_past, sin_past, w_uk, w_uv,
                     k_gain, N_PROMPT, into=b)

    zeros_ret = jnp.zeros((BATCH, RET_HEADS, RET_DK, RET_DV), F32)
    c, p_ret = _retention_branch(proj, cos_ret, sin_ret, lambda t: t, zeros_ret, 0, BATCH, SEQ, 256, 0)
    c, s_ret = _retention_branch(proj, cos_ret, sin_ret, lambda t: SEQ // DEC_SEQ, ret_s0_all, layer * DEC_BATCH,
                                 DEC_BATCH, DEC_SEQ, DEC_SEQ, N_PROMPT, into=c)

    merged = _merge(a, b, c, lp['w_branch'].astype(BF16), proj, 1024, 512)
    x = _res_matmul(merged, lp['w_out'].astype(BF16), x, 1024, 1024)

    mem_k_p, mem_v_p = _memkv(mem_prompt, lp['norm_mem'], lp['w_ck'].astype(BF16), lp['w_cv'].astype(BF16),
                              lp['cross_k_gain'], 256)
    qc = _rms_proj(x, lp['norm_cross'], lp['w_cq'].astype(BF16), BF16, 512, MEM_WIDTH)[0]
    o = _cross_attn(qc, mem_k_p.reshape(BATCH, N_MEM, MEM_WIDTH), mem_v_p.reshape(BATCH, N_MEM, MEM_WIDTH), 0,
                    lp['cross_q_gain'], BATCH, SEQ, 512, 0)
    o = _cross_attn(qc, mem_k_all, mem_v_all, layer * DEC_BATCH, lp['cross_q_gain'], DEC_BATCH, DEC_SEQ, DEC_SEQ,
                    N_PROMPT, into=o)
    x = _res_matmul(o, lp['w_co'].astype(BF16), x, 1024, 1024)

    x = _hier_moe(x, lp, w_gate_up, w_down, layer)

    kpe_rows = kpe[:, :MLA_ROPE]
    state_p = (c_kv[:N_PROMPT].reshape(BATCH, SEQ, MLA_KV_LORA), kpe_rows[:N_PROMPT].reshape(BATCH, SEQ, MLA_ROPE),
               p_s5_re, p_s5_im, p_ret,
               mem_k_p.reshape(BATCH, N_MEM, MEM_HEADS, MEM_HEAD_DIM),
               mem_v_p.reshape(BATCH, N_MEM, MEM_HEADS, MEM_HEAD_DIM))
    state_s = (c_kv[N_PROMPT:].reshape(DEC_BATCH, DEC_SEQ, MLA_KV_LORA),
               kpe_rows[N_PROMPT:].reshape(DEC_BATCH, DEC_SEQ, MLA_ROPE), s_s5_re, s_s5_im, s_ret)
    return x, state_p, state_s


_LAYER_PARAMS = (
    'norm_mix', 'w_in', 's5_a_re', 's5_a_im', 's5_log_dt', 's5_b_re', 's5_b_im', 's5_c_re', 's5_c_im', 's5_d',
    's5_w_glu', 'mla_q_norm', 'mla_w_uq', 'mla_kv_norm', 'mla_w_ukv', 'mla_q_gain', 'mla_k_gain', 'w_branch',
    'w_out', 'norm_cross', 'norm_mem', 'w_cq', 'w_ck', 'w_cv', 'cross_q_gain', 'cross_k_gain', 'w_co', 'norm_ffn',
    'w_group_router', 'b_group', 'w_expert_router', 'b_expert')


def kernel(x_prompt, x_sample, mem_prompt, cache_mla_ckv, cache_mla_kpe, state_s5_re, state_s5_im, state_ret,
           cache_mem_k, cache_mem_v, norm_mix, w_in, s5_a_re, s5_a_im, s5_log_dt, s5_b_re, s5_b_im, s5_c_re,
           s5_c_im, s5_d, s5_w_glu, mla_q_norm, mla_w_uq, mla_kv_norm, mla_w_ukv, mla_q_gain, mla_k_gain,
           w_branch, w_out, norm_cross, norm_mem, w_cq, w_ck, w_cv, cross_q_gain, cross_k_gain, w_co, norm_ffn,
           w_group_router, b_group, w_expert_router, b_expert, w_gate_up, w_down):
    params = dict(zip(_LAYER_PARAMS, (
        norm_mix, w_in, s5_a_re, s5_a_im, s5_log_dt, s5_b_re, s5_b_im, s5_c_re, s5_c_im, s5_d, s5_w_glu,
        mla_q_norm, mla_w_uq, mla_kv_norm, mla_w_ukv, mla_q_gain, mla_k_gain, w_branch, w_out, norm_cross,
        norm_mem, w_cq, w_ck, w_cv, cross_q_gain, cross_k_gain, w_co, norm_ffn, w_group_router, b_group,
        w_expert_router, b_expert)))
    assert x_prompt.shape == (BATCH, SEQ, D_MODEL) and x_sample.shape == (DEC_BATCH, DEC_SEQ, D_MODEL)
    assert cache_mla_ckv.shape == (DEPTH, DEC_BATCH, PAST_LEN, MLA_KV_LORA)

    pos_p = jnp.arange(SEQ, dtype=jnp.int32)
    pos_s = PAST_LEN + jnp.arange(DEC_SEQ, dtype=jnp.int32)
    tile_rows = 512
    tables = (_rope_tables(jnp.concatenate([pos_p, jnp.tile(pos_s, tile_rows // DEC_SEQ)]), MLA_ROPE)
              + _rope_tables(jnp.arange(PAST_LEN, dtype=jnp.int32), MLA_ROPE)
              + _rope_tables(jnp.concatenate([pos_p, pos_s]), RET_DK))

    x = jnp.concatenate([x_prompt.reshape(N_PROMPT, D_MODEL), x_sample.reshape(N_SAMPLE, D_MODEL)], axis=0)
    mem2d = mem_prompt.reshape(BATCH * N_MEM, D_MODEL)
    n_past = DEPTH * DEC_BATCH * PAST_LEN
    ckv_all = cache_mla_ckv.reshape(n_past, MLA_KV_LORA)
    kpe_past_all = jnp.pad(cache_mla_kpe.reshape(n_past, MLA_ROPE), ((0, 0), (0, LANES - MLA_ROPE)))
    ret_s0_all = state_ret.reshape(DEPTH * DEC_BATCH, RET_HEADS, RET_DK, RET_DV)
    mem_k_all = cache_mem_k.reshape(DEPTH * DEC_BATCH, N_MEM, MEM_HEADS, MEM_HEAD_DIM)
    mem_v_all = cache_mem_v.reshape(DEPTH * DEC_BATCH, N_MEM, MEM_HEADS, MEM_HEAD_DIM)
    outs_p, outs_s = [], []
    for l in range(DEPTH):
        lp = {name: value[l] for name, value in params.items()}
        caches = (ckv_all, kpe_past_all, state_s5_re[l], state_s5_im[l], ret_s0_all, mem_k_all, mem_v_all,
                  w_gate_up, w_down)
        x, st_p, st_s = _layer(x, mem2d, lp, l, caches, tables)
        outs_p.append(st_p)
        outs_s.append(st_s)
    stack = lambda outs, i: jnp.stack([o[i] for o in outs])
    return ((x[:N_PROMPT].reshape(BATCH, SEQ, D_MODEL), x[N_PROMPT:].reshape(DEC_BATCH, DEC_SEQ, D_MODEL))
            + tuple(stack(outs_p, i) for i in range(7)) + tuple(stack(outs_s, i) for i in range(5)))
```

```python
import functools
import math

import jax
import jax.numpy as jnp
from jax import lax
from jax.experimental import pallas as pl
from jax.experimental.pallas import tpu as pltpu

F32 = jnp.float32
BF16 = jnp.bfloat16

D_MODEL = 2048
BATCH = 4
SEQ = 2048
DEPTH = 2
DEC_BATCH = 32
DEC_SEQ = 32
PAST_LEN = 1024
CHUNK = 64
RMS_EPS = 1e-6
ROPE_THETA = 10000.0
NEG_INF = -1e30

S5_WIDTH = 1024
S5_GROUP_CH = 16
S5_GROUPS = 64
S5_STATE = 64
S5_COLS = S5_GROUPS * S5_STATE

MLA_HEADS = 8
MLA_NOPE = 128
MLA_ROPE = 64
MLA_QK = MLA_NOPE + MLA_ROPE
MLA_V = 128
MLA_Q_LORA = 512
MLA_KV_LORA = 512
MLA_HEAD_PAD = 256
MLA_QK_PAD = MLA_HEADS * MLA_HEAD_PAD

RET_HEADS = 4
RET_DK = 128
RET_DV = 256
BRANCH_WIDTH = 1024
N_BRANCH = 3

N_MEM = 256
MEM_HEADS = 4
MEM_HEAD_DIM = 128
MEM_WIDTH = MEM_HEADS * MEM_HEAD_DIM

MOE_GROUPS = 4
MOE_PER_GROUP = 8
N_EXPERTS = 32
MOE_TOP_K = 2
EXPERT_FF = 512

N_PROMPT = BATCH * SEQ
N_SAMPLE = DEC_BATCH * DEC_SEQ
N_ROWS = N_PROMPT + N_SAMPLE

COL_GMIX = 0
COL_S5 = 6144
COL_RV = 7168
COL_RG = 8192
COL_QLAT = 9216
COL_KVLAT = 9728
COL_RQ = 10240
COL_RK = 10752
PROJ_COLS = 11264
IN_S5 = (0, 1024)
IN_QLAT = (1024, 1536)
IN_KVLAT = (1536, 2048)
IN_KPE = (2048, 2112)
IN_RQ = (2112, 2624)
IN_RK = (2624, 3136)
IN_RV = (3136, 4160)
IN_RG = (4160, 5184)
IN_GMIX = (5184, 11328)

LANES = 128
EXPERT_ROWS = 256
N_PAIRS = N_ROWS * MOE_TOP_K
N_EXPERT_BLOCKS = (N_PAIRS + N_EXPERTS * (EXPERT_ROWS - 1)) // EXPERT_ROWS + 1
VMEM_LIMIT = 56 * 1024 * 1024


def _cparams(sem):
    return pltpu.CompilerParams(dimension_semantics=sem, vmem_limit_bytes=VMEM_LIMIT)


def _rms_rows(x, g):
    r = lax.rsqrt(jnp.mean(x * x, axis=-1, keepdims=True) + RMS_EPS)
    return (x * r) * g


def _dot(a, b):
    return jnp.dot(a, b, preferred_element_type=F32)


def _rows_call(kernel_fn, into, *, in_specs, args, **kw):
    if into is None:
        out0 = kw['out_shape'][0] if isinstance(kw['out_shape'], (list, tuple)) else kw['out_shape']
        into = jnp.zeros(out0.shape, out0.dtype)

    def aliased_kernel(into_ref, *refs):
        del into_ref
        kernel_fn(*refs)

    return pl.pallas_call(aliased_kernel, in_specs=[pl.BlockSpec(memory_space=pl.ANY)] + in_specs,
                          input_output_aliases={0: 0}, **kw)(into, *args)


def _dot_nt(a, b):
    return lax.dot_general(a, b, (((1,), (1,)), ((), ())), preferred_element_type=F32)


def _rms_proj_kernel(x_ref, g_ref, w_ref, *refs, side):
    if side:
        w2_ref, o_ref, o2_ref, xn_ref = refs
    else:
        o_ref, xn_ref = refs

    @pl.when(pl.program_id(1) == 0)
    def _():
        xn_ref[...] = _rms_rows(x_ref[...], g_ref[...]).astype(BF16)
        if side:
            o2_ref[...] = _dot(xn_ref[...], w2_ref[...]).astype(o2_ref.dtype)

    o_ref[...] = _dot(xn_ref[...], w_ref[...]).astype(o_ref.dtype)


def _rms_proj(x, g, w, out_dtype, tm, tn, w2=None, out2_dtype=None):
    n, k = x.shape
    cols = w.shape[1]
    side = w2 is not None
    in_specs = [
        pl.BlockSpec((tm, k), lambda i, j: (i, 0)),
        pl.BlockSpec((1, k), lambda i, j: (0, 0)),
        pl.BlockSpec((k, tn), lambda i, j: (0, j)),
    ]
    out_specs = [pl.BlockSpec((tm, tn), lambda i, j: (i, j))]
    out_shape = [jax.ShapeDtypeStruct((n, cols), out_dtype)]
    args = [x, g.reshape(1, k), w]
    if side:
        cols2 = w2.shape[1]
        in_specs.append(pl.BlockSpec((k, cols2), lambda i, j: (0, 0)))
        out_specs.append(pl.BlockSpec((tm, cols2), lambda i, j: (i, 0)))
        out_shape.append(jax.ShapeDtypeStruct((n, cols2), out2_dtype))
        args.append(w2)
    return pl.pallas_call(
        functools.partial(_rms_proj_kernel, side=side),
        grid=(n // tm, cols // tn),
        in_specs=in_specs,
        out_specs=out_specs,
        out_shape=out_shape,
        scratch_shapes=[pltpu.VMEM((tm, k), BF16)],
        compiler_params=_cparams(("parallel", "arbitrary")),
        name="rms_proj_side" if side else "rms_proj",
    )(*args)


def _res_matmul_kernel(a_ref, w_ref, r_ref, o_ref):
    o_ref[...] = r_ref[...] + _dot(a_ref[...], w_ref[...])


def _res_matmul(a, w, res, tm, tn):
    n, k = a.shape
    cols = w.shape[1]
    return pl.pallas_call(
        _res_matmul_kernel,
        grid=(n // tm, cols // tn),
        in_specs=[
            pl.BlockSpec((tm, k), lambda i, j: (i, 0)),
            pl.BlockSpec((k, tn), lambda i, j: (0, j)),
            pl.BlockSpec((tm, tn), lambda i, j: (i, j)),
        ],
        out_specs=pl.BlockSpec((tm, tn), lambda i, j: (i, j)),
        out_shape=jax.ShapeDtypeStruct((n, cols), F32),
        compiler_params=_cparams(("parallel", "arbitrary")),
        name="res_matmul",
    )(a, w, res)


S5_K_SLAB = 256
S5_N_SLAB = 1024
S5_SLABS = S5_WIDTH // S5_K_SLAB
S5_SCAN_COLS = 1024
S5_SCAN_ROWS = 8


def _gelu_tanh(x):
    return 0.5 * x * (1.0 + jnp.tanh(math.sqrt(2.0 / math.pi) * (x + 0.044715 * (x * x * x))))


def _sigmoid(x):
    return 1.0 / (1.0 + jnp.exp(-x))


def _s5_kernel(u_ref, bre_ref, bim_ref, cre_ref, cim_ref, pw_ref, d_ref, wglu_ref,
               h0r_ref, h0i_ref, o_ref, hr_out_ref, hi_out_ref, sre_ref, sim_ref, cr_ref, ci_ref, *, seqs, seq_rows):
    @pl.when(pl.program_id(1) == 0)
    def _():
        cr_ref[...] = h0r_ref[...]
        ci_ref[...] = h0i_ref[...]

    u = u_ref[...]
    for n in range(S5_SLABS):
        un = u[:, S5_K_SLAB * n:S5_K_SLAB * (n + 1)]
        sre_ref[:, S5_N_SLAB * n:S5_N_SLAB * (n + 1)] = _dot(un, bre_ref[n])
        sim_ref[:, S5_N_SLAB * n:S5_N_SLAB * (n + 1)] = _dot(un, bim_ref[n])

    for c, q in [(c, q) for c in range(S5_COLS // S5_SCAN_COLS) for q in range(seqs)]:
        sl = slice(c * S5_SCAN_COLS, (c + 1) * S5_SCAN_COLS)

        def body(blk, carry, sl=sl, q=q):
            cr, ci = carry
            rows = pl.ds(pl.multiple_of(q * seq_rows + blk * S5_SCAN_ROWS, S5_SCAN_ROWS), S5_SCAN_ROWS)
            xr = sre_ref[rows, sl]
            xi = sim_ref[rows, sl]
            for j in range(3):
                kr = pw_ref[j, 0, :, sl]
                ki = pw_ref[j, 1, :, sl]
                sr = pltpu.roll(xr, 1 << j, 0)
                si = pltpu.roll(xi, 1 << j, 0)
                xr, xi = xr + (kr * sr - ki * si), xi + (kr * si + ki * sr)
            pr = pw_ref[3, 0, :, sl]
            pi = pw_ref[3, 1, :, sl]
            hr = xr + (pr * cr - pi * ci)
            hi = xi + (pr * ci + pi * cr)
            sre_ref[rows, sl] = hr
            sim_ref[rows, sl] = hi
            last = S5_SCAN_ROWS - 1
            return (jnp.broadcast_to(hr[last:last + 1], hr.shape), jnp.broadcast_to(hi[last:last + 1], hi.shape))

        init = (jnp.broadcast_to(cr_ref[q, :, sl], (S5_SCAN_ROWS, S5_SCAN_COLS)),
                jnp.broadcast_to(ci_ref[q, :, sl], (S5_SCAN_ROWS, S5_SCAN_COLS)))
        hr, hi = lax.fori_loop(0, seq_rows // S5_SCAN_ROWS, body, init)
        cr_ref[q, :, sl] = hr[0:1]
        ci_ref[q, :, sl] = hi[0:1]

    ys = []
    for n in range(S5_SLABS):
        hr_n = sre_ref[:, S5_N_SLAB * n:S5_N_SLAB * (n + 1)].astype(BF16)
        hi_n = sim_ref[:, S5_N_SLAB * n:S5_N_SLAB * (n + 1)].astype(BF16)
        ys.append(_dot(hr_n, cre_ref[n]) + _dot(hi_n, cim_ref[n]))
    y = jnp.concatenate(ys, axis=1) + d_ref[...] * u.astype(F32)
    z = _gelu_tanh(y)
    gate = _sigmoid(_dot(z.astype(BF16), wglu_ref[...]))
    o_ref[...] = (z * gate).astype(o_ref.dtype)
    hr_out_ref[...] = cr_ref[...]
    hi_out_ref[...] = ci_ref[...]


def _s5_tables(lp):
    a_re, a_im = lp['s5_a_re'], lp['s5_a_im']
    dt = jnp.exp(lp['s5_log_dt'])[:, None]
    mag = jnp.exp(a_re * dt)
    lb_re = mag * jnp.cos(a_im * dt)
    lb_im = mag * jnp.sin(a_im * dt)
    den = a_re * a_re + a_im * a_im
    n_re = lb_re - 1.0
    co_re = (n_re * a_re + lb_im * a_im) / den
    co_im = (lb_im * a_re - n_re * a_im) / den
    bb_re = co_re[..., None] * lp['s5_b_re'] - co_im[..., None] * lp['s5_b_im']
    bb_im = co_re[..., None] * lp['s5_b_im'] + co_im[..., None] * lp['s5_b_re']
    per_slab = S5_GROUPS // S5_SLABS
    eye = jnp.eye(per_slab, dtype=F32)

    def b_tiles(bb):
        bb = bb.reshape(S5_SLABS, per_slab, S5_STATE, S5_GROUP_CH)
        return (bb.transpose(0, 1, 3, 2)[:, :, :, None, :] * eye[None, :, None, :, None]).reshape(
            S5_SLABS, S5_K_SLAB, S5_N_SLAB).astype(BF16)

    def c_tiles(cc):
        cc = cc.reshape(S5_SLABS, per_slab, S5_GROUP_CH, S5_STATE)
        return (cc.transpose(0, 1, 3, 2)[:, :, :, None, :] * eye[None, :, None, :, None]).reshape(
            S5_SLABS, S5_N_SLAB, S5_K_SLAB).astype(BF16)

    def lam_pow(k):
        m = jnp.exp(k * (a_re * dt).reshape(1, S5_COLS))
        ang = k * (a_im * dt).reshape(1, S5_COLS)
        return jnp.stack([m * jnp.cos(ang), m * jnp.sin(ang)])

    t = jnp.arange(S5_SCAN_ROWS, dtype=F32)[:, None]
    steps = [jnp.where(t >= float(1 << j), lam_pow(jnp.full_like(t, float(1 << j))), 0.0) for j in range(3)]
    pw = jnp.stack(steps + [lam_pow(t + 1.0)])

    return dict(
        bre=b_tiles(bb_re), bim=b_tiles(bb_im),
        cre=c_tiles(lp['s5_c_re']), cim=c_tiles(-lp['s5_c_im']),
        pw=pw, d=lp['s5_d'].reshape(1, S5_WIDTH),
    )


def _s5_branch(proj, tabs, wglu, h0_re, h0_im, n_seq, seq_len, tt, row0, into=None):
    seqs = max(tt // seq_len, 1)
    seq_rows = min(tt, seq_len)
    nt = seq_len // seq_rows
    base = row0 // tt
    col = COL_S5 // S5_WIDTH
    const3 = lambda s, t: (0, 0, 0)
    const2 = lambda s, t: (0, 0)
    state = pl.BlockSpec((seqs, 1, S5_COLS), lambda s, t: (s, 0, 0))
    out, hr, hi = _rows_call(
        functools.partial(_s5_kernel, seqs=seqs, seq_rows=seq_rows), into,
        grid=(n_seq // seqs, nt),
        in_specs=[
            pl.BlockSpec((tt, S5_WIDTH), lambda s, t: (base + s * nt + t, col)),
            pl.BlockSpec((S5_SLABS, S5_K_SLAB, S5_N_SLAB), const3),
            pl.BlockSpec((S5_SLABS, S5_K_SLAB, S5_N_SLAB), const3),
            pl.BlockSpec((S5_SLABS, S5_N_SLAB, S5_K_SLAB), const3),
            pl.BlockSpec((S5_SLABS, S5_N_SLAB, S5_K_SLAB), const3),
            pl.BlockSpec((4, 2, S5_SCAN_ROWS, S5_COLS), lambda s, t: (0, 0, 0, 0)),
            pl.BlockSpec((1, S5_WIDTH), const2),
            pl.BlockSpec((S5_WIDTH, S5_WIDTH), const2),
            state, state,
        ],
        out_specs=[pl.BlockSpec((tt, S5_WIDTH), lambda s, t: (base + s * nt + t, 0)), state, state],
        out_shape=[
            jax.ShapeDtypeStruct((N_ROWS, S5_WIDTH), BF16),
            jax.ShapeDtypeStruct((n_seq, 1, S5_COLS), F32),
            jax.ShapeDtypeStruct((n_seq, 1, S5_COLS), F32),
        ],
        scratch_shapes=[
            pltpu.VMEM((tt, S5_COLS), F32), pltpu.VMEM((tt, S5_COLS), F32),
            pltpu.VMEM((seqs, 1, S5_COLS), F32), pltpu.VMEM((seqs, 1, S5_COLS), F32),
        ],
        compiler_params=_cparams(("parallel", "arbitrary")),
        name="s5_branch",
        args=(proj, tabs['bre'], tabs['bim'], tabs['cre'], tabs['cim'], tabs['pw'], tabs['d'], wglu,
              h0_re.reshape(n_seq, 1, S5_COLS), h0_im.reshape(n_seq, 1, S5_COLS)),
    )
    return (out, hr.reshape(n_seq, S5_GROUPS, S5_STATE), hi.reshape(n_seq, S5_GROUPS, S5_STATE))


def _rope_tables(pos, d):
    inv_freq = ROPE_THETA ** (-jnp.arange(0, d, 2, dtype=F32) / d)
    ang = pos.astype(F32)[:, None] * inv_freq[None, :]
    cos, sin = jnp.cos(ang), jnp.sin(ang)
    pad = jnp.zeros((pos.shape[0], LANES - d), F32)
    return (jnp.concatenate([cos, cos, pad], axis=1), jnp.concatenate([-sin, sin, pad], axis=1))


def _pad_head_vec(g):
    return jnp.concatenate([g, jnp.zeros((MLA_HEAD_PAD - MLA_QK,), F32)]).reshape(1, MLA_HEAD_PAD)


def _qk_head(nope, pe, gain, cos2, sin2, scale):
    ss = jnp.sum(nope * nope, axis=-1, keepdims=True) + jnp.sum(pe * pe, axis=-1, keepdims=True)
    r = lax.rsqrt(ss * (1.0 / MLA_QK) + RMS_EPS)
    nope = (nope * r) * gain[:, :MLA_NOPE]
    pe = (pe * r) * gain[:, MLA_NOPE:]
    lane = lax.broadcasted_iota(jnp.int32, pe.shape, 1)
    half = MLA_ROPE // 2
    swap = jnp.where(lane < half, pltpu.roll(pe, LANES - half, 1), pltpu.roll(pe, half, 1))
    pe = pe * cos2 + swap * sin2
    return nope * scale, pe * scale


def _mla_q_kernel(lat_ref, g_ref, w_ref, gain_ref, cos_ref, sin_ref, o_ref):
    xn = _rms_rows(lat_ref[...].astype(F32), g_ref[...]).astype(BF16)
    q = _dot(xn, w_ref[...])
    gain = gain_ref[...]
    cos2 = cos_ref[...]
    sin2 = sin_ref[...]
    for h in range(MLA_HEADS):
        c0 = h * MLA_HEAD_PAD
        nope, pe = _qk_head(q[:, c0:c0 + MLA_NOPE], q[:, c0 + MLA_NOPE:c0 + MLA_HEAD_PAD],
                            gain, cos2, sin2, MLA_QK ** -0.5)
        o_ref[:, c0:c0 + MLA_NOPE] = nope.astype(BF16)
        o_ref[:, c0 + MLA_NOPE:c0 + MLA_HEAD_PAD] = pe.astype(BF16)


def _mla_q(proj, g, w_uq_pad, gain_pad, cos2, sin2, tab_block, tm):
    n = proj.shape[0]
    col = COL_QLAT // MLA_Q_LORA
    c2 = lambda i: (0, 0)
    tab = lambda i: (tab_block(i), 0)
    return pl.pallas_call(
        _mla_q_kernel,
        grid=(n // tm,),
        in_specs=[
            pl.BlockSpec((tm, MLA_Q_LORA), lambda i: (i, col)),
            pl.BlockSpec((1, MLA_Q_LORA), c2),
            pl.BlockSpec((MLA_Q_LORA, MLA_QK_PAD), c2),
            pl.BlockSpec((1, MLA_HEAD_PAD), c2),
            pl.BlockSpec((tm, LANES), tab),
            pl.BlockSpec((tm, LANES), tab),
        ],
        out_specs=pl.BlockSpec((tm, MLA_QK_PAD), lambda i: (i, 0)),
        out_shape=jax.ShapeDtypeStruct((n, MLA_QK_PAD), BF16),
        compiler_params=_cparams(("parallel",)),
        name="mla_q",
    )(proj, g.reshape(1, -1), w_uq_pad, gain_pad, cos2, sin2)


def _mla_kv_kernel(lat_ref, g_ref, w_ref, gain_ref, kpe_ref, cos_ref, sin_ref, ckv_ref, k_ref, v_ref):
    lat = _rms_rows(lat_ref[...].astype(F32), g_ref[...])
    ckv_ref[...] = lat
    kv = _dot(lat.astype(BF16), w_ref[...])
    gain = gain_ref[...]
    kpe = kpe_ref[...]
    cos2 = cos_ref[...]
    sin2 = sin_ref[...]
    for h in range(MLA_HEADS):
        c0 = h * (MLA_NOPE + MLA_V)
        nope, pe = _qk_head(kv[:, c0:c0 + MLA_NOPE], kpe, gain, cos2, sin2, 1.0)
        k0 = h * MLA_HEAD_PAD
        k_ref[:, k0:k0 + MLA_NOPE] = nope.astype(BF16)
        k_ref[:, k0 + MLA_NOPE:k0 + MLA_HEAD_PAD] = pe.astype(BF16)
        v_ref[:, h * MLA_V:(h + 1) * MLA_V] = kv[:, c0 + MLA_NOPE:c0 + MLA_NOPE + MLA_V].astype(BF16)


def _mla_kv(proj, g, w_ukv, gain_pad, kpe_pad, cos2, sin2, tab_block, tm):
    n = proj.shape[0]
    c2 = lambda i: (0, 0)
    row = lambda i: (i, 0)
    tab = lambda i: (tab_block(i), 0)
    return pl.pallas_call(
        _mla_kv_kernel,
        grid=(n // tm,),
        in_specs=[
            pl.BlockSpec((tm, MLA_KV_LORA), lambda i: (i, COL_KVLAT // MLA_KV_LORA)),
            pl.BlockSpec((1, MLA_KV_LORA), c2),
            pl.BlockSpec((MLA_KV_LORA, MLA_HEADS * (MLA_NOPE + MLA_V)), c2),
            pl.BlockSpec((1, MLA_HEAD_PAD), c2),
            pl.BlockSpec((tm, LANES), row),
            pl.BlockSpec((tm, LANES), tab),
            pl.BlockSpec((tm, LANES), tab),
        ],
        out_specs=[pl.BlockSpec((tm, MLA_KV_LORA), row), pl.BlockSpec((tm, MLA_QK_PAD), row),
                   pl.BlockSpec((tm, MLA_HEADS * MLA_V), row)],
        out_shape=[jax.ShapeDtypeStruct((n, MLA_KV_LORA), F32), jax.ShapeDtypeStruct((n, MLA_QK_PAD), BF16),
                   jax.ShapeDtypeStruct((n, MLA_HEADS * MLA_V), BF16)],
        compiler_params=_cparams(("parallel",)),
        name="mla_kv",
    )(proj, g.reshape(1, -1), w_ukv, gain_pad, kpe_pad, cos2, sin2)


ATTN_HEADS_PER_STEP = 2


def _attn_prompt_kernel(q_ref, k_ref, v_ref, o_ref, *, tq):
    qi = pl.program_id(2)

    def key_tile(ki, state, diagonal):
        rows = pl.ds(pl.multiple_of(ki * tq, tq), tq)
        new_state = []
        for h in range(ATTN_HEADS_PER_STEP):
            qk = slice(h * MLA_HEAD_PAD, (h + 1) * MLA_HEAD_PAD)
            vs = slice(h * MLA_V, (h + 1) * MLA_V)
            s = _dot_nt(q_ref[:, qk], k_ref[rows, qk])
            if diagonal:
                row_chunk = lax.broadcasted_iota(jnp.int32, s.shape, 0) // CHUNK
                col_chunk = lax.broadcasted_iota(jnp.int32, s.shape, 1) // CHUNK
                s = jnp.where(col_chunk <= row_chunk, s, NEG_INF)
            m_old, l_old, acc_old = state[h]
            m_new = jnp.maximum(m_old, jnp.max(s, axis=-1, keepdims=True))
            alpha = jnp.exp(m_old - m_new)
            p = jnp.exp(s - m_new)
            l_new = alpha * l_old + jnp.sum(p, axis=-1, keepdims=True)
            acc_new = alpha * acc_old + _dot(p.astype(BF16), v_ref[rows, vs])
            new_state.append((m_new, l_new, acc_new))
        return tuple(new_state)

    init = tuple((jnp.full((tq, 1), NEG_INF, F32), jnp.zeros((tq, 1), F32), jnp.zeros((tq, MLA_V), F32))
                 for _ in range(ATTN_HEADS_PER_STEP))
    state = lax.fori_loop(0, qi, lambda ki, st: key_tile(ki, st, False), init)
    state = key_tile(qi, state, True)
    for h in range(ATTN_HEADS_PER_STEP):
        _, l_fin, acc_fin = state[h]
        o_ref[:, h * MLA_V:(h + 1) * MLA_V] = (acc_fin / l_fin).astype(o_ref.dtype)


def _attn_prompt(q, k, v, n_batch, seq_len, tq):
    assert tq % CHUNK == 0
    nq = seq_len // tq
    hp = ATTN_HEADS_PER_STEP
    return _rows_call(
        functools.partial(_attn_prompt_kernel, tq=tq), None,
        grid=(n_batch, MLA_HEADS // hp, nq),
        in_specs=[
            pl.BlockSpec((tq, hp * MLA_HEAD_PAD), lambda b, h, qi: (b * nq + qi, h)),
            pl.BlockSpec((seq_len, hp * MLA_HEAD_PAD), lambda b, h, qi: (b, h)),
            pl.BlockSpec((seq_len, hp * MLA_V), lambda b, h, qi: (b, h)),
        ],
        out_specs=pl.BlockSpec((tq, hp * MLA_V), lambda b, h, qi: (b * nq + qi, h)),
        out_shape=jax.ShapeDtypeStruct((N_ROWS, MLA_HEADS * MLA_V), BF16),
        compiler_params=_cparams(("parallel", "parallel", "arbitrary")),
        name="attn_prompt",
        args=(q, k, v),
    )


def _attn_sample_kernel(q_ref, kn_ref, vn_ref, c_ref, kpe_ref, cos_ref, sin_ref, wk_ref, wv_ref, gain_ref, sel_ref,
                        o_ref):
    c = c_ref[...].astype(BF16)
    k_raw = _dot(c, wk_ref[...])
    v_all = _dot(c, wv_ref[...]).astype(BF16)
    gain = gain_ref[...]
    gain_n = gain[:, :MLA_NOPE]
    kpe = kpe_ref[...]
    pe = kpe * gain[:, MLA_NOPE:]
    lane = lax.broadcasted_iota(jnp.int32, pe.shape, 1)
    half = MLA_ROPE // 2
    swap = jnp.where(lane < half, pltpu.roll(pe, LANES - half, 1), pltpu.roll(pe, half, 1))
    pe = (pe * cos_ref[...] + swap * sin_ref[...]).astype(BF16)
    ss = (_dot_nt(sel_ref[...], (k_raw * k_raw).astype(BF16))
          + _dot_nt(jnp.ones((MLA_HEADS, LANES), BF16), (kpe * kpe).astype(BF16)))
    r = lax.rsqrt(ss * (1.0 / MLA_QK) + RMS_EPS)
    k_raw = k_raw.astype(BF16)
    for h in range(MLA_HEADS):
        ks = slice(h * MLA_HEAD_PAD, (h + 1) * MLA_HEAD_PAD)
        vs = slice(h * MLA_V, (h + 1) * MLA_V)
        q = q_ref[:, ks]
        q_n = (q[:, :MLA_NOPE].astype(F32) * gain_n).astype(BF16)
        s_p = (_dot_nt(q_n, k_raw[:, vs]) + _dot_nt(q[:, MLA_NOPE:], pe)) * r[h:h + 1]
        s_n = _dot_nt(q, kn_ref[:, ks])
        m = jnp.maximum(jnp.max(s_p, axis=-1, keepdims=True), jnp.max(s_n, axis=-1, keepdims=True))
        p_p = jnp.exp(s_p - m)
        p_n = jnp.exp(s_n - m)
        l = jnp.sum(p_p, axis=-1, keepdims=True) + jnp.sum(p_n, axis=-1, keepdims=True)
        o = _dot(p_p.astype(BF16), v_all[:, vs]) + _dot(p_n.astype(BF16), vn_ref[:, vs])
        o_ref[:, vs] = (o / l).astype(o_ref.dtype)


def _attn_sample(q, k_new, v_new, ckv_all, kpe_all, past0, cos, sin, w_uk, w_uv, gain_pad, row0, into):
    assert (PAST_LEN + DEC_SEQ - 1) // CHUNK <= PAST_LEN // CHUNK
    base = row0 // DEC_SEQ
    new = lambda b: (base + b, 0)
    past = lambda b: (past0 + b, 0)
    c2 = lambda b: (0, 0)
    width = MLA_HEADS * MLA_NOPE
    sel = jnp.repeat(jnp.eye(MLA_HEADS, dtype=F32), MLA_NOPE, axis=1).astype(BF16)
    return _rows_call(
        _attn_sample_kernel, into,
        grid=(DEC_BATCH,),
        in_specs=[
            pl.BlockSpec((DEC_SEQ, MLA_QK_PAD), new),
            pl.BlockSpec((DEC_SEQ, MLA_QK_PAD), new),
            pl.BlockSpec((DEC_SEQ, MLA_HEADS * MLA_V), new),
            pl.BlockSpec((PAST_LEN, MLA_KV_LORA), past),
            pl.BlockSpec((PAST_LEN, LANES), past),
            pl.BlockSpec((PAST_LEN, LANES), c2),
            pl.BlockSpec((PAST_LEN, LANES), c2),
            pl.BlockSpec((MLA_KV_LORA, width), c2),
            pl.BlockSpec((MLA_KV_LORA, MLA_HEADS * MLA_V), c2),
            pl.BlockSpec((1, MLA_HEAD_PAD), c2),
            pl.BlockSpec((MLA_HEADS, width), c2),
        ],
        out_specs=pl.BlockSpec((DEC_SEQ, MLA_HEADS * MLA_V), new),
        out_shape=jax.ShapeDtypeStruct((N_ROWS, MLA_HEADS * MLA_V), BF16),
        compiler_params=_cparams(("parallel",)),
        name="attn_sample",
        args=(q, k_new, v_new, ckv_all, kpe_all, cos, sin, w_uk, w_uv, gain_pad, sel),
    )


def _retention_tables(block):
    log_g = jnp.log1p(-jnp.exp2(-5.0 - jnp.arange(RET_HEADS, dtype=F32)))
    idx = jnp.arange(block, dtype=F32)
    diff = idx[:, None] - idx[None, :]
    decay = jnp.where(diff >= 0, jnp.exp(log_g[:, None, None] * jnp.maximum(diff, 0.0)), 0.0)
    q_decay = jnp.exp(log_g[:, None] * (idx + 1.0))
    k_decay = jnp.exp(log_g[:, None] * (block - 1.0 - idx))
    blk_decay = jnp.exp(log_g * block)
    return (decay,
            jnp.broadcast_to(q_decay[:, :, None], (RET_HEADS, block, RET_DK)),
            jnp.broadcast_to(k_decay[:, :, None], (RET_HEADS, block, RET_DK)),
            jnp.broadcast_to(blk_decay[:, None, None], (RET_HEADS, 1, RET_DV)))


def _rope128(x, cos, sin):
    return x * cos + pltpu.roll(x, RET_DK // 2, 1) * sin


def _retention_kernel(rq_ref, rk_ref, rv_ref, rg_ref, cos_ref, sin_ref, dec_ref, qd_ref, kd_ref, bd_ref, s0_ref,
                      o_ref, s_out_ref, s_ref):
    @pl.when(pl.program_id(1) == 0)
    def _():
        s_ref[...] = s0_ref[0]

    cos = cos_ref[...]
    sin = sin_ref[...]
    for h in range(RET_HEADS):
        ks = slice(h * RET_DK, (h + 1) * RET_DK)
        vs = slice(h * RET_DV, (h + 1) * RET_DV)
        q = _rope128(rq_ref[:, ks].astype(F32), cos, sin) * (RET_DK ** -0.5)
        k = _rope128(rk_ref[:, ks].astype(F32), cos, sin)
        v = rv_ref[:, vs]
        s_old = s_ref[h]
        att = _dot_nt(q.astype(BF16), k.astype(BF16)) * dec_ref[h]
        o = _dot(att.astype(BF16), v) + _dot((q * qd_ref[h]).astype(BF16), s_old.astype(BF16))
        kd = (k * kd_ref[h]).astype(BF16)
        s_new = bd_ref[h] * s_old + lax.dot_general(kd, v, (((0,), (0,)), ((), ())), preferred_element_type=F32)
        s_ref[h] = s_new
        s_out_ref[0, h] = s_new
        mu = jnp.mean(o, axis=-1, keepdims=True)
        c = o - mu
        o = c * lax.rsqrt(jnp.mean(c * c, axis=-1, keepdims=True) + RMS_EPS)
        g = rg_ref[:, vs].astype(F32)
        o_ref[:, vs] = (o * (g * _sigmoid(g))).astype(o_ref.dtype)


def _retention_branch(proj, cos, sin, tab_block, s0, s0_base, n_seq, seq_len, block, row0, into=None):
    nblk = seq_len // block
    base = row0 // block
    dec, qd, kd, bd = _retention_tables(block)
    rows = lambda b, t: base + b * nblk + t
    tab = lambda b, t: (tab_block(t), 0)
    all3 = lambda b, t: (0, 0, 0)
    qk_w = RET_HEADS * RET_DK
    v_w = RET_HEADS * RET_DV
    out, s_last = _rows_call(
        _retention_kernel, into,
        grid=(n_seq, nblk),
        in_specs=[
            pl.BlockSpec((block, qk_w), lambda b, t: (rows(b, t), COL_RQ // qk_w)),
            pl.BlockSpec((block, qk_w), lambda b, t: (rows(b, t), COL_RK // qk_w)),
            pl.BlockSpec((block, v_w), lambda b, t: (rows(b, t), COL_RV // v_w)),
            pl.BlockSpec((block, v_w), lambda b, t: (rows(b, t), COL_RG // v_w)),
            pl.BlockSpec((block, LANES), tab),
            pl.BlockSpec((block, LANES), tab),
            pl.BlockSpec((RET_HEADS, block, block), all3),
            pl.BlockSpec((RET_HEADS, block, RET_DK), all3),
            pl.BlockSpec((RET_HEADS, block, RET_DK), all3),
            pl.BlockSpec((RET_HEADS, 1, RET_DV), all3),
            pl.BlockSpec((1, RET_HEADS, RET_DK, RET_DV), lambda b, t: (s0_base + b, 0, 0, 0)),
        ],
        out_specs=[
            pl.BlockSpec((block, v_w), lambda b, t: (rows(b, t), 0)),
            pl.BlockSpec((1, RET_HEADS, RET_DK, RET_DV), lambda b, t: (b, 0, 0, 0)),
        ],
        out_shape=[
            jax.ShapeDtypeStruct((N_ROWS, v_w), BF16),
            jax.ShapeDtypeStruct((n_seq, RET_HEADS, RET_DK, RET_DV), F32),
        ],
        scratch_shapes=[pltpu.VMEM((RET_HEADS, RET_DK, RET_DV), F32)],
        compiler_params=_cparams(("parallel", "arbitrary")),
        name="retention",
        args=(proj, proj, proj, proj, cos, sin, dec, qd, kd, bd, s0),
    )
    return out, s_last


def _merge_kernel(a_ref, b_ref, c_ref, w_ref, g0_ref, g1_ref, g2_ref, o_ref):
    acc = _sigmoid(g0_ref[...].astype(F32)) * _dot(a_ref[...], w_ref[0])
    acc += _sigmoid(g1_ref[...].astype(F32)) * _dot(b_ref[...], w_ref[1])
    acc += _sigmoid(g2_ref[...].astype(F32)) * _dot(c_ref[...], w_ref[2])
    o_ref[...] = acc.astype(o_ref.dtype)


def _merge(o_a, o_b, o_c, w_branch, proj, tm, tn):
    n = o_a.shape[0]
    nj = D_MODEL // tn
    br = pl.BlockSpec((tm, BRANCH_WIDTH), lambda i, j: (i, 0))
    gate = lambda b: pl.BlockSpec((tm, tn), lambda i, j: (i, COL_GMIX // tn + b * nj + j))
    return pl.pallas_call(
        _merge_kernel,
        grid=(n // tm, nj),
        in_specs=[br, br, br,
                  pl.BlockSpec((N_BRANCH, BRANCH_WIDTH, tn), lambda i, j: (0, 0, j)),
                  gate(0), gate(1), gate(2)],
        out_specs=pl.BlockSpec((tm, tn), lambda i, j: (i, j)),
        out_shape=jax.ShapeDtypeStruct((n, D_MODEL), BF16),
        compiler_params=_cparams(("parallel", "arbitrary")),
        name="branch_merge",
    )(o_a, o_b, o_c, w_branch, proj, proj, proj)


def _memkv_kernel(m_ref, g_ref, wk_ref, wv_ref, kg_ref, k_ref, v_ref):
    mn = _rms_rows(m_ref[...], g_ref[...]).astype(BF16)
    kk = _dot(mn, wk_ref[...])
    for h in range(MEM_HEADS):
        sl = slice(h * MEM_HEAD_DIM, (h + 1) * MEM_HEAD_DIM)
        k_ref[:, sl] = _rms_rows(kk[:, sl], kg_ref[...])
    v_ref[...] = _dot(mn, wv_ref[...])


def _memkv(mem, g, wk, wv, k_gain, tm):
    n, k = mem.shape
    c2 = lambda i: (0, 0)
    return pl.pallas_call(
        _memkv_kernel,
        grid=(n // tm,),
        in_specs=[
            pl.BlockSpec((tm, k), lambda i: (i, 0)),
            pl.BlockSpec((1, k), c2),
            pl.BlockSpec((k, MEM_WIDTH), c2),
            pl.BlockSpec((k, MEM_WIDTH), c2),
            pl.BlockSpec((1, MEM_HEAD_DIM), c2),
        ],
        out_specs=[pl.BlockSpec((tm, MEM_WIDTH), lambda i: (i, 0)), pl.BlockSpec((tm, MEM_WIDTH), lambda i: (i, 0))],
        out_shape=[jax.ShapeDtypeStruct((n, MEM_WIDTH), F32), jax.ShapeDtypeStruct((n, MEM_WIDTH), F32)],
        compiler_params=_cparams(("parallel",)),
        name="memory_kv",
    )(mem, g.reshape(1, k), wk, wv, k_gain.reshape(1, MEM_HEAD_DIM))


def _cross_attn_kernel(q_ref, k_ref, v_ref, qg_ref, o_ref, *, head_axis):
    for h in range(MEM_HEADS):
        sl = slice(h * MEM_HEAD_DIM, (h + 1) * MEM_HEAD_DIM)
        if head_axis:
            k, v = k_ref[0, :, h, :], v_ref[0, :, h, :]
        else:
            k, v = k_ref[0, :, sl], v_ref[0, :, sl]
        q = _rms_rows(q_ref[:, sl].astype(F32), qg_ref[...]) * (MEM_HEAD_DIM ** -0.5)
        s = _dot_nt(q.astype(BF16), k.astype(BF16))
        p = jnp.exp(s - jnp.max(s, axis=-1, keepdims=True))
        l = jnp.sum(p, axis=-1, keepdims=True)
        o = _dot(p.astype(BF16), v.astype(BF16))
        o_ref[:, sl] = (o / l).astype(o_ref.dtype)


def _cross_attn(qc, mem_k, mem_v, mem0, q_gain, n_batch, rows_per_batch, tm, row0, into=None):
    tiles = rows_per_batch // tm
    base = row0 // tm
    head_axis = mem_k.ndim == 4
    if head_axis:
        mem_spec = pl.BlockSpec((1, N_MEM, MEM_HEADS, MEM_HEAD_DIM), lambda b, t: (mem0 + b, 0, 0, 0))
    else:
        mem_spec = pl.BlockSpec((1, N_MEM, MEM_WIDTH), lambda b, t: (mem0 + b, 0, 0))
    return _rows_call(
        functools.partial(_cross_attn_kernel, head_axis=head_axis), into,
        grid=(n_batch, tiles),
        in_specs=[
            pl.BlockSpec((tm, MEM_WIDTH), lambda b, t: (base + b * tiles + t, 0)),
            mem_spec, mem_spec,
            pl.BlockSpec((1, MEM_HEAD_DIM), lambda b, t: (0, 0)),
        ],
        out_specs=pl.BlockSpec((tm, MEM_WIDTH), lambda b, t: (base + b * tiles + t, 0)),
        out_shape=jax.ShapeDtypeStruct((N_ROWS, MEM_WIDTH), BF16),
        compiler_params=_cparams(("parallel", "arbitrary")),
        name="cross_attn",
        args=(qc, mem_k, mem_v, q_gain.reshape(1, MEM_HEAD_DIM)),
    )


ROUTER_LANES = LANES
FAR_LANE = 4 * LANES


def _split_bf16(x):
    hi = x.astype(BF16)
    lo = (x - hi.astype(F32)).astype(BF16)
    return hi, lo


def _router_kernel(x_ref, g_ref, w_ref, b_ref, xn_ref, ids_ref, gates_ref):
    xn = _rms_rows(x_ref[...], g_ref[...])
    xn_ref[...] = xn
    xh, xl = _split_bf16(xn)
    wh, wl = _split_bf16(w_ref[...])
    logits = _dot(xh, wh) + (_dot(xh, wl) + _dot(xl, wh)) + b_ref[...]
    lane = lax.broadcasted_iota(jnp.int32, logits.shape, 1)

    def first_max(vals):
        top = jnp.max(vals, axis=-1, keepdims=True)
        idx = jnp.min(jnp.where(vals == top, lane, FAR_LANE), axis=-1, keepdims=True)
        return top, idx

    g_mask = jnp.logical_and(lane >= N_EXPERTS, lane < N_EXPERTS + MOE_GROUPS)
    g_top, g_lane = first_max(jnp.where(g_mask, logits, NEG_INF))
    p_group = 1.0 / jnp.sum(jnp.where(g_mask, jnp.exp(logits - g_top), 0.0), axis=-1, keepdims=True)
    g_sel = g_lane - N_EXPERTS
    e_mask = jnp.logical_and(lane < N_EXPERTS, lane // MOE_PER_GROUP == g_sel)
    e_vals = jnp.where(e_mask, logits, NEG_INF)
    e1_top, e1 = first_max(e_vals)
    e2_top, e2 = first_max(jnp.where(lane == e1, NEG_INF, e_vals))
    t = jnp.exp(e2_top - e1_top)
    gate1 = p_group / (1.0 + t)
    gate2 = p_group * t / (1.0 + t)
    ids_ref[...] = jnp.where(lane == 0, e1, jnp.where(lane == 1, e2, 0))
    gates_ref[...] = jnp.where(lane == 0, gate1, jnp.where(lane == 1, gate2, 0.0))


def _router(x, g, w_router, b_router, tm):
    n, k = x.shape
    c2 = lambda i: (0, 0)
    row = lambda i: (i, 0)
    return pl.pallas_call(
        _router_kernel,
        grid=(n // tm,),
        in_specs=[
            pl.BlockSpec((tm, k), row),
            pl.BlockSpec((1, k), c2),
            pl.BlockSpec((k, ROUTER_LANES), c2),
            pl.BlockSpec((1, ROUTER_LANES), c2),
        ],
        out_specs=[pl.BlockSpec((tm, k), row), pl.BlockSpec((tm, ROUTER_LANES), row),
                   pl.BlockSpec((tm, ROUTER_LANES), row)],
        out_shape=[jax.ShapeDtypeStruct((n, k), F32), jax.ShapeDtypeStruct((n, ROUTER_LANES), jnp.int32),
                   jax.ShapeDtypeStruct((n, ROUTER_LANES), F32)],
        compiler_params=_cparams(("parallel",)),
        name="moe_router",
    )(x, g.reshape(1, k), w_router, b_router)


def _route_plan(expert_idx):
    flat_e = expert_idx.reshape(N_PAIRS)
    onehot = (flat_e[:, None] == jnp.arange(N_EXPERTS, dtype=jnp.int32)[None, :]).astype(jnp.int32)
    csum = jnp.cumsum(onehot, axis=0)
    rank = jnp.take_along_axis(csum, flat_e[:, None], axis=1)[:, 0] - 1
    counts = csum[-1]
    padded = (counts + EXPERT_ROWS - 1) // EXPERT_ROWS * EXPERT_ROWS
    pad_ends = jnp.cumsum(padded)
    pad_starts = pad_ends - padded
    dest = (pad_starts[flat_e] + rank).astype(jnp.int32)
    n_used = (pad_ends[-1] // EXPERT_ROWS).astype(jnp.int32).reshape(1)
    block_start = jnp.arange(N_EXPERT_BLOCKS, dtype=jnp.int32) * EXPERT_ROWS
    block_e = jnp.sum((pad_ends[None, :] <= block_start[:, None]).astype(jnp.int32), axis=1)
    block_e = jnp.minimum(block_e, N_EXPERTS - 1)
    src = jnp.zeros((N_EXPERT_BLOCKS * EXPERT_ROWS,), jnp.int32).at[dest].set(
        jnp.arange(N_PAIRS, dtype=jnp.int32) // MOE_TOP_K)
    return dest, src, block_e, n_used


DMA_UNROLL = 8


def _experts_kernel(src_ref, be_ref, nu_ref, x_hbm, wgu_ref, wd_ref, o_ref, xbuf_ref, sem):
    i = pl.program_id(0)
    n_used = nu_ref[0]
    slot = lax.rem(i, 2)

    def row_copy(blk, buf, c, u):
        r = c * DMA_UNROLL + u
        return pltpu.make_async_copy(x_hbm.at[pl.ds(src_ref[blk * EXPERT_ROWS + r], 1)],
                                     xbuf_ref.at[buf, pl.ds(r, 1)], sem.at[buf])

    def start_block(blk, buf):
        def body(c, carry):
            for u in range(DMA_UNROLL):
                row_copy(blk, buf, c, u).start(priority=u % 2)
            return carry
        lax.fori_loop(0, EXPERT_ROWS // DMA_UNROLL, body, 0)

    def wait_block(blk, buf):
        def body(c, carry):
            for u in range(DMA_UNROLL):
                row_copy(blk, buf, c, u).wait()
            return carry
        lax.fori_loop(0, EXPERT_ROWS // DMA_UNROLL, body, 0)

    @pl.when(jnp.logical_and(i == 0, n_used > 0))
    def _():
        start_block(0, 0)

    @pl.when(i + 1 < n_used)
    def _():
        start_block(i + 1, 1 - slot)

    @pl.when(i < n_used)
    def _():
        wait_block(i, slot)
        gu = _dot(xbuf_ref[slot].astype(BF16), wgu_ref[0, 0].astype(BF16))
        g = gu[:, :EXPERT_FF]
        h = (g * _sigmoid(g)) * gu[:, EXPERT_FF:]
        o_ref[...] = _dot(h.astype(BF16), wd_ref[0, 0].astype(BF16))

    @pl.when(i >= n_used)
    def _():
        o_ref[...] = jnp.zeros(o_ref.shape, o_ref.dtype)


def _experts(xn, src, block_e, n_used, w_gate_up, w_down, layer):
    return pl.pallas_call(
        _experts_kernel,
        grid_spec=pltpu.PrefetchScalarGridSpec(
            num_scalar_prefetch=3,
            grid=(N_EXPERT_BLOCKS,),
            in_specs=[
                pl.BlockSpec(memory_space=pl.ANY),
                pl.BlockSpec((1, 1, D_MODEL, 2 * EXPERT_FF), lambda i, src, be, nu: (layer, be[i], 0, 0)),
                pl.BlockSpec((1, 1, EXPERT_FF, D_MODEL), lambda i, src, be, nu: (layer, be[i], 0, 0)),
            ],
            out_specs=pl.BlockSpec((EXPERT_ROWS, D_MODEL), lambda i, src, be, nu: (i, 0)),
            scratch_shapes=[pltpu.VMEM((2, EXPERT_ROWS, D_MODEL), F32), pltpu.SemaphoreType.DMA((2,))],
        ),
        out_shape=jax.ShapeDtypeStruct((N_EXPERT_BLOCKS * EXPERT_ROWS, D_MODEL), F32),
        compiler_params=_cparams(("arbitrary",)),
        name="moe_experts",
    )(src, block_e, n_used, xn, w_gate_up, w_down)


def _combine_kernel(dest_ref, x_ref, gates_ref, y_hbm, o_ref, buf_ref, sem, *, rows):
    i = pl.program_id(0)
    slot = lax.rem(i, 2)
    per_iter = DMA_UNROLL

    def row_copy(tile, buf, c, u, k):
        return pltpu.make_async_copy(
            y_hbm.at[pl.ds(dest_ref[(tile * rows + c * per_iter + u) * MOE_TOP_K + k], 1)],
            buf_ref.at[buf, k, c, pl.ds(u, 1)], sem.at[buf])

    def start_tile(tile, buf):
        def body(c, carry):
            for u in range(per_iter):
                for k in range(MOE_TOP_K):
                    row_copy(tile, buf, c, u, k).start(priority=k)
            return carry
        lax.fori_loop(0, rows // per_iter, body, 0)

    def wait_tile(tile, buf):
        def body(c, carry):
            for u in range(per_iter):
                for k in range(MOE_TOP_K):
                    row_copy(tile, buf, c, u, k).wait()
            return carry
        lax.fori_loop(0, rows // per_iter, body, 0)

    @pl.when(i == 0)
    def _():
        start_tile(0, 0)

    @pl.when(i + 1 < pl.num_programs(0))
    def _():
        start_tile(i + 1, 1 - slot)

    wait_tile(i, slot)
    gates = gates_ref[...]
    y0 = buf_ref[slot, 0].reshape(x_ref.shape)
    y1 = buf_ref[slot, 1].reshape(x_ref.shape)
    o_ref[...] = x_ref[...] + (gates[:, 0:1] * y0 + gates[:, 1:2] * y1)


def _combine(x, gates, y_sorted, dest, rows):
    n, width = x.shape
    return pl.pallas_call(
        functools.partial(_combine_kernel, rows=rows),
        grid_spec=pltpu.PrefetchScalarGridSpec(
            num_scalar_prefetch=1,
            grid=(n // rows,),
            in_specs=[
                pl.BlockSpec((rows, width), lambda i, d: (i, 0)),
                pl.BlockSpec((rows, ROUTER_LANES), lambda i, d: (i, 0)),
                pl.BlockSpec(memory_space=pl.ANY),
            ],
            out_specs=pl.BlockSpec((rows, width), lambda i, d: (i, 0)),
            scratch_shapes=[pltpu.VMEM((2, MOE_TOP_K, rows // DMA_UNROLL, DMA_UNROLL, width), F32),
                            pltpu.SemaphoreType.DMA((2,))],
        ),
        out_shape=jax.ShapeDtypeStruct((n, width), F32),
        compiler_params=_cparams(("arbitrary",)),
        name="moe_combine",
    )(dest, x, gates, y_sorted)


def _hier_moe(x, lp, w_gate_up, w_down, layer):
    w_router = jnp.concatenate(
        [lp['w_expert_router'], lp['w_group_router'],
         jnp.zeros((D_MODEL, ROUTER_LANES - N_EXPERTS - MOE_GROUPS), F32)], axis=1)
    b_router = jnp.concatenate(
        [lp['b_expert'], lp['b_group'], jnp.zeros((ROUTER_LANES - N_EXPERTS - MOE_GROUPS,), F32)]).reshape(1, -1)
    xn, ids, gates = _router(x, lp['norm_ffn'], w_router, b_router, 512)
    dest, src, block_e, n_used = _route_plan(ids[:, :MOE_TOP_K])
    y_sorted = _experts(xn, src, block_e, n_used, w_gate_up, w_down, layer)
    return _combine(x, gates, y_sorted, dest, 256)


def _cols(w, *ranges):
    return jnp.concatenate([w[:, a:b] for a, b in ranges], axis=1)


def _prompt_or_sample_block(i, tm):
    return jnp.where(i < N_PROMPT // tm, lax.rem(i, SEQ // tm), SEQ // tm)


def _layer(x, mem_prompt, lp, layer, caches, tables):
    (ckv_all, kpe_past_all, h0_re, h0_im, ret_s0_all, mem_k_all, mem_v_all, w_gate_up, w_down) = caches
    cos_mla, sin_mla, cos_past, sin_past, cos_ret, sin_ret = tables
    tm = 512
    w_in = lp['w_in']
    w_main = _cols(w_in, IN_GMIX, IN_S5, IN_RV, IN_RG, IN_QLAT, IN_KVLAT, IN_RQ, IN_RK).astype(BF16)
    w_kpe = jnp.pad(_cols(w_in, IN_KPE), ((0, 0), (0, LANES - MLA_ROPE))).astype(BF16)
    proj, kpe = _rms_proj(x, lp['norm_mix'], w_main, BF16, 1024, 1024, w2=w_kpe, out2_dtype=F32)

    s5_tabs = _s5_tables(lp)
    w_glu = lp['s5_w_glu'].astype(BF16)
    zeros_s5 = jnp.zeros((BATCH, S5_GROUPS, S5_STATE), F32)
    a, p_s5_re, p_s5_im = _s5_branch(proj, s5_tabs, w_glu, zeros_s5, zeros_s5, BATCH, SEQ, 256, 0)
    a, s_s5_re, s_s5_im = _s5_branch(proj, s5_tabs, w_glu, h0_re, h0_im, DEC_BATCH, DEC_SEQ, 256, N_PROMPT, into=a)

    w_uq = lp['mla_w_uq'].reshape(MLA_Q_LORA, MLA_HEADS, MLA_QK)
    w_uq = jnp.pad(w_uq, ((0, 0), (0, 0), (0, MLA_HEAD_PAD - MLA_QK))).reshape(MLA_Q_LORA, MLA_QK_PAD).astype(BF16)
    w_ukv = lp['mla_w_ukv'].astype(BF16)
    q_gain = _pad_head_vec(lp['mla_q_gain'])
    k_gain = _pad_head_vec(lp['mla_k_gain'])
    row_tab = lambda i: _prompt_or_sample_block(i, tm)
    q = _mla_q(proj, lp['mla_q_norm'], w_uq, q_gain, cos_mla, sin_mla, row_tab, tm)
    c_kv, k_new, v_new = _mla_kv(proj, lp['mla_kv_norm'], w_ukv, k_gain, kpe, cos_mla, sin_mla, row_tab, tm)
    w_ukv_heads = w_ukv.reshape(MLA_KV_LORA, MLA_HEADS, MLA_NOPE + MLA_V)
    w_uk = w_ukv_heads[:, :, :MLA_NOPE].reshape(MLA_KV_LORA, MLA_HEADS * MLA_NOPE)
    w_uv = w_ukv_heads[:, :, MLA_NOPE:].reshape(MLA_KV_LORA, MLA_HEADS * MLA_V)
    b = _attn_prompt(q, k_new, v_new, BATCH, SEQ, 512)
    b = _attn_sample(q, k_new, v_new, ckv_all, kpe_past_all, layer * DEC_BATCH, cos_past, sin_past, w_uk, w_uv,
                     k_gain, N_PROMPT, into=b)

    zeros_ret = jnp.zeros((BATCH, RET_HEADS, RET_DK, RET_DV), F32)
    c, p_ret = _retention_branch(proj, cos_ret, sin_ret, lambda t: t, zeros_ret, 0, BATCH, SEQ, 256, 0)
    c, s_ret = _retention_branch(proj, cos_ret, sin_ret, lambda t: SEQ // DEC_SEQ, ret_s0_all, layer * DEC_BATCH,
                                 DEC_BATCH, DEC_SEQ, DEC_SEQ, N_PROMPT, into=c)

    merged = _merge(a, b, c, lp['w_branch'].astype(BF16), proj, 1024, 512)
    x = _res_matmul(merged, lp['w_out'].astype(BF16), x, 1024, 1024)

    mem_k_p, mem_v_p = _memkv(mem_prompt, lp['norm_mem'], lp['w_ck'].astype(BF16), lp['w_cv'].astype(BF16),
                              lp['cross_k_gain'], 256)
    qc = _rms_proj(x, lp['norm_cross'], lp['w_cq'].astype(BF16), BF16, 512, MEM_WIDTH)[0]
    o = _cross_attn(qc, mem_k_p.reshape(BATCH, N_MEM, MEM_WIDTH), mem_v_p.reshape(BATCH, N_MEM, MEM_WIDTH), 0,
                    lp['cross_q_gain'], BATCH, SEQ, 512, 0)
    o = _cross_attn(qc, mem_k_all, mem_v_all, layer * DEC_BATCH, lp['cross_q_gain'], DEC_BATCH, DEC_SEQ, DEC_SEQ,
                    N_PROMPT, into=o)
    x = _res_matmul(o, lp['w_co'].astype(BF16), x, 1024, 1024)

    x = _hier_moe(x, lp, w_gate_up, w_down, layer)

    kpe_rows = kpe[:, :MLA_ROPE]
    state_p = (c_kv[:N_PROMPT].reshape(BATCH, SEQ, MLA_KV_LORA), kpe_rows[:N_PROMPT].reshape(BATCH, SEQ, MLA_ROPE),
               p_s5_re, p_s5_im, p_ret,
               mem_k_p.reshape(BATCH, N_MEM, MEM_HEADS, MEM_HEAD_DIM),
               mem_v_p.reshape(BATCH, N_MEM, MEM_HEADS, MEM_HEAD_DIM))
    state_s = (c_kv[N_PROMPT:].reshape(DEC_BATCH, DEC_SEQ, MLA_KV_LORA),
               kpe_rows[N_PROMPT:].reshape(DEC_BATCH, DEC_SEQ, MLA_ROPE), s_s5_re, s_s5_im, s_ret)
    return x, state_p, state_s


_LAYER_PARAMS = (
    'norm_mix', 'w_in', 's5_a_re', 's5_a_im', 's5_log_dt', 's5_b_re', 's5_b_im', 's5_c_re', 's5_c_im', 's5_d',
    's5_w_glu', 'mla_q_norm', 'mla_w_uq', 'mla_kv_norm', 'mla_w_ukv', 'mla_q_gain', 'mla_k_gain', 'w_branch',
    'w_out', 'norm_cross', 'norm_mem', 'w_cq', 'w_ck', 'w_cv', 'cross_q_gain', 'cross_k_gain', 'w_co', 'norm_ffn',
    'w_group_router', 'b_group', 'w_expert_router', 'b_expert')


def kernel(x_prompt, x_sample, mem_prompt, cache_mla_ckv, cache_mla_kpe, state_s5_re, state_s5_im, state_ret,
           cache_mem_k, cache_mem_v, norm_mix, w_in, s5_a_re, s5_a_im, s5_log_dt, s5_b_re, s5_b_im, s5_c_re,
           s5_c_im, s5_d, s5_w_glu, mla_q_norm, mla_w_uq, mla_kv_norm, mla_w_ukv, mla_q_gain, mla_k_gain,
           w_branch, w_out, norm_cross, norm_mem, w_cq, w_ck, w_cv, cross_q_gain, cross_k_gain, w_co, norm_ffn,
           w_group_router, b_group, w_expert_router, b_expert, w_gate_up, w_down):
    params = dict(zip(_LAYER_PARAMS, (
        norm_mix, w_in, s5_a_re, s5_a_im, s5_log_dt, s5_b_re, s5_b_im, s5_c_re, s5_c_im, s5_d, s5_w_glu,
        mla_q_norm, mla_w_uq, mla_kv_norm, mla_w_ukv, mla_q_gain, mla_k_gain, w_branch, w_out, norm_cross,
        norm_mem, w_cq, w_ck, w_cv, cross_q_gain, cross_k_gain, w_co, norm_ffn, w_group_router, b_group,
        w_expert_router, b_expert)))
    assert x_prompt.shape == (BATCH, SEQ, D_MODEL) and x_sample.shape == (DEC_BATCH, DEC_SEQ, D_MODEL)
    assert cache_mla_ckv.shape == (DEPTH, DEC_BATCH, PAST_LEN, MLA_KV_LORA)

    pos_p = jnp.arange(SEQ, dtype=jnp.int32)
    pos_s = PAST_LEN + jnp.arange(DEC_SEQ, dtype=jnp.int32)
    tile_rows = 512
    tables = (_rope_tables(jnp.concatenate([pos_p, jnp.tile(pos_s, tile_rows // DEC_SEQ)]), MLA_ROPE)
              + _rope_tables(jnp.arange(PAST_LEN, dtype=jnp.int32), MLA_ROPE)
              + _rope_tables(jnp.concatenate([pos_p, pos_s]), RET_DK))

    x = jnp.concatenate([x_prompt.reshape(N_PROMPT, D_MODEL), x_sample.reshape(N_SAMPLE, D_MODEL)], axis=0)
    mem2d = mem_prompt.reshape(BATCH * N_MEM, D_MODEL)
    n_past = DEPTH * DEC_BATCH * PAST_LEN
    ckv_all = cache_mla_ckv.reshape(n_past, MLA_KV_LORA)
    kpe_past_all = jnp.pad(cache_mla_kpe.reshape(n_past, MLA_ROPE), ((0, 0), (0, LANES - MLA_ROPE)))
    ret_s0_all = state_ret.reshape(DEPTH * DEC_BATCH, RET_HEADS, RET_DK, RET_DV)
    mem_k_all = cache_mem_k.reshape(DEPTH * DEC_BATCH, N_MEM, MEM_HEADS, MEM_HEAD_DIM)
    mem_v_all = cache_mem_v.reshape(DEPTH * DEC_BATCH, N_MEM, MEM_HEADS, MEM_HEAD_DIM)
    outs_p, outs_s = [], []
    for l in range(DEPTH):
        lp = {name: value[l] for name, value in params.items()}
        caches = (ckv_all, kpe_past_all, state_s5_re[l], state_s5_im[l], ret_s0_all, mem_k_all, mem_v_all,
                  w_gate_up, w_down)
        x, st_p, st_s = _layer(x, mem2d, lp, l, caches, tables)
        outs_p.append(st_p)
        outs_s.append(st_s)
    stack = lambda outs, i: jnp.stack([o[i] for o in outs])
    return ((x[:N_PROMPT].reshape(BATCH, SEQ, D_MODEL), x[N_PROMPT:].reshape(DEC_BATCH, DEC_SEQ, D_MODEL))
            + tuple(stack(outs_p, i) for i in range(7)) + tuple(stack(outs_s, i) for i in range(5)))
```

```python
import functools
import math

import jax
import jax.numpy as jnp
from jax import lax
from jax.experimental import pallas as pl
from jax.experimental.pallas import tpu as pltpu

F32 = jnp.float32
BF16 = jnp.bfloat16

D_MODEL = 2048
BATCH = 4
SEQ = 2048
DEPTH = 2
DEC_BATCH = 32
DEC_SEQ = 32
PAST_LEN = 1024
CHUNK = 64
RMS_EPS = 1e-6
ROPE_THETA = 10000.0
NEG_INF = -1e30

S5_WIDTH = 1024
S5_GROUP_CH = 16
S5_GROUPS = 64
S5_STATE = 64
S5_COLS = S5_GROUPS * S5_STATE

MLA_HEADS = 8
MLA_NOPE = 128
MLA_ROPE = 64
MLA_QK = MLA_NOPE + MLA_ROPE
MLA_V = 128
MLA_Q_LORA = 512
MLA_KV_LORA = 512
MLA_HEAD_PAD = 256
MLA_QK_PAD = MLA_HEADS * MLA_HEAD_PAD

RET_HEADS = 4
RET_DK = 128
RET_DV = 256
BRANCH_WIDTH = 1024
N_BRANCH = 3

N_MEM = 256
MEM_HEADS = 4
MEM_HEAD_DIM = 128
MEM_WIDTH = MEM_HEADS * MEM_HEAD_DIM

MOE_GROUPS = 4
MOE_PER_GROUP = 8
N_EXPERTS = 32
MOE_TOP_K = 2
EXPERT_FF = 512

N_PROMPT = BATCH * SEQ
N_SAMPLE = DEC_BATCH * DEC_SEQ
N_ROWS = N_PROMPT + N_SAMPLE

COL_GMIX = 0
COL_S5 = 6144
COL_RV = 7168
COL_RG = 8192
COL_QLAT = 9216
COL_KVLAT = 9728
COL_RQ = 10240
COL_RK = 10752
PROJ_COLS = 11264
IN_S5 = (0, 1024)
IN_QLAT = (1024, 1536)
IN_KVLAT = (1536, 2048)
IN_KPE = (2048, 2112)
IN_RQ = (2112, 2624)
IN_RK = (2624, 3136)
IN_RV = (3136, 4160)
IN_RG = (4160, 5184)
IN_GMIX = (5184, 11328)

LANES = 128
EXPERT_ROWS = 256
N_PAIRS = N_ROWS * MOE_TOP_K
N_EXPERT_BLOCKS = (N_PAIRS + N_EXPERTS * (EXPERT_ROWS - 1)) // EXPERT_ROWS + 1
VMEM_LIMIT = 56 * 1024 * 1024


def _cparams(sem):
    return pltpu.CompilerParams(dimension_semantics=sem, vmem_limit_bytes=VMEM_LIMIT)


def _rms_rows(x, g):
    r = lax.rsqrt(jnp.mean(x * x, axis=-1, keepdims=True) + RMS_EPS)
    return (x * r) * g


def _dot(a, b):
    return jnp.dot(a, b, preferred_element_type=F32)


def _rows_call(kernel_fn, into, *, in_specs, args, **kw):
    if into is None:
        out0 = kw['out_shape'][0] if isinstance(kw['out_shape'], (list, tuple)) else kw['out_shape']
        into = jnp.zeros(out0.shape, out0.dtype)

    def aliased_kernel(into_ref, *refs):
        del into_ref
        kernel_fn(*refs)

    return pl.pallas_call(aliased_kernel, in_specs=[pl.BlockSpec(memory_space=pl.ANY)] + in_specs,
                          input_output_aliases={0: 0}, **kw)(into, *args)


def _dot_nt(a, b):
    return lax.dot_general(a, b, (((1,), (1,)), ((), ())), preferred_element_type=F32)


def _rms_proj_kernel(x_ref, g_ref, w_ref, *refs, side):
    if side:
        w2_ref, o_ref, o2_ref, xn_ref = refs
    else:
        o_ref, xn_ref = refs

    @pl.when(pl.program_id(1) == 0)
    def _():
        xn_ref[...] = _rms_rows(x_ref[...], g_ref[...]).astype(BF16)
        if side:
            o2_ref[...] = _dot(xn_ref[...], w2_ref[...]).astype(o2_ref.dtype)

    o_ref[...] = _dot(xn_ref[...], w_ref[...]).astype(o_ref.dtype)


def _rms_proj(x, g, w, out_dtype, tm, tn, w2=None, out2_dtype=None):
    n, k = x.shape
    cols = w.shape[1]
    side = w2 is not None
    in_specs = [
        pl.BlockSpec((tm, k), lambda i, j: (i, 0)),
        pl.BlockSpec((1, k), lambda i, j: (0, 0)),
        pl.BlockSpec((k, tn), lambda i, j: (0, j)),
    ]
    out_specs = [pl.BlockSpec((tm, tn), lambda i, j: (i, j))]
    out_shape = [jax.ShapeDtypeStruct((n, cols), out_dtype)]
    args = [x, g.reshape(1, k), w]
    if side:
        cols2 = w2.shape[1]
        in_specs.append(pl.BlockSpec((k, cols2), lambda i, j: (0, 0)))
        out_specs.append(pl.BlockSpec((tm, cols2), lambda i, j: (i, 0)))
        out_shape.append(jax.ShapeDtypeStruct((n, cols2), out2_dtype))
        args.append(w2)
    return pl.pallas_call(
        functools.partial(_rms_proj_kernel, side=side),
        grid=(n // tm, cols // tn),
        in_specs=in_specs,
        out_specs=out_specs,
        out_shape=out_shape,
        scratch_shapes=[pltpu.VMEM((tm, k), BF16)],
        compiler_params=_cparams(("parallel", "arbitrary")),
        name="rms_proj_side" if side else "rms_proj",
    )(*args)


def _res_matmul_kernel(a_ref, w_ref, r_ref, o_ref):
    o_ref[...] = r_ref[...] + _dot(a_ref[...], w_ref[...])


def _res_matmul(a, w, res, tm, tn):
    n, k = a.shape
    cols = w.shape[1]
    return pl.pallas_call(
        _res_matmul_kernel,
        grid=(n // tm, cols // tn),
        in_specs=[
            pl.BlockSpec((tm, k), lambda i, j: (i, 0)),
            pl.BlockSpec((k, tn), lambda i, j: (0, j)),
            pl.BlockSpec((tm, tn), lambda i, j: (i, j)),
        ],
        out_specs=pl.BlockSpec((tm, tn), lambda i, j: (i, j)),
        out_shape=jax.ShapeDtypeStruct((n, cols), F32),
        compiler_params=_cparams(("parallel", "arbitrary")),
        name="res_matmul",
    )(a, w, res)


S5_K_SLAB = 256
S5_N_SLAB = 1024
S5_SLABS = S5_WIDTH // S5_K_SLAB
S5_SCAN_COLS = 1024
S5_SCAN_ROWS = 8


def _gelu_tanh(x):
    return 0.5 * x * (1.0 + jnp.tanh(math.sqrt(2.0 / math.pi) * (x + 0.044715 * (x * x * x))))


def _sigmoid(x):
    return 1.0 / (1.0 + jnp.exp(-x))


def _s5_kernel(u_ref, bre_ref, bim_ref, cre_ref, cim_ref, pw_ref, d_ref, wglu_ref,
               h0r_ref, h0i_ref, o_ref, hr_out_ref, hi_out_ref, sre_ref, sim_ref, cr_ref, ci_ref, *, seqs, seq_rows):
    @pl.when(pl.program_id(1) == 0)
    def _():
        cr_ref[...] = h0r_ref[...]
        ci_ref[...] = h0i_ref[...]

    u = u_ref[...]
    for n in range(S5_SLABS):
        un = u[:, S5_K_SLAB * n:S5_K_SLAB * (n + 1)]
        sre_ref[:, S5_N_SLAB * n:S5_N_SLAB * (n + 1)] = _dot(un, bre_ref[n])
        sim_ref[:, S5_N_SLAB * n:S5_N_SLAB * (n + 1)] = _dot(un, bim_ref[n])

    for c, q in [(c, q) for c in range(S5_COLS // S5_SCAN_COLS) for q in range(seqs)]:
        sl = slice(c * S5_SCAN_COLS, (c + 1) * S5_SCAN_COLS)

        def body(blk, carry, sl=sl, q=q):
            cr, ci = carry
            rows = pl.ds(pl.multiple_of(q * seq_rows + blk * S5_SCAN_ROWS, S5_SCAN_ROWS), S5_SCAN_ROWS)
            xr = sre_ref[rows, sl]
            xi = sim_ref[rows, sl]
            for j in range(3):
                kr = pw_ref[j, 0, :, sl]
                ki = pw_ref[j, 1, :, sl]
                sr = pltpu.roll(xr, 1 << j, 0)
                si = pltpu.roll(xi, 1 << j, 0)
                xr, xi = xr + (kr * sr - ki * si), xi + (kr * si + ki * sr)
            pr = pw_ref[3, 0, :, sl]
            pi = pw_ref[3, 1, :, sl]
            hr = xr + (pr * cr - pi * ci)
            hi = xi + (pr * ci + pi * cr)
            sre_ref[rows, sl] = hr
            sim_ref[rows, sl] = hi
            last = S5_SCAN_ROWS - 1
            return (jnp.broadcast_to(hr[last:last + 1], hr.shape), jnp.broadcast_to(hi[last:last + 1], hi.shape))

        init = (jnp.broadcast_to(cr_ref[q, :, sl], (S5_SCAN_ROWS, S5_SCAN_COLS)),
                jnp.broadcast_to(ci_ref[q, :, sl], (S5_SCAN_ROWS, S5_SCAN_COLS)))
        hr, hi = lax.fori_loop(0, seq_rows // S5_SCAN_ROWS, body, init)
        cr_ref[q, :, sl] = hr[0:1]
        ci_ref[q, :, sl] = hi[0:1]

    ys = []
    for n in range(S5_SLABS):
        hr_n = sre_ref[:, S5_N_SLAB * n:S5_N_SLAB * (n + 1)].astype(BF16)
        hi_n = sim_ref[:, S5_N_SLAB * n:S5_N_SLAB * (n + 1)].astype(BF16)
        ys.append(_dot(hr_n, cre_ref[n]) + _dot(hi_n, cim_ref[n]))
    y = jnp.concatenate(ys, axis=1) + d_ref[...] * u.astype(F32)
    z = _gelu_tanh(y)
    gate = _sigmoid(_dot(z.astype(BF16), wglu_ref[...]))
    o_ref[...] = (z * gate).astype(o_ref.dtype)
    hr_out_ref[...] = cr_ref[...]
    hi_out_ref[...] = ci_ref[...]


def _s5_tables(lp):
    a_re, a_im = lp['s5_a_re'], lp['s5_a_im']
    dt = jnp.exp(lp['s5_log_dt'])[:, None]
    mag = jnp.exp(a_re * dt)
    lb_re = mag * jnp.cos(a_im * dt)
    lb_im = mag * jnp.sin(a_im * dt)
    den = a_re * a_re + a_im * a_im
    n_re = lb_re - 1.0
    co_re = (n_re * a_re + lb_im * a_im) / den
    co_im = (lb_im * a_re - n_re * a_im) / den
    bb_re = co_re[..., None] * lp['s5_b_re'] - co_im[..., None] * lp['s5_b_im']
    bb_im = co_re[..., None] * lp['s5_b_im'] + co_im[..., None] * lp['s5_b_re']
    per_slab = S5_GROUPS // S5_SLABS

    def block_diag_tiles(blocks, rows, cols):
        spread = jnp.tile(jnp.eye(cols, dtype=F32), (1, per_slab))
        tiled = jnp.einsum('nrc,cd->nrd', blocks, spread, precision=lax.Precision.HIGHEST)
        row_group = jnp.arange(per_slab * rows)[:, None] // rows
        col_group = jnp.arange(per_slab * cols)[None, :] // cols
        return jnp.where(row_group == col_group, tiled, 0.0).astype(BF16)

    def b_tiles(bb):
        bb = bb.reshape(S5_SLABS, per_slab, S5_STATE, S5_GROUP_CH).transpose(0, 1, 3, 2)
        return block_diag_tiles(bb.reshape(S5_SLABS, S5_K_SLAB, S5_STATE), S5_GROUP_CH, S5_STATE)

    def c_tiles(cc):
        cc = cc.reshape(S5_SLABS, per_slab, S5_GROUP_CH, S5_STATE).transpose(0, 1, 3, 2)
        return block_diag_tiles(cc.reshape(S5_SLABS, S5_N_SLAB, S5_GROUP_CH), S5_STATE, S5_GROUP_CH)

    def lam_pow(k):
        m = jnp.exp(k * (a_re * dt).reshape(1, S5_COLS))
        ang = k * (a_im * dt).reshape(1, S5_COLS)
        return jnp.stack([m * jnp.cos(ang), m * jnp.sin(ang)])

    t = jnp.arange(S5_SCAN_ROWS, dtype=F32)[:, None]
    steps = [jnp.where(t >= float(1 << j), lam_pow(jnp.full_like(t, float(1 << j))), 0.0) for j in range(3)]
    pw = jnp.stack(steps + [lam_pow(t + 1.0)])

    return dict(
        bre=b_tiles(bb_re), bim=b_tiles(bb_im),
        cre=c_tiles(lp['s5_c_re']), cim=c_tiles(-lp['s5_c_im']),
        pw=pw, d=lp['s5_d'].reshape(1, S5_WIDTH),
    )


def _s5_branch(proj, tabs, wglu, h0_re, h0_im, n_seq, seq_len, tt, row0, into=None):
    seqs = max(tt // seq_len, 1)
    seq_rows = min(tt, seq_len)
    nt = seq_len // seq_rows
    base = row0 // tt
    col = COL_S5 // S5_WIDTH
    const3 = lambda s, t: (0, 0, 0)
    const2 = lambda s, t: (0, 0)
    state = pl.BlockSpec((seqs, 1, S5_COLS), lambda s, t: (s, 0, 0))
    out, hr, hi = _rows_call(
        functools.partial(_s5_kernel, seqs=seqs, seq_rows=seq_rows), into,
        grid=(n_seq // seqs, nt),
        in_specs=[
            pl.BlockSpec((tt, S5_WIDTH), lambda s, t: (base + s * nt + t, col)),
            pl.BlockSpec((S5_SLABS, S5_K_SLAB, S5_N_SLAB), const3),
            pl.BlockSpec((S5_SLABS, S5_K_SLAB, S5_N_SLAB), const3),
            pl.BlockSpec((S5_SLABS, S5_N_SLAB, S5_K_SLAB), const3),
            pl.BlockSpec((S5_SLABS, S5_N_SLAB, S5_K_SLAB), const3),
            pl.BlockSpec((4, 2, S5_SCAN_ROWS, S5_COLS), lambda s, t: (0, 0, 0, 0)),
            pl.BlockSpec((1, S5_WIDTH), const2),
            pl.BlockSpec((S5_WIDTH, S5_WIDTH), const2),
            state, state,
        ],
        out_specs=[pl.BlockSpec((tt, S5_WIDTH), lambda s, t: (base + s * nt + t, 0)), state, state],
        out_shape=[
            jax.ShapeDtypeStruct((N_ROWS, S5_WIDTH), BF16),
            jax.ShapeDtypeStruct((n_seq, 1, S5_COLS), F32),
            jax.ShapeDtypeStruct((n_seq, 1, S5_COLS), F32),
        ],
        scratch_shapes=[
            pltpu.VMEM((tt, S5_COLS), F32), pltpu.VMEM((tt, S5_COLS), F32),
            pltpu.VMEM((seqs, 1, S5_COLS), F32), pltpu.VMEM((seqs, 1, S5_COLS), F32),
        ],
        compiler_params=_cparams(("parallel", "arbitrary")),
        name="s5_branch",
        args=(proj, tabs['bre'], tabs['bim'], tabs['cre'], tabs['cim'], tabs['pw'], tabs['d'], wglu,
              h0_re.reshape(n_seq, 1, S5_COLS), h0_im.reshape(n_seq, 1, S5_COLS)),
    )
    return (out, hr.reshape(n_seq, S5_GROUPS, S5_STATE), hi.reshape(n_seq, S5_GROUPS, S5_STATE))


def _rope_tables(pos, d):
    inv_freq = ROPE_THETA ** (-jnp.arange(0, d, 2, dtype=F32) / d)
    ang = pos.astype(F32)[:, None] * inv_freq[None, :]
    cos, sin = jnp.cos(ang), jnp.sin(ang)
    pad = jnp.zeros((pos.shape[0], LANES - d), F32)
    return (jnp.concatenate([cos, cos, pad], axis=1), jnp.concatenate([-sin, sin, pad], axis=1))


def _pad_head_vec(g):
    return jnp.concatenate([g, jnp.zeros((MLA_HEAD_PAD - MLA_QK,), F32)]).reshape(1, MLA_HEAD_PAD)


def _qk_head(nope, pe, gain, cos2, sin2, scale):
    ss = jnp.sum(nope * nope, axis=-1, keepdims=True) + jnp.sum(pe * pe, axis=-1, keepdims=True)
    r = lax.rsqrt(ss * (1.0 / MLA_QK) + RMS_EPS)
    nope = (nope * r) * gain[:, :MLA_NOPE]
    pe = (pe * r) * gain[:, MLA_NOPE:]
    lane = lax.broadcasted_iota(jnp.int32, pe.shape, 1)
    half = MLA_ROPE // 2
    swap = jnp.where(lane < half, pltpu.roll(pe, LANES - half, 1), pltpu.roll(pe, half, 1))
    pe = pe * cos2 + swap * sin2
    return nope * scale, pe * scale


def _mla_q_kernel(lat_ref, g_ref, w_ref, gain_ref, cos_ref, sin_ref, o_ref):
    xn = _rms_rows(lat_ref[...].astype(F32), g_ref[...]).astype(BF16)
    q = _dot(xn, w_ref[...])
    gain = gain_ref[...]
    cos2 = cos_ref[...]
    sin2 = sin_ref[...]
    for h in range(MLA_HEADS):
        c0 = h * MLA_HEAD_PAD
        nope, pe = _qk_head(q[:, c0:c0 + MLA_NOPE], q[:, c0 + MLA_NOPE:c0 + MLA_HEAD_PAD],
                            gain, cos2, sin2, MLA_QK ** -0.5)
        o_ref[:, c0:c0 + MLA_NOPE] = nope.astype(BF16)
        o_ref[:, c0 + MLA_NOPE:c0 + MLA_HEAD_PAD] = pe.astype(BF16)


def _mla_q(proj, g, w_uq_pad, gain_pad, cos2, sin2, tab_block, tm):
    n = proj.shape[0]
    col = COL_QLAT // MLA_Q_LORA
    c2 = lambda i: (0, 0)
    tab = lambda i: (tab_block(i), 0)
    return pl.pallas_call(
        _mla_q_kernel,
        grid=(n // tm,),
        in_specs=[
            pl.BlockSpec((tm, MLA_Q_LORA), lambda i: (i, col)),
            pl.BlockSpec((1, MLA_Q_LORA), c2),
            pl.BlockSpec((MLA_Q_LORA, MLA_QK_PAD), c2),
            pl.BlockSpec((1, MLA_HEAD_PAD), c2),
            pl.BlockSpec((tm, LANES), tab),
            pl.BlockSpec((tm, LANES), tab),
        ],
        out_specs=pl.BlockSpec((tm, MLA_QK_PAD), lambda i: (i, 0)),
        out_shape=jax.ShapeDtypeStruct((n, MLA_QK_PAD), BF16),
        compiler_params=_cparams(("parallel",)),
        name="mla_q",
    )(proj, g.reshape(1, -1), w_uq_pad, gain_pad, cos2, sin2)


def _mla_kv_kernel(lat_ref, g_ref, w_ref, gain_ref, kpe_ref, cos_ref, sin_ref, ckv_ref, k_ref, v_ref):
    lat = _rms_rows(lat_ref[...].astype(F32), g_ref[...])
    ckv_ref[...] = lat
    kv = _dot(lat.astype(BF16), w_ref[...])
    gain = gain_ref[...]
    kpe = kpe_ref[...]
    cos2 = cos_ref[...]
    sin2 = sin_ref[...]
    for h in range(MLA_HEADS):
        c0 = h * (MLA_NOPE + MLA_V)
        nope, pe = _qk_head(kv[:, c0:c0 + MLA_NOPE], kpe, gain, cos2, sin2, 1.0)
        k0 = h * MLA_HEAD_PAD
        k_ref[:, k0:k0 + MLA_NOPE] = nope.astype(BF16)
        k_ref[:, k0 + MLA_NOPE:k0 + MLA_HEAD_PAD] = pe.astype(BF16)
        v_ref[:, h * MLA_V:(h + 1) * MLA_V] = kv[:, c0 + MLA_NOPE:c0 + MLA_NOPE + MLA_V].astype(BF16)


def _mla_kv(proj, g, w_ukv, gain_pad, kpe_pad, cos2, sin2, tab_block, tm):
    n = proj.shape[0]
    c2 = lambda i: (0, 0)
    row = lambda i: (i, 0)
    tab = lambda i: (tab_block(i), 0)
    return pl.pallas_call(
        _mla_kv_kernel,
        grid=(n // tm,),
        in_specs=[
            pl.BlockSpec((tm, MLA_KV_LORA), lambda i: (i, COL_KVLAT // MLA_KV_LORA)),
            pl.BlockSpec((1, MLA_KV_LORA), c2),
            pl.BlockSpec((MLA_KV_LORA, MLA_HEADS * (MLA_NOPE + MLA_V)), c2),
            pl.BlockSpec((1, MLA_HEAD_PAD), c2),
            pl.BlockSpec((tm, LANES), row),
            pl.BlockSpec((tm, LANES), tab),
            pl.BlockSpec((tm, LANES), tab),
        ],
        out_specs=[pl.BlockSpec((tm, MLA_KV_LORA), row), pl.BlockSpec((tm, MLA_QK_PAD), row),
                   pl.BlockSpec((tm, MLA_HEADS * MLA_V), row)],
        out_shape=[jax.ShapeDtypeStruct((n, MLA_KV_LORA), F32), jax.ShapeDtypeStruct((n, MLA_QK_PAD), BF16),
                   jax.ShapeDtypeStruct((n, MLA_HEADS * MLA_V), BF16)],
        compiler_params=_cparams(("parallel",)),
        name="mla_kv",
    )(proj, g.reshape(1, -1), w_ukv, gain_pad, kpe_pad, cos2, sin2)


ATTN_HEADS_PER_STEP = 2


def _attn_prompt_kernel(q_ref, k_ref, v_ref, o_ref, *, tq):
    qi = pl.program_id(2)

    def key_tile(ki, state, diagonal):
        rows = pl.ds(pl.multiple_of(ki * tq, tq), tq)
        new_state = []
        for h in range(ATTN_HEADS_PER_STEP):
            qk = slice(h * MLA_HEAD_PAD, (h + 1) * MLA_HEAD_PAD)
            vs = slice(h * MLA_V, (h + 1) * MLA_V)
            s = _dot_nt(q_ref[:, qk], k_ref[rows, qk])
            if diagonal:
                row_chunk = lax.broadcasted_iota(jnp.int32, s.shape, 0) // CHUNK
                col_chunk = lax.broadcasted_iota(jnp.int32, s.shape, 1) // CHUNK
                s = jnp.where(col_chunk <= row_chunk, s, NEG_INF)
            m_old, l_old, acc_old = state[h]
            m_new = jnp.maximum(m_old, jnp.max(s, axis=-1, keepdims=True))
            alpha = jnp.exp(m_old - m_new)
            p = jnp.exp(s - m_new)
            l_new = alpha * l_old + jnp.sum(p, axis=-1, keepdims=True)
            acc_new = alpha * acc_old + _dot(p.astype(BF16), v_ref[rows, vs])
            new_state.append((m_new, l_new, acc_new))
        return tuple(new_state)

    init = tuple((jnp.full((tq, 1), NEG_INF, F32), jnp.zeros((tq, 1), F32), jnp.zeros((tq, MLA_V), F32))
                 for _ in range(ATTN_HEADS_PER_STEP))
    state = lax.fori_loop(0, qi, lambda ki, st: key_tile(ki, st, False), init)
    state = key_tile(qi, state, True)
    for h in range(ATTN_HEADS_PER_STEP):
        _, l_fin, acc_fin = state[h]
        o_ref[:, h * MLA_V:(h + 1) * MLA_V] = (acc_fin / l_fin).astype(o_ref.dtype)


def _attn_prompt(q, k, v, n_batch, seq_len, tq):
    assert tq % CHUNK == 0
    nq = seq_len // tq
    hp = ATTN_HEADS_PER_STEP
    return _rows_call(
        functools.partial(_attn_prompt_kernel, tq=tq), None,
        grid=(n_batch, MLA_HEADS // hp, nq),
        in_specs=[
            pl.BlockSpec((tq, hp * MLA_HEAD_PAD), lambda b, h, qi: (b * nq + qi, h)),
            pl.BlockSpec((seq_len, hp * MLA_HEAD_PAD), lambda b, h, qi: (b, h)),
            pl.BlockSpec((seq_len, hp * MLA_V), lambda b, h, qi: (b, h)),
        ],
        out_specs=pl.BlockSpec((tq, hp * MLA_V), lambda b, h, qi: (b * nq + qi, h)),
        out_shape=jax.ShapeDtypeStruct((N_ROWS, MLA_HEADS * MLA_V), BF16),
        compiler_params=_cparams(("parallel", "parallel", "arbitrary")),
        name="attn_prompt",
        args=(q, k, v),
    )


def _attn_sample_kernel(q_ref, kn_ref, vn_ref, c_ref, kpe_ref, cos_ref, sin_ref, wk_ref, wv_ref, gain_ref, sel_ref,
                        o_ref):
    c = c_ref[...].astype(BF16)
    k_raw = _dot(c, wk_ref[...])
    v_all = _dot(c, wv_ref[...]).astype(BF16)
    gain = gain_ref[...]
    gain_n = gain[:, :MLA_NOPE]
    kpe = kpe_ref[...]
    pe = kpe * gain[:, MLA_NOPE:]
    lane = lax.broadcasted_iota(jnp.int32, pe.shape, 1)
    half = MLA_ROPE // 2
    swap = jnp.where(lane < half, pltpu.roll(pe, LANES - half, 1), pltpu.roll(pe, half, 1))
    pe = (pe * cos_ref[...] + swap * sin_ref[...]).astype(BF16)
    ss = (_dot_nt(sel_ref[...], (k_raw * k_raw).astype(BF16))
          + _dot_nt(jnp.ones((MLA_HEADS, LANES), BF16), (kpe * kpe).astype(BF16)))
    r = lax.rsqrt(ss * (1.0 / MLA_QK) + RMS_EPS)
    k_raw = k_raw.astype(BF16)
    for h in range(MLA_HEADS):
        ks = slice(h * MLA_HEAD_PAD, (h + 1) * MLA_HEAD_PAD)
        vs = slice(h * MLA_V, (h + 1) * MLA_V)
        q = q_ref[:, ks]
        q_n = (q[:, :MLA_NOPE].astype(F32) * gain_n).astype(BF16)
        s_p = (_dot_nt(q_n, k_raw[:, vs]) + _dot_nt(q[:, MLA_NOPE:], pe)) * r[h:h + 1]
        s_n = _dot_nt(q, kn_ref[:, ks])
        m = jnp.maximum(jnp.max(s_p, axis=-1, keepdims=True), jnp.max(s_n, axis=-1, keepdims=True))
        p_p = jnp.exp(s_p - m)
        p_n = jnp.exp(s_n - m)
        l = jnp.sum(p_p, axis=-1, keepdims=True) + jnp.sum(p_n, axis=-1, keepdims=True)
        o = _dot(p_p.astype(BF16), v_all[:, vs]) + _dot(p_n.astype(BF16), vn_ref[:, vs])
        o_ref[:, vs] = (o / l).astype(o_ref.dtype)


def _attn_sample(q, k_new, v_new, ckv_all, kpe_all, past0, cos, sin, w_uk, w_uv, gain_pad, row0, into):
    assert (PAST_LEN + DEC_SEQ - 1) // CHUNK <= PAST_LEN // CHUNK
    base = row0 // DEC_SEQ
    new = lambda b: (base + b, 0)
    past = lambda b: (past0 + b, 0)
    c2 = lambda b: (0, 0)
    width = MLA_HEADS * MLA_NOPE
    sel = jnp.repeat(jnp.eye(MLA_HEADS, dtype=F32), MLA_NOPE, axis=1).astype(BF16)
    return _rows_call(
        _attn_sample_kernel, into,
        grid=(DEC_BATCH,),
        in_specs=[
            pl.BlockSpec((DEC_SEQ, MLA_QK_PAD), new),
            pl.BlockSpec((DEC_SEQ, MLA_QK_PAD), new),
            pl.BlockSpec((DEC_SEQ, MLA_HEADS * MLA_V), new),
            pl.BlockSpec((PAST_LEN, MLA_KV_LORA), past),
            pl.BlockSpec((PAST_LEN, LANES), past),
            pl.BlockSpec((PAST_LEN, LANES), c2),
            pl.BlockSpec((PAST_LEN, LANES), c2),
            pl.BlockSpec((MLA_KV_LORA, width), c2),
            pl.BlockSpec((MLA_KV_LORA, MLA_HEADS * MLA_V), c2),
            pl.BlockSpec((1, MLA_HEAD_PAD), c2),
            pl.BlockSpec((MLA_HEADS, width), c2),
        ],
        out_specs=pl.BlockSpec((DEC_SEQ, MLA_HEADS * MLA_V), new),
        out_shape=jax.ShapeDtypeStruct((N_ROWS, MLA_HEADS * MLA_V), BF16),
        compiler_params=_cparams(("parallel",)),
        name="attn_sample",
        args=(q, k_new, v_new, ckv_all, kpe_all, cos, sin, w_uk, w_uv, gain_pad, sel),
    )


def _retention_tables(block):
    log_g = jnp.log1p(-jnp.exp2(-5.0 - jnp.arange(RET_HEADS, dtype=F32)))
    idx = jnp.arange(block, dtype=F32)
    diff = idx[:, None] - idx[None, :]
    decay = jnp.where(diff >= 0, jnp.exp(log_g[:, None, None] * jnp.maximum(diff, 0.0)), 0.0)
    q_decay = jnp.exp(log_g[:, None] * (idx + 1.0))
    k_decay = jnp.exp(log_g[:, None] * (block - 1.0 - idx))
    blk_decay = jnp.exp(log_g * block)
    return (decay,
            jnp.broadcast_to(q_decay[:, :, None], (RET_HEADS, block, RET_DK)),
            jnp.broadcast_to(k_decay[:, :, None], (RET_HEADS, block, RET_DK)),
            jnp.broadcast_to(blk_decay[:, None, None], (RET_HEADS, 1, RET_DV)))


def _rope128(x, cos, sin):
    return x * cos + pltpu.roll(x, RET_DK // 2, 1) * sin


def _retention_kernel(rq_ref, rk_ref, rv_ref, rg_ref, cos_ref, sin_ref, dec_ref, qd_ref, kd_ref, bd_ref, s0_ref,
                      o_ref, s_out_ref, s_ref):
    @pl.when(pl.program_id(1) == 0)
    def _():
        s_ref[...] = s0_ref[0]

    cos = cos_ref[...]
    sin = sin_ref[...]
    for h in range(RET_HEADS):
        ks = slice(h * RET_DK, (h + 1) * RET_DK)
        vs = slice(h * RET_DV, (h + 1) * RET_DV)
        q = _rope128(rq_ref[:, ks].astype(F32), cos, sin) * (RET_DK ** -0.5)
        k = _rope128(rk_ref[:, ks].astype(F32), cos, sin)
        v = rv_ref[:, vs]
        s_old = s_ref[h]
        att = _dot_nt(q.astype(BF16), k.astype(BF16)) * dec_ref[h]
        o = _dot(att.astype(BF16), v) + _dot((q * qd_ref[h]).astype(BF16), s_old.astype(BF16))
        kd = (k * kd_ref[h]).astype(BF16)
        s_new = bd_ref[h] * s_old + lax.dot_general(kd, v, (((0,), (0,)), ((), ())), preferred_element_type=F32)
        s_ref[h] = s_new
        s_out_ref[0, h] = s_new
        mu = jnp.mean(o, axis=-1, keepdims=True)
        c = o - mu
        o = c * lax.rsqrt(jnp.mean(c * c, axis=-1, keepdims=True) + RMS_EPS)
        g = rg_ref[:, vs].astype(F32)
        o_ref[:, vs] = (o * (g * _sigmoid(g))).astype(o_ref.dtype)


def _retention_branch(proj, cos, sin, tab_block, s0, s0_base, n_seq, seq_len, block, row0, into=None):
    nblk = seq_len // block
    base = row0 // block
    dec, qd, kd, bd = _retention_tables(block)
    rows = lambda b, t: base + b * nblk + t
    tab = lambda b, t: (tab_block(t), 0)
    all3 = lambda b, t: (0, 0, 0)
    qk_w = RET_HEADS * RET_DK
    v_w = RET_HEADS * RET_DV
    out, s_last = _rows_call(
        _retention_kernel, into,
        grid=(n_seq, nblk),
        in_specs=[
            pl.BlockSpec((block, qk_w), lambda b, t: (rows(b, t), COL_RQ // qk_w)),
            pl.BlockSpec((block, qk_w), lambda b, t: (rows(b, t), COL_RK // qk_w)),
            pl.BlockSpec((block, v_w), lambda b, t: (rows(b, t), COL_RV // v_w)),
            pl.BlockSpec((block, v_w), lambda b, t: (rows(b, t), COL_RG // v_w)),
            pl.BlockSpec((block, LANES), tab),
            pl.BlockSpec((block, LANES), tab),
            pl.BlockSpec((RET_HEADS, block, block), all3),
            pl.BlockSpec((RET_HEADS, block, RET_DK), all3),
            pl.BlockSpec((RET_HEADS, block, RET_DK), all3),
            pl.BlockSpec((RET_HEADS, 1, RET_DV), all3),
            pl.BlockSpec((1, RET_HEADS, RET_DK, RET_DV), lambda b, t: (s0_base + b, 0, 0, 0)),
        ],
        out_specs=[
            pl.BlockSpec((block, v_w), lambda b, t: (rows(b, t), 0)),
            pl.BlockSpec((1, RET_HEADS, RET_DK, RET_DV), lambda b, t: (b, 0, 0, 0)),
        ],
        out_shape=[
            jax.ShapeDtypeStruct((N_ROWS, v_w), BF16),
            jax.ShapeDtypeStruct((n_seq, RET_HEADS, RET_DK, RET_DV), F32),
        ],
        scratch_shapes=[pltpu.VMEM((RET_HEADS, RET_DK, RET_DV), F32)],
        compiler_params=_cparams(("parallel", "arbitrary")),
        name="retention",
        args=(proj, proj, proj, proj, cos, sin, dec, qd, kd, bd, s0),
    )
    return out, s_last


def _merge_kernel(a_ref, b_ref, c_ref, w_ref, g0_ref, g1_ref, g2_ref, o_ref):
    acc = _sigmoid(g0_ref[...].astype(F32)) * _dot(a_ref[...], w_ref[0])
    acc += _sigmoid(g1_ref[...].astype(F32)) * _dot(b_ref[...], w_ref[1])
    acc += _sigmoid(g2_ref[...].astype(F32)) * _dot(c_ref[...], w_ref[2])
    o_ref[...] = acc.astype(o_ref.dtype)


def _merge(o_a, o_b, o_c, w_branch, proj, tm, tn):
    n = o_a.shape[0]
    nj = D_MODEL // tn
    br = pl.BlockSpec((tm, BRANCH_WIDTH), lambda i, j: (i, 0))
    gate = lambda b: pl.BlockSpec((tm, tn), lambda i, j: (i, COL_GMIX // tn + b * nj + j))
    return pl.pallas_call(
        _merge_kernel,
        grid=(n // tm, nj),
        in_specs=[br, br, br,
                  pl.BlockSpec((N_BRANCH, BRANCH_WIDTH, tn), lambda i, j: (0, 0, j)),
                  gate(0), gate(1), gate(2)],
        out_specs=pl.BlockSpec((tm, tn), lambda i, j: (i, j)),
        out_shape=jax.ShapeDtypeStruct((n, D_MODEL), BF16),
        compiler_params=_cparams(("parallel", "arbitrary")),
        name="branch_merge",
    )(o_a, o_b, o_c, w_branch, proj, proj, proj)


def _memkv_kernel(m_ref, g_ref, wk_ref, wv_ref, kg_ref, k_ref, v_ref):
    mn = _rms_rows(m_ref[...], g_ref[...]).astype(BF16)
    kk = _dot(mn, wk_ref[...])
    for h in range(MEM_HEADS):
        sl = slice(h * MEM_HEAD_DIM, (h + 1) * MEM_HEAD_DIM)
        k_ref[:, sl] = _rms_rows(kk[:, sl], kg_ref[...])
    v_ref[...] = _dot(mn, wv_ref[...])


def _memkv(mem, g, wk, wv, k_gain, tm):
    n, k = mem.shape
    c2 = lambda i: (0, 0)
    return pl.pallas_call(
        _memkv_kernel,
        grid=(n // tm,),
        in_specs=[
            pl.BlockSpec((tm, k), lambda i: (i, 0)),
            pl.BlockSpec((1, k), c2),
            pl.BlockSpec((k, MEM_WIDTH), c2),
            pl.BlockSpec((k, MEM_WIDTH), c2),
            pl.BlockSpec((1, MEM_HEAD_DIM), c2),
        ],
        out_specs=[pl.BlockSpec((tm, MEM_WIDTH), lambda i: (i, 0)), pl.BlockSpec((tm, MEM_WIDTH), lambda i: (i, 0))],
        out_shape=[jax.ShapeDtypeStruct((n, MEM_WIDTH), F32), jax.ShapeDtypeStruct((n, MEM_WIDTH), F32)],
        compiler_params=_cparams(("parallel",)),
        name="memory_kv",
    )(mem, g.reshape(1, k), wk, wv, k_gain.reshape(1, MEM_HEAD_DIM))


def _cross_attn_kernel(q_ref, k_ref, v_ref, qg_ref, o_ref, *, head_axis):
    for h in range(MEM_HEADS):
        sl = slice(h * MEM_HEAD_DIM, (h + 1) * MEM_HEAD_DIM)
        if head_axis:
            k, v = k_ref[0, :, h, :], v_ref[0, :, h, :]
        else:
            k, v = k_ref[0, :, sl], v_ref[0, :, sl]
        q = _rms_rows(q_ref[:, sl].astype(F32), qg_ref[...]) * (MEM_HEAD_DIM ** -0.5)
        s = _dot_nt(q.astype(BF16), k.astype(BF16))
        p = jnp.exp(s - jnp.max(s, axis=-1, keepdims=True))
        l = jnp.sum(p, axis=-1, keepdims=True)
        o = _dot(p.astype(BF16), v.astype(BF16))
        o_ref[:, sl] = (o / l).astype(o_ref.dtype)


def _cross_attn(qc, mem_k, mem_v, mem0, q_gain, n_batch, rows_per_batch, tm, row0, into=None):
    tiles = rows_per_batch // tm
    base = row0 // tm
    head_axis = mem_k.ndim == 4
    if head_axis:
        mem_spec = pl.BlockSpec((1, N_MEM, MEM_HEADS, MEM_HEAD_DIM), lambda b, t: (mem0 + b, 0, 0, 0))
    else:
        mem_spec = pl.BlockSpec((1, N_MEM, MEM_WIDTH), lambda b, t: (mem0 + b, 0, 0))
    return _rows_call(
        functools.partial(_cross_attn_kernel, head_axis=head_axis), into,
        grid=(n_batch, tiles),
        in_specs=[
            pl.BlockSpec((tm, MEM_WIDTH), lambda b, t: (base + b * tiles + t, 0)),
            mem_spec, mem_spec,
            pl.BlockSpec((1, MEM_HEAD_DIM), lambda b, t: (0, 0)),
        ],
        out_specs=pl.BlockSpec((tm, MEM_WIDTH), lambda b, t: (base + b * tiles + t, 0)),
        out_shape=jax.ShapeDtypeStruct((N_ROWS, MEM_WIDTH), BF16),
        compiler_params=_cparams(("parallel", "arbitrary")),
        name="cross_attn",
        args=(qc, mem_k, mem_v, q_gain.reshape(1, MEM_HEAD_DIM)),
    )


ROUTER_LANES = LANES
FAR_LANE = 4 * LANES


def _split_bf16(x):
    hi = x.astype(BF16)
    lo = (x - hi.astype(F32)).astype(BF16)
    return hi, lo


def _router_kernel(x_ref, g_ref, w_ref, b_ref, xn_ref, ids_ref, gates_ref):
    xn = _rms_rows(x_ref[...], g_ref[...])
    xn_ref[...] = xn
    xh, xl = _split_bf16(xn)
    wh, wl = _split_bf16(w_ref[...])
    logits = _dot(xh, wh) + (_dot(xh, wl) + _dot(xl, wh)) + b_ref[...]
    lane = lax.broadcasted_iota(jnp.int32, logits.shape, 1)

    def first_max(vals):
        top = jnp.max(vals, axis=-1, keepdims=True)
        idx = jnp.min(jnp.where(vals == top, lane, FAR_LANE), axis=-1, keepdims=True)
        return top, idx

    g_mask = jnp.logical_and(lane >= N_EXPERTS, lane < N_EXPERTS + MOE_GROUPS)
    g_top, g_lane = first_max(jnp.where(g_mask, logits, NEG_INF))
    p_group = 1.0 / jnp.sum(jnp.where(g_mask, jnp.exp(logits - g_top), 0.0), axis=-1, keepdims=True)
    g_sel = g_lane - N_EXPERTS
    e_mask = jnp.logical_and(lane < N_EXPERTS, lane // MOE_PER_GROUP == g_sel)
    e_vals = jnp.where(e_mask, logits, NEG_INF)
    e1_top, e1 = first_max(e_vals)
    e2_top, e2 = first_max(jnp.where(lane == e1, NEG_INF, e_vals))
    t = jnp.exp(e2_top - e1_top)
    gate1 = p_group / (1.0 + t)
    gate2 = p_group * t / (1.0 + t)
    ids_ref[...] = jnp.where(lane == 0, e1, jnp.where(lane == 1, e2, 0))
    gates_ref[...] = jnp.where(lane == 0, gate1, jnp.where(lane == 1, gate2, 0.0))


def _router(x, g, w_router, b_router, tm):
    n, k = x.shape
    c2 = lambda i: (0, 0)
    row = lambda i: (i, 0)
    return pl.pallas_call(
        _router_kernel,
        grid=(n // tm,),
        in_specs=[
            pl.BlockSpec((tm, k), row),
            pl.BlockSpec((1, k), c2),
            pl.BlockSpec((k, ROUTER_LANES), c2),
            pl.BlockSpec((1, ROUTER_LANES), c2),
        ],
        out_specs=[pl.BlockSpec((tm, k), row), pl.BlockSpec((tm, ROUTER_LANES), row),
                   pl.BlockSpec((tm, ROUTER_LANES), row)],
        out_shape=[jax.ShapeDtypeStruct((n, k), F32), jax.ShapeDtypeStruct((n, ROUTER_LANES), jnp.int32),
                   jax.ShapeDtypeStruct((n, ROUTER_LANES), F32)],
        compiler_params=_cparams(("parallel",)),
        name="moe_router",
    )(x, g.reshape(1, k), w_router, b_router)


def _route_plan(expert_idx):
    flat_e = expert_idx.reshape(N_PAIRS)
    onehot = (flat_e[:, None] == jnp.arange(N_EXPERTS, dtype=jnp.int32)[None, :]).astype(jnp.int32)
    csum = jnp.cumsum(onehot, axis=0)
    rank = jnp.take_along_axis(csum, flat_e[:, None], axis=1)[:, 0] - 1
    counts = csum[-1]
    padded = (counts + EXPERT_ROWS - 1) // EXPERT_ROWS * EXPERT_ROWS
    pad_ends = jnp.cumsum(padded)
    pad_starts = pad_ends - padded
    dest = (pad_starts[flat_e] + rank).astype(jnp.int32)
    n_used = (pad_ends[-1] // EXPERT_ROWS).astype(jnp.int32).reshape(1)
    block_start = jnp.arange(N_EXPERT_BLOCKS, dtype=jnp.int32) * EXPERT_ROWS
    block_e = jnp.sum((pad_ends[None, :] <= block_start[:, None]).astype(jnp.int32), axis=1)
    block_e = jnp.minimum(block_e, N_EXPERTS - 1)
    src = jnp.zeros((N_EXPERT_BLOCKS * EXPERT_ROWS,), jnp.int32).at[dest].set(
        jnp.arange(N_PAIRS, dtype=jnp.int32) // MOE_TOP_K)
    return dest, src, block_e, n_used


DMA_UNROLL = 8


def _experts_kernel(src_ref, be_ref, nu_ref, x_hbm, wgu_ref, wd_ref, o_ref, xbuf_ref, sem):
    i = pl.program_id(0)
    n_used = nu_ref[0]
    slot = lax.rem(i, 2)

    def row_copy(blk, buf, c, u):
        return pltpu.make_async_copy(x_hbm.at[pl.ds(src_ref[blk * EXPERT_ROWS + c * DMA_UNROLL + u], 1)],
                                     xbuf_ref.at[buf, c, pl.ds(u, 1)], sem.at[buf])

    def start_block(blk, buf):
        def body(c, carry):
            for u in range(DMA_UNROLL):
                row_copy(blk, buf, c, u).start(priority=u % 2)
            return carry
        lax.fori_loop(0, EXPERT_ROWS // DMA_UNROLL, body, 0)

    def wait_block(blk, buf):
        def body(c, carry):
            for u in range(DMA_UNROLL):
                row_copy(blk, buf, c, u).wait()
            return carry
        lax.fori_loop(0, EXPERT_ROWS // DMA_UNROLL, body, 0)

    @pl.when(jnp.logical_and(i == 0, n_used > 0))
    def _():
        start_block(0, 0)

    @pl.when(i + 1 < n_used)
    def _():
        start_block(i + 1, 1 - slot)

    @pl.when(i < n_used)
    def _():
        wait_block(i, slot)
        x = xbuf_ref[slot].reshape(EXPERT_ROWS, D_MODEL)
        gu = _dot(x.astype(BF16), wgu_ref[0, 0].astype(BF16))
        g = gu[:, :EXPERT_FF]
        h = (g * _sigmoid(g)) * gu[:, EXPERT_FF:]
        o_ref[...] = _dot(h.astype(BF16), wd_ref[0, 0].astype(BF16))

    @pl.when(i >= n_used)
    def _():
        o_ref[...] = jnp.zeros(o_ref.shape, o_ref.dtype)


def _experts(xn, src, block_e, n_used, w_gate_up, w_down, layer):
    return pl.pallas_call(
        _experts_kernel,
        grid_spec=pltpu.PrefetchScalarGridSpec(
            num_scalar_prefetch=3,
            grid=(N_EXPERT_BLOCKS,),
            in_specs=[
                pl.BlockSpec(memory_space=pl.ANY),
                pl.BlockSpec((1, 1, D_MODEL, 2 * EXPERT_FF), lambda i, src, be, nu: (layer, be[i], 0, 0)),
                pl.BlockSpec((1, 1, EXPERT_FF, D_MODEL), lambda i, src, be, nu: (layer, be[i], 0, 0)),
            ],
            out_specs=pl.BlockSpec((EXPERT_ROWS, D_MODEL), lambda i, src, be, nu: (i, 0)),
            scratch_shapes=[pltpu.VMEM((2, EXPERT_ROWS // DMA_UNROLL, DMA_UNROLL, D_MODEL), F32),
                            pltpu.SemaphoreType.DMA((2,))],
        ),
        out_shape=jax.ShapeDtypeStruct((N_EXPERT_BLOCKS * EXPERT_ROWS, D_MODEL), F32),
        compiler_params=_cparams(("arbitrary",)),
        name="moe_experts",
    )(src, block_e, n_used, xn, w_gate_up, w_down)


def _combine_kernel(dest_ref, x_ref, gates_ref, y_hbm, o_ref, buf_ref, sem, *, rows):
    i = pl.program_id(0)
    slot = lax.rem(i, 2)
    per_iter = DMA_UNROLL

    def row_copy(tile, buf, c, u, k):
        return pltpu.make_async_copy(
            y_hbm.at[pl.ds(dest_ref[(tile * rows + c * per_iter + u) * MOE_TOP_K + k], 1)],
            buf_ref.at[buf, k, c, pl.ds(u, 1)], sem.at[buf])

    def start_tile(tile, buf):
        def body(c, carry):
            for u in range(per_iter):
                for k in range(MOE_TOP_K):
                    row_copy(tile, buf, c, u, k).start(priority=k)
            return carry
        lax.fori_loop(0, rows // per_iter, body, 0)

    def wait_tile(tile, buf):
        def body(c, carry):
            for u in range(per_iter):
                for k in range(MOE_TOP_K):
                    row_copy(tile, buf, c, u, k).wait()
            return carry
        lax.fori_loop(0, rows // per_iter, body, 0)

    @pl.when(i == 0)
    def _():
        start_tile(0, 0)

    @pl.when(i + 1 < pl.num_programs(0))
    def _():
        start_tile(i + 1, 1 - slot)

    wait_tile(i, slot)
    gates = gates_ref[...]
    y0 = buf_ref[slot, 0].reshape(x_ref.shape)
    y1 = buf_ref[slot, 1].reshape(x_ref.shape)
    o_ref[...] = x_ref[...] + (gates[:, 0:1] * y0 + gates[:, 1:2] * y1)


def _combine(x, gates, y_sorted, dest, rows):
    n, width = x.shape
    return pl.pallas_call(
        functools.partial(_combine_kernel, rows=rows),
        grid_spec=pltpu.PrefetchScalarGridSpec(
            num_scalar_prefetch=1,
            grid=(n // rows,),
            in_specs=[
                pl.BlockSpec((rows, width), lambda i, d: (i, 0)),
                pl.BlockSpec((rows, ROUTER_LANES), lambda i, d: (i, 0)),
                pl.BlockSpec(memory_space=pl.ANY),
            ],
            out_specs=pl.BlockSpec((rows, width), lambda i, d: (i, 0)),
            scratch_shapes=[pltpu.VMEM((2, MOE_TOP_K, rows // DMA_UNROLL, DMA_UNROLL, width), F32),
                            pltpu.SemaphoreType.DMA((2,))],
        ),
        out_shape=jax.ShapeDtypeStruct((n, width), F32),
        compiler_params=_cparams(("arbitrary",)),
        name="moe_combine",
    )(dest, x, gates, y_sorted)


def _hier_moe(x, lp, w_gate_up, w_down, layer):
    w_router = jnp.concatenate(
        [lp['w_expert_router'], lp['w_group_router'],
         jnp.zeros((D_MODEL, ROUTER_LANES - N_EXPERTS - MOE_GROUPS), F32)], axis=1)
    b_router = jnp.concatenate(
        [lp['b_expert'], lp['b_group'], jnp.zeros((ROUTER_LANES - N_EXPERTS - MOE_GROUPS,), F32)]).reshape(1, -1)
    xn, ids, gates = _router(x, lp['norm_ffn'], w_router, b_router, 512)
    dest, src, block_e, n_used = _route_plan(ids[:, :MOE_TOP_K])
    y_sorted = _experts(xn, src, block_e, n_used, w_gate_up, w_down, layer)
    return _combine(x, gates, y_sorted, dest, 256)


def _cols(w, *ranges):
    return jnp.concatenate([w[:, a:b] for a, b in ranges], axis=1)


def _prompt_or_sample_block(i, tm):
    return jnp.where(i < N_PROMPT // tm, lax.rem(i, SEQ // tm), SEQ // tm)


def _layer(x, mem_prompt, lp, layer, caches, tables):
    (ckv_all, kpe_past_all, h0_re, h0_im, ret_s0_all, mem_k_all, mem_v_all, w_gate_up, w_down) = caches
    cos_mla, sin_mla, cos_past, sin_past, cos_ret, sin_ret = tables
    tm = 512
    w_in = lp['w_in']
    w_main = _cols(w_in, IN_GMIX, IN_S5, IN_RV, IN_RG, IN_QLAT, IN_KVLAT, IN_RQ, IN_RK).astype(BF16)
    w_kpe = jnp.pad(_cols(w_in, IN_KPE), ((0, 0), (0, LANES - MLA_ROPE))).astype(BF16)
    proj, kpe = _rms_proj(x, lp['norm_mix'], w_main, BF16, 1024, 1024, w2=w_kpe, out2_dtype=F32)

    s5_tabs = _s5_tables(lp)
    w_glu = lp['s5_w_glu'].astype(BF16)
    zeros_s5 = jnp.zeros((BATCH, S5_GROUPS, S5_STATE), F32)
    a, p_s5_re, p_s5_im = _s5_branch(proj, s5_tabs, w_glu, zeros_s5, zeros_s5, BATCH, SEQ, 256, 0)
    a, s_s5_re, s_s5_im = _s5_branch(proj, s5_tabs, w_glu, h0_re, h0_im, DEC_BATCH, DEC_SEQ, 256, N_PROMPT, into=a)

    w_uq = lp['mla_w_uq'].reshape(MLA_Q_LORA, MLA_HEADS, MLA_QK)
    w_uq = jnp.pad(w_uq, ((0, 0), (0, 0), (0, MLA_HEAD_PAD - MLA_QK))).reshape(MLA_Q_LORA, MLA_QK_PAD).astype(BF16)
    w_ukv = lp['mla_w_ukv'].astype(BF16)
    q_gain = _pad_head_vec(lp['mla_q_gain'])
    k_gain = _pad_head_vec(lp['mla_k_gain'])
    row_tab = lambda i: _prompt_or_sample_block(i, tm)
    q = _mla_q(proj, lp['mla_q_norm'], w_uq, q_gain, cos_mla, sin_mla, row_tab, tm)
    c_kv, k_new, v_new = _mla_kv(proj, lp['mla_kv_norm'], w_ukv, k_gain, kpe, cos_mla, sin_mla, row_tab, tm)
    w_ukv_heads = w_ukv.reshape(MLA_KV_LORA, MLA_HEADS, MLA_NOPE + MLA_V)
    w_uk = w_ukv_heads[:, :, :MLA_NOPE].reshape(MLA_KV_LORA, MLA_HEADS * MLA_NOPE)
    w_uv = w_ukv_heads[:, :, MLA_NOPE:].reshape(MLA_KV_LORA, MLA_HEADS * MLA_V)
    b = _attn_prompt(q, k_new, v_new, BATCH, SEQ, 512)
    b = _attn_sample(q, k_new, v_new, ckv_all, kpe_past_all, layer * DEC_BATCH, cos_past, sin_past, w_uk, w_uv,
                     k_gain, N_PROMPT, into=b)

    zeros_ret = jnp.zeros((BATCH, RET_HEADS, RET_DK, RET_DV), F32)
    c, p_ret = _retention_branch(proj, cos_ret, sin_ret, lambda t: t, zeros_ret, 0, BATCH, SEQ, 256, 0)
    c, s_ret = _retention_branch(proj, cos_ret, sin_ret, lambda t: SEQ // DEC_SEQ, ret_s0_all, layer * DEC_BATCH,
                                 DEC_BATCH, DEC_SEQ, DEC_SEQ, N_PROMPT, into=c)

    merged = _merge(a, b, c, lp['w_branch'].astype(BF16), proj, 1024, 512)
    x = _res_matmul(merged, lp['w_out'].astype(BF16), x, 1024, 1024)

    mem_k_p, mem_v_p = _memkv(mem_prompt, lp['norm_mem'], lp['w_ck'].astype(BF16), lp['w_cv'].astype(BF16),
                              lp['cross_k_gain'], 256)
    qc = _rms_proj(x, lp['norm_cross'], lp['w_cq'].astype(BF16), BF16, 512, MEM_WIDTH)[0]
    o = _cross_attn(qc, mem_k_p.reshape(BATCH, N_MEM, MEM_WIDTH), mem_v_p.reshape(BATCH, N_MEM, MEM_WIDTH), 0,
                    lp['cross_q_gain'], BATCH, SEQ, 512, 0)
    o = _cross_attn(qc, mem_k_all, mem_v_all, layer * DEC_BATCH, lp['cross_q_gain'], DEC_BATCH, DEC_SEQ, DEC_SEQ,
                    N_PROMPT, into=o)
    x = _res_matmul(o, lp['w_co'].astype(BF16), x, 1024, 1024)

    x = _hier_moe(x, lp, w_gate_up, w_down, layer)

    kpe_rows = kpe[:, :MLA_ROPE]
    state_p = (c_kv[:N_PROMPT].reshape(BATCH, SEQ, MLA_KV_LORA), kpe_rows[:N_PROMPT].reshape(BATCH, SEQ, MLA_ROPE),
               p_s5_re, p_s5_im, p_ret,
               mem_k_p.reshape(BATCH, N_MEM, MEM_HEADS, MEM_HEAD_DIM),
               mem_v_p.reshape(BATCH, N_MEM, MEM_HEADS, MEM_HEAD_DIM))
    state_s = (c_kv[N_PROMPT:].reshape(DEC_BATCH, DEC_SEQ, MLA_KV_LORA),
               kpe_rows[N_PROMPT:].reshape(DEC_BATCH, DEC_SEQ, MLA_ROPE), s_s5_re, s_s5_im, s_ret)
    return x, state_p, state_s


_LAYER_PARAMS = (
    'norm_mix', 'w_in', 's5_a_re', 's5_a_im', 's5_log_dt', 's5_b_re', 's5_b_im', 's5_c_re', 's5_c_im', 's5_d',
    's5_w_glu', 'mla_q_norm', 'mla_w_uq', 'mla_kv_norm', 'mla_w_ukv', 'mla_q_gain', 'mla_k_gain', 'w_branch',
    'w_out', 'norm_cross', 'norm_mem', 'w_cq', 'w_ck', 'w_cv', 'cross_q_gain', 'cross_k_gain', 'w_co', 'norm_ffn',
    'w_group_router', 'b_group', 'w_expert_router', 'b_expert')


def kernel(x_prompt, x_sample, mem_prompt, cache_mla_ckv, cache_mla_kpe, state_s5_re, state_s5_im, state_ret,
           cache_mem_k, cache_mem_v, norm_mix, w_in, s5_a_re, s5_a_im, s5_log_dt, s5_b_re, s5_b_im, s5_c_re,
           s5_c_im, s5_d, s5_w_glu, mla_q_norm, mla_w_uq, mla_kv_norm, mla_w_ukv, mla_q_gain, mla_k_gain,
           w_branch, w_out, norm_cross, norm_mem, w_cq, w_ck, w_cv, cross_q_gain, cross_k_gain, w_co, norm_ffn,
           w_group_router, b_group, w_expert_router, b_expert, w_gate_up, w_down):
    params = dict(zip(_LAYER_PARAMS, (
        norm_mix, w_in, s5_a_re, s5_a_im, s5_log_dt, s5_b_re, s5_b_im, s5_c_re, s5_c_im, s5_d, s5_w_glu,
        mla_q_norm, mla_w_uq, mla_kv_norm, mla_w_ukv, mla_q_gain, mla_k_gain, w_branch, w_out, norm_cross,
        norm_mem, w_cq, w_ck, w_cv, cross_q_gain, cross_k_gain, w_co, norm_ffn, w_group_router, b_group,
        w_expert_router, b_expert)))
    assert x_prompt.shape == (BATCH, SEQ, D_MODEL) and x_sample.shape == (DEC_BATCH, DEC_SEQ, D_MODEL)
    assert cache_mla_ckv.shape == (DEPTH, DEC_BATCH, PAST_LEN, MLA_KV_LORA)

    pos_p = jnp.arange(SEQ, dtype=jnp.int32)
    pos_s = PAST_LEN + jnp.arange(DEC_SEQ, dtype=jnp.int32)
    tile_rows = 512
    tables = (_rope_tables(jnp.concatenate([pos_p, jnp.tile(pos_s, tile_rows // DEC_SEQ)]), MLA_ROPE)
              + _rope_tables(jnp.arange(PAST_LEN, dtype=jnp.int32), MLA_ROPE)
              + _rope_tables(jnp.concatenate([pos_p, pos_s]), RET_DK))

    x = jnp.concatenate([x_prompt.reshape(N_PROMPT, D_MODEL), x_sample.reshape(N_SAMPLE, D_MODEL)], axis=0)
    mem2d = mem_prompt.reshape(BATCH * N_MEM, D_MODEL)
    n_past = DEPTH * DEC_BATCH * PAST_LEN
    ckv_all = cache_mla_ckv.reshape(n_past, MLA_KV_LORA)
    kpe_past_all = jnp.pad(cache_mla_kpe.reshape(n_past, MLA_ROPE), ((0, 0), (0, LANES - MLA_ROPE)))
    ret_s0_all = state_ret.reshape(DEPTH * DEC_BATCH, RET_HEADS, RET_DK, RET_DV)
    mem_k_all = cache_mem_k.reshape(DEPTH * DEC_BATCH, N_MEM, MEM_HEADS, MEM_HEAD_DIM)
    mem_v_all = cache_mem_v.reshape(DEPTH * DEC_BATCH, N_MEM, MEM_HEADS, MEM_HEAD_DIM)
    outs_p, outs_s = [], []
    for l in range(DEPTH):
        lp = {name: value[l] for name, value in params.items()}
        caches = (ckv_all, kpe_past_all, state_s5_re[l], state_s5_im[l], ret_s0_all, mem_k_all, mem_v_all,
                  w_gate_up, w_down)
        x, st_p, st_s = _layer(x, mem2d, lp, l, caches, tables)
        outs_p.append(st_p)
        outs_s.append(st_s)
    stack = lambda outs, i: jnp.stack([o[i] for o in outs])
    return ((x[:N_PROMPT].reshape(BATCH, SEQ, D_MODEL), x[N_PROMPT:].reshape(DEC_BATCH, DEC_SEQ, D_MODEL))
            + tuple(stack(outs_p, i) for i in range(7)) + tuple(stack(outs_s, i) for i in range(5)))
```
